```python
import math
import jax
import jax.numpy as jnp
from jax import lax
import numpy as np

D_MODEL = 2048
BATCH = 2
SEQ = 4096
DEPTH = 4

GRID_W = 64
CTX_LEN = 256
NORM_EPS = 1e-6
N_BRANCH = 3

ATT_HEADS = 8
ATT_KV_HEADS = 2
HEAD_DIM = 128
ATT_W = ATT_HEADS * HEAD_DIM
ATT_KV_W = ATT_KV_HEADS * HEAD_DIM
AXIS_ROPE_DIM = HEAD_DIM // 2
ROPE_THETA = 10000.0
Q_BLOCK = 128

HY_W = 1024
HY_ORDER = 2
HY_SHORT = 3
HY_BANDS = 16
HY_EMB = 1 + 2 * HY_BANDS
HY_FILTER_HIDDEN = 64
HY_DECAY_TARGET = 1e-2
HY_DECAY_FAST = 0.3
HY_DECAY_SLOW = 1.5

DN_QK_HEADS = 4
DN_V_HEADS = 8
DN_HEAD_DIM = 128
DN_QK_W = DN_QK_HEADS * DN_HEAD_DIM
DN_V_W = DN_V_HEADS * DN_HEAD_DIM
DN_SHORT = 3
DN_CHUNK = 64

IN_WIDTHS = (ATT_W, ATT_KV_W, ATT_KV_W, ATT_W,
             HY_W, HY_W, HY_W, HY_W,
             DN_QK_W, DN_QK_W, DN_V_W, DN_V_W, 2 * DN_V_HEADS, 2 * DN_V_HEADS,
             N_BRANCH * D_MODEL)
IN_W = sum(IN_WIDTHS)

kernel_name = 'hybrid_gqa_hyena_deltanet_dit'


def rms_norm(x, g):
    xf = x.astype(jnp.float32)
    y = xf * lax.rsqrt(jnp.mean(xf * xf, axis=-1, keepdims=True) + NORM_EPS)
    return (y * g.astype(jnp.float32)).astype(x.dtype)


def l2_normalize(x):
    xf = x.astype(jnp.float32)
    return xf * lax.rsqrt(jnp.sum(xf * xf, axis=-1, keepdims=True) + 1e-6)


def centred_depthwise_conv(u, w):
    pad = w.shape[0] // 2
    return lax.conv_general_dilated(u, w[:, None, :].astype(u.dtype), window_strides=(1,),
                                    padding=[(pad, pad)], dimension_numbers=('NWC', 'WIO', 'NWC'),
                                    feature_group_count=u.shape[-1])


def ada_modulation(cond, w_mod, b_mod):
    m = jax.nn.silu(cond) @ w_mod + b_mod
    return jnp.split(m, 3, axis=-1)


def rotate_half_axis(xa, ang):
    x1, x2 = jnp.split(xa, 2, axis=-1)
    cs = jnp.cos(ang)[:, None, :]
    sn = jnp.sin(ang)[:, None, :]
    return jnp.concatenate([x1 * cs - x2 * sn, x1 * sn + x2 * cs], axis=-1)


def apply_axial_rope(x, ang_row, ang_col):
    xf = x.astype(jnp.float32)
    out = jnp.concatenate([rotate_half_axis(xf[..., :AXIS_ROPE_DIM], ang_row),
                           rotate_half_axis(xf[..., AXIS_ROPE_DIM:], ang_col)], axis=-1)
    return out.astype(x.dtype)


def gqa_softmax(q, k, v):
    s = jnp.einsum('bqhgd,bkhd->bhgqk', q, k).astype(jnp.float32) * (HEAD_DIM ** -0.5)
    p = jax.nn.softmax(s, axis=-1).astype(v.dtype)
    return jnp.einsum('bhgqk,bkhd->bqhgd', p, v)


def attention_mixer(q, k, v, qc, kc, vc, q_g, k_g, ang_row, ang_col, need_ctx):
    bsz, n_tok, _ = q.shape
    ctx_len = qc.shape[1]
    grp = ATT_HEADS // ATT_KV_HEADS

    def heads(t, n):
        return t.reshape(t.shape[0], t.shape[1], n, HEAD_DIM)

    q = apply_axial_rope(rms_norm(heads(q, ATT_HEADS), q_g), ang_row, ang_col)
    k = apply_axial_rope(rms_norm(heads(k, ATT_KV_HEADS), k_g), ang_row, ang_col)
    kc = rms_norm(heads(kc, ATT_KV_HEADS), k_g)
    vc = heads(vc, ATT_KV_HEADS)
    k_all = jnp.concatenate([kc, k], axis=1)
    v_all = jnp.concatenate([vc, heads(v, ATT_KV_HEADS)], axis=1)
    n_blk = n_tok // Q_BLOCK
    qb = q.reshape(bsz, n_blk, Q_BLOCK, ATT_KV_HEADS, grp, HEAD_DIM).swapaxes(0, 1)
    ob = lax.map(lambda qi: gqa_softmax(qi, k_all, v_all), qb)
    y = ob.swapaxes(0, 1).reshape(bsz, n_tok, ATT_W)
    yc = None
    if need_ctx:
        qch = rms_norm(heads(qc, ATT_HEADS), q_g).reshape(bsz, ctx_len, ATT_KV_HEADS, grp, HEAD_DIM)
        yc = gqa_softmax(qch, kc, vc).reshape(bsz, ctx_len, ATT_W)
    return y, yc


def hyena_filters(n_tok, w1, b1, fr1, w2, b2, fr2, w3):
    f32 = jnp.float32
    pos = jnp.arange(n_tok, dtype=f32)
    t = pos / max(n_tok - 1, 1)
    bands = jnp.linspace(1e-4, HY_BANDS - 1, HY_BANDS, dtype=f32)
    ang = (2.0 * math.pi / n_tok) * pos[:, None] * bands
    z = jnp.concatenate([t[:, None], jnp.cos(ang), jnp.sin(ang)], axis=-1)
    h = jnp.sin(fr1.astype(f32) * (z @ w1.astype(f32) + b1.astype(f32)))
    h = jnp.sin(fr2.astype(f32) * (h @ w2.astype(f32) + b2.astype(f32)))
    h = h @ w3.astype(f32)
    deltas = jnp.abs(jnp.linspace(math.log(HY_DECAY_TARGET) / HY_DECAY_SLOW,
                                  math.log(HY_DECAY_TARGET) / HY_DECAY_FAST, HY_W, dtype=f32))
    h = h.reshape(n_tok, 2, HY_ORDER, HY_W) * jnp.exp(-t[:, None, None, None] * deltas)
    k_full = jnp.concatenate([h[:, 0], jnp.zeros((1, HY_ORDER, HY_W), f32), h[:0:-1, 1]], axis=0)
    return k_full / (jnp.sum(jnp.abs(k_full), axis=0, keepdims=True) + 1e-6)


def fft_long_conv(u, k_full, d):
    n_tok = u.shape[1]
    uf = u.astype(jnp.float32)
    spec = jnp.fft.rfft(uf, n=2 * n_tok, axis=1) * jnp.fft.rfft(k_full, axis=0)
    y = jnp.fft.irfft(spec, n=2 * n_tok, axis=1)[:, :n_tok]
    return (y + uf * d.astype(jnp.float32)).astype(u.dtype)


def hyena_sequence(v, x1, x2, conv_w, conv_b, k_full, hy_d):
    u = centred_depthwise_conv(jnp.concatenate([v, x1, x2], axis=-1), conv_w) + conv_b
    v, x1, x2 = jnp.split(u, 3, axis=-1)
    z = x1 * fft_long_conv(v, k_full[:, 0], hy_d[0])
    return x2 * fft_long_conv(z, k_full[:, 1], hy_d[1])


def gated_delta_chunked(q, k, v, g, beta, state):
    f32 = jnp.float32
    bsz, n_tok, n_h, dk = q.shape
    dv = v.shape[-1]
    cl = DN_CHUNK
    n_ch = n_tok // cl

    def chunks(a):
        return jnp.moveaxis(a.reshape((bsz, n_ch, cl, n_h) + a.shape[3:]), 3, 1)

    q = chunks(q) * (dk ** -0.5)
    k = chunks(k)
    v = chunks(v)
    g = chunks(g)
    beta = chunks(beta)
    gc = jnp.cumsum(g, axis=-1)
    incl = jnp.tril(jnp.ones((cl, cl), bool))
    strict = jnp.tril(jnp.ones((cl, cl), bool), -1)
    decay = jnp.exp(jnp.where(incl, gc[..., :, None] - gc[..., None, :], -jnp.inf))
    kb = k * beta[..., None]
    a_mat = jnp.where(strict, jnp.einsum('bhncd,bhnsd->bhncs', kb, k) * decay, 0.0)
    eye = jnp.eye(cl, dtype=f32)
    t_mat = lax.linalg.triangular_solve(a_mat + eye, jnp.broadcast_to(eye, a_mat.shape),
                                        left_side=True, lower=True, unit_diagonal=True)
    u = t_mat @ (v * beta[..., None])
    w = t_mat @ (kb * jnp.exp(gc)[..., None])
    qk = jnp.einsum('bhncd,bhnsd->bhncs', q, k) * decay
    q_dec = q * jnp.exp(gc)[..., None]
    k_dec = k * jnp.exp(gc[..., -1:] - gc)[..., None]
    g_tot = jnp.exp(gc[..., -1])
    xs = tuple(jnp.moveaxis(a, 2, 0) for a in (u, w, qk, q_dec, k_dec, g_tot))

    def step(s, inp):
        u_n, w_n, qk_n, qd_n, kd_n, gt_n = inp
        v_new = u_n - jnp.einsum('bhck,bhkv->bhcv', w_n, s)
        o_n = jnp.einsum('bhck,bhkv->bhcv', qd_n, s) + jnp.einsum('bhcs,bhsv->bhcv', qk_n, v_new)
        s = s * gt_n[..., None, None] + jnp.einsum('bhck,bhcv->bhkv', kd_n, v_new)
        return s, o_n

    state, o = lax.scan(step, state, xs)
    o = jnp.moveaxis(jnp.moveaxis(o, 0, 2), 1, 3).reshape(bsz, n_tok, n_h, dv)
    return o, state


def deltanet_prepare(q, k, v, a, b, conv_w, a_log, dt_bias):
    f32 = jnp.float32
    bsz, n_tok, _ = q.shape
    u = jax.nn.silu(centred_depthwise_conv(jnp.concatenate([q, k, v], axis=-1), conv_w))
    q, k, v = jnp.split(u, [DN_QK_W, 2 * DN_QK_W], axis=-1)
    rep = DN_V_HEADS // DN_QK_HEADS
    q = jnp.repeat(l2_normalize(q.reshape(bsz, n_tok, DN_QK_HEADS, DN_HEAD_DIM)), rep, axis=2)
    k = jnp.repeat(l2_normalize(k.reshape(bsz, n_tok, DN_QK_HEADS, DN_HEAD_DIM)), rep, axis=2)
    v = v.reshape(bsz, n_tok, DN_V_HEADS, DN_HEAD_DIM).astype(f32)
    a = a.astype(f32).reshape(bsz, n_tok, 2, DN_V_HEADS)
    g = -jnp.exp(a_log.astype(f32)) * jax.nn.softplus(a + dt_bias.astype(f32))
    beta = jax.nn.sigmoid(b.astype(f32).reshape(bsz, n_tok, 2, DN_V_HEADS))
    return q, k, v, g, beta


def deltanet_gated_out(o, z, norm_g):
    bsz, n_tok = z.shape[:2]
    zh = z.reshape(bsz, n_tok, DN_V_HEADS, DN_HEAD_DIM).astype(jnp.float32)
    y = rms_norm(o, norm_g) * jax.nn.silu(zh)
    return y.astype(z.dtype).reshape(bsz, n_tok, DN_V_W)


def deltanet_mixer(q, k, v, z, a, b, qc, kc, vc, zc, ac, bc, conv_w, a_log, dt_bias, norm_g, need_ctx):
    q, k, v, g, beta = deltanet_prepare(q, k, v, a, b, conv_w, a_log, dt_bias)
    qc, kc, vc, gcx, bcx = deltanet_prepare(qc, kc, vc, ac, bc, conv_w, a_log, dt_bias)
    s0 = jnp.zeros((q.shape[0], DN_V_HEADS, DN_HEAD_DIM, DN_HEAD_DIM), jnp.float32)

    def flip(t):
        return jnp.flip(t, axis=1)

    oc_f, s_f = gated_delta_chunked(qc, kc, vc, gcx[:, :, 0], bcx[:, :, 0], s0)
    oc_b, s_b = gated_delta_chunked(flip(qc), flip(kc), flip(vc), flip(gcx[:, :, 1]), flip(bcx[:, :, 1]), s0)
    o_f, _ = gated_delta_chunked(q, k, v, g[:, :, 0], beta[:, :, 0], s_f)
    o_b, _ = gated_delta_chunked(flip(q), flip(k), flip(v), flip(g[:, :, 1]), flip(beta[:, :, 1]), s_b)
    y = deltanet_gated_out(o_f + flip(o_b), z, norm_g)
    yc = deltanet_gated_out(oc_f + flip(oc_b), zc, norm_g) if need_ctx else None
    return y, yc


def merge_branches(ya, yb, yc, merge_logits, w_pa, w_pb, w_pc, w_out):
    gates = jax.nn.sigmoid(merge_logits.astype(jnp.float32)).astype(merge_logits.dtype)
    g_a, g_b, g_c = jnp.split(gates, N_BRANCH, axis=-1)
    m = g_a * (ya @ w_pa) + g_b * (yb @ w_pb) + g_c * (yc @ w_pc)
    return m @ w_out


def setup_inputs(seed: int = 0) -> dict:
    key = jax.random.key(seed)
    ks = jax.random.split(key, 32)
    f32 = jnp.float32
    D = D_MODEL

    def nrm(k, shape, scale):
        return jax.random.normal(k, shape, f32) * scale

    dt = jnp.exp(jax.random.uniform(ks[20], (DEPTH, 2, DN_V_HEADS), f32, math.log(1e-3), math.log(1e-1)))
    return {
        'x': nrm(ks[0], (BATCH, SEQ, D), 1.0),
        'c': nrm(ks[1], (BATCH, D), 1.0),
        'ctx': nrm(ks[2], (BATCH, CTX_LEN, D), 1.0),
        'c_ctx': nrm(ks[3], (D,), 1.0),
        'norm_g': 1.0 + nrm(ks[4], (DEPTH, D), 0.02),
        'w_mod': nrm(ks[5], (DEPTH, D, 3 * D), 0.5 * D ** -0.5),
        'b_mod': nrm(ks[6], (DEPTH, 3 * D), 0.02),
        'w_in': nrm(ks[7], (DEPTH, D, IN_W), D ** -0.5),
        'q_norm_g': 1.0 + nrm(ks[8], (DEPTH, HEAD_DIM), 0.02),
        'k_norm_g': 1.0 + nrm(ks[9], (DEPTH, HEAD_DIM), 0.02),
        'hy_conv_w': nrm(ks[10], (DEPTH, HY_SHORT, 3 * HY_W), HY_SHORT ** -0.5),
        'hy_conv_b': nrm(ks[11], (DEPTH, 3 * HY_W), 0.02),
        'hy_w1': nrm(ks[12], (DEPTH, HY_EMB, HY_FILTER_HIDDEN), HY_EMB ** -0.5),
        'hy_b1': nrm(ks[13], (DEPTH, HY_FILTER_HIDDEN), 0.1),
        'hy_freq1': 1.0 + nrm(ks[14], (DEPTH, HY_FILTER_HIDDEN), 0.02),
        'hy_w2': nrm(ks[15], (DEPTH, HY_FILTER_HIDDEN, HY_FILTER_HIDDEN), HY_FILTER_HIDDEN ** -0.5),
        'hy_b2': nrm(ks[16], (DEPTH, HY_FILTER_HIDDEN), 0.1),
        'hy_freq2': 1.0 + nrm(ks[17], (DEPTH, HY_FILTER_HIDDEN), 0.02),
        'hy_w3': nrm(ks[18], (DEPTH, HY_FILTER_HIDDEN, 2 * HY_ORDER * HY_W), HY_FILTER_HIDDEN ** -0.5),
        'hy_d': nrm(ks[19], (DEPTH, HY_ORDER, HY_W), 0.5),
        'dn_conv_w': nrm(ks[21], (DEPTH, DN_SHORT, 2 * DN_QK_W + DN_V_W), DN_SHORT ** -0.5),
        'dn_a_log': jnp.log(jax.random.uniform(ks[22], (DEPTH, 2, DN_V_HEADS), f32, 1.0, 16.0)),
        'dn_dt_bias': dt + jnp.log(-jnp.expm1(-dt)),
        'dn_norm_g': 1.0 + nrm(ks[23], (DEPTH, DN_HEAD_DIM), 0.02),
        'w_pa': nrm(ks[24], (DEPTH, ATT_W, D), ATT_W ** -0.5),
        'w_pb': nrm(ks[25], (DEPTH, HY_W, D), HY_W ** -0.5),
        'w_pc': nrm(ks[26], (DEPTH, DN_V_W, D), DN_V_W ** -0.5),
        'w_out': nrm(ks[27], (DEPTH, D, D), D ** -0.5),
        'final_g': 1.0 + nrm(ks[28], (D,), 0.02),
    }


def reference(x, c, ctx, c_ctx, norm_g, w_mod, b_mod, w_in, q_norm_g, k_norm_g,
              hy_conv_w, hy_conv_b, hy_w1, hy_b1, hy_freq1, hy_w2, hy_b2, hy_freq2, hy_w3, hy_d,
              dn_conv_w, dn_a_log, dn_dt_bias, dn_norm_g, w_pa, w_pb, w_pc, w_out, final_g):
    f32 = jnp.float32
    n_tok = x.shape[1]
    ctx_len = ctx.shape[1]
    rows = n_tok // GRID_W
    row_idx = jnp.repeat(jnp.arange(rows, dtype=f32), GRID_W)
    col_idx = jnp.tile(jnp.arange(GRID_W, dtype=f32), rows)
    inv_freq = ROPE_THETA ** (-jnp.arange(0, AXIS_ROPE_DIM, 2, dtype=f32) / AXIS_ROPE_DIM)
    ang_row = row_idx[:, None] * inv_freq
    ang_col = col_idx[:, None] * inv_freq
    split_at = [int(i) for i in np.cumsum(IN_WIDTHS)[:-1]]

    xc = ctx
    for layer in range(DEPTH):
        need_ctx = layer < DEPTH - 1
        shift, scale, gate = ada_modulation(c, w_mod[layer], b_mod[layer])
        shift_c, scale_c, gate_c = ada_modulation(c_ctx, w_mod[layer], b_mod[layer])
        h = rms_norm(x, norm_g[layer]) * (1.0 + scale[:, None]) + shift[:, None]
        hc = rms_norm(xc, norm_g[layer]) * (1.0 + scale_c) + shift_c
        p = jnp.split(h @ w_in[layer], split_at, axis=-1)
        pc = jnp.split(hc @ w_in[layer], split_at, axis=-1)

        att, att_c = attention_mixer(p[0], p[1], p[2], pc[0], pc[1], pc[2],
                                     q_norm_g[layer], k_norm_g[layer], ang_row, ang_col, need_ctx)
        ya = att * jax.nn.silu(p[3])

        filt_args = (hy_w1[layer], hy_b1[layer], hy_freq1[layer], hy_w2[layer], hy_b2[layer],
                     hy_freq2[layer], hy_w3[layer])
        hy = hyena_sequence(p[4], p[5], p[6], hy_conv_w[layer], hy_conv_b[layer],
                            hyena_filters(n_tok, *filt_args), hy_d[layer])
        yb = hy * jax.nn.silu(p[7])

        yc, dn_c = deltanet_mixer(p[8], p[9], p[10], p[11], p[12], p[13],
                                  pc[8], pc[9], pc[10], pc[11], pc[12], pc[13],
                                  dn_conv_w[layer], dn_a_log[layer], dn_dt_bias[layer],
                                  dn_norm_g[layer], need_ctx)

        out = merge_branches(ya, yb, yc, p[14], w_pa[layer], w_pb[layer], w_pc[layer], w_out[layer])
        if need_ctx:
            hy_c = hyena_sequence(pc[4], pc[5], pc[6], hy_conv_w[layer], hy_conv_b[layer],
                                  hyena_filters(ctx_len, *filt_args), hy_d[layer])
            out_c = merge_branches(att_c * jax.nn.silu(pc[3]), hy_c * jax.nn.silu(pc[7]), dn_c, pc[14],
                                   w_pa[layer], w_pb[layer], w_pc[layer], w_out[layer])
            xc = xc + gate_c * out_c
        x = x + gate[:, None] * out

    return rms_norm(x, final_g)
```

```python
import functools
import math

import jax
import jax.numpy as jnp
import numpy as np
from jax import lax
from jax.experimental import pallas as pl
from jax.experimental.pallas import tpu as pltpu

F32 = jnp.float32
BF16 = jnp.bfloat16

GRID_W = 64
NORM_EPS = 1e-6
N_BRANCH = 3

ATT_HEADS = 8
ATT_KV_HEADS = 2
HEAD_DIM = 128
ATT_GROUP = ATT_HEADS // ATT_KV_HEADS
ATT_W = ATT_HEADS * HEAD_DIM
ATT_KV_W = ATT_KV_HEADS * HEAD_DIM
AXIS_ROPE_DIM = HEAD_DIM // 2
ROPE_THETA = 10000.0

HY_W = 1024
HY_ORDER = 2
HY_BANDS = 16
HY_EMB = 1 + 2 * HY_BANDS
HY_FILTER_HIDDEN = 64
HY_DECAY_TARGET = 1e-2
HY_DECAY_FAST = 0.3
HY_DECAY_SLOW = 1.5

DN_QK_HEADS = 4
DN_V_HEADS = 8
DN_HEAD_DIM = 128
DN_QK_W = DN_QK_HEADS * DN_HEAD_DIM
DN_V_W = DN_V_HEADS * DN_HEAD_DIM
DN_CHUNK = 64
DN_W = 2 * DN_QK_W + DN_V_W

LANE = 128
MIB = 1024 * 1024

OFF_AQ = 0
OFF_AK = OFF_AQ + ATT_W
OFF_AV = OFF_AK + ATT_KV_W
OFF_AG = OFF_AV + ATT_KV_W
OFF_HV = OFF_AG + ATT_W
OFF_HX1 = OFF_HV + HY_W
OFF_HX2 = OFF_HX1 + HY_W
OFF_HG = OFF_HX2 + HY_W
OFF_DQ = OFF_HG + HY_W
OFF_DV = OFF_DQ + 2 * DN_QK_W
OFF_DZ = OFF_DV + DN_V_W
OFF_DL = OFF_DZ + DN_V_W
N_LOGITS = 4 * DN_V_HEADS
MERGE_ALIGN = 512
OFF_MG = -(-(OFF_DL + N_LOGITS) // MERGE_ALIGN) * MERGE_ALIGN


def _pick(n, cap, mult=LANE):
    best = None
    for t in range(mult, min(n, cap) + 1, mult):
        if n % t == 0:
            best = t
    assert best is not None, (n, cap, mult)
    return best


def _cp(sem, vmem_mib=48):
    return pltpu.CompilerParams(dimension_semantics=sem, vmem_limit_bytes=vmem_mib * MIB)


def _silu(x):
    return x * jax.nn.sigmoid(x)


def _split3(x):
    x1 = x.astype(BF16)
    r = x - x1.astype(F32)
    x2 = r.astype(BF16)
    x3 = (r - x2.astype(F32)).astype(BF16)
    return x1, x2, x3


def _dot(a, b):
    return jnp.dot(a, b, preferred_element_type=F32)


def _dot_hp(a, b):
    a1 = a.astype(BF16)
    a2 = (a - a1.astype(F32)).astype(BF16)
    b1 = b.astype(BF16)
    b2 = (b - b1.astype(F32)).astype(BF16)
    return _dot(a1, b1) + (_dot(a1, b2) + _dot(a2, b1))


def _mod_kernel(cs_ref, w_ref, b_ref, o_ref):
    cs = cs_ref[...]
    o_ref[0] = _dot(_silu(cs).astype(BF16), w_ref[0].astype(BF16)) + b_ref[0]


def _mod_call(cs, w_mod, b_mod):
    depth, d, d3 = w_mod.shape
    tn = _pick(d3, 512)
    return pl.pallas_call(
        _mod_kernel,
        grid=(depth, d3 // tn),
        in_specs=[pl.BlockSpec((8, d), lambda l, j: (0, 0)),
                  pl.BlockSpec((1, d, tn), lambda l, j: (l, 0, j)),
                  pl.BlockSpec((1, 1, tn), lambda l, j: (l, 0, j))],
        out_specs=pl.BlockSpec((1, 8, tn), lambda l, j: (l, 0, j)),
        out_shape=jax.ShapeDtypeStruct((depth, 8, d3), F32),
        compiler_params=_cp(("parallel", "parallel")),
        name="adaln_mod",
    )(cs, w_mod, b_mod.reshape(depth, 1, d3))


def _row_mods(mod_ref, i, r0, rows, tm, tiles_per_batch, s_len, n_batch):
    b = i // tiles_per_batch
    row = (i % tiles_per_batch) * tm + r0 + lax.broadcasted_iota(jnp.int32, (rows, 1), 0)
    return row >= s_len, mod_ref[pl.ds(b, 1), :], mod_ref[n_batch:n_batch + 1, :]


NORM_ROWS = 64


def _inproj_kernel(x_ref, g_ref, mod_ref, w_ref, o_ref, h_ref, *, tm, tiles_per_batch, s_len, n_batch, d):
    i = pl.program_id(0)

    @pl.when(pl.program_id(1) == 0)
    def _():
        def body(c, carry):
            r0 = pl.multiple_of(c * NORM_ROWS, NORM_ROWS)
            x = x_ref[pl.ds(r0, NORM_ROWS), :]
            y = x * lax.rsqrt(jnp.mean(x * x, axis=-1, keepdims=True) + NORM_EPS) * g_ref[...]
            is_ctx, ml, mc = _row_mods(mod_ref, i, r0, NORM_ROWS, tm, tiles_per_batch, s_len, n_batch)
            shift = jnp.where(is_ctx, mc[:, :d], ml[:, :d])
            scale = jnp.where(is_ctx, mc[:, d:2 * d], ml[:, d:2 * d])
            h_ref[pl.ds(r0, NORM_ROWS), :] = (y * (1.0 + scale) + shift).astype(BF16)
            return carry
        lax.fori_loop(0, tm // NORM_ROWS, body, 0)

    o_ref[...] = _dot(h_ref[...], w_ref[...])


def _inproj_call(xs2, norm_g, mods, w_bf, *, n_batch, r_len, s_len):
    rt, d = xs2.shape
    nw = w_bf.shape[1]
    tm = _pick(r_len, 1088, NORM_ROWS)
    tn = _pick(nw, 1024)
    kern = functools.partial(_inproj_kernel, tm=tm, tiles_per_batch=r_len // tm, s_len=s_len,
                             n_batch=n_batch, d=d)
    return pl.pallas_call(
        kern,
        grid=(rt // tm, nw // tn),
        in_specs=[pl.BlockSpec((tm, d), lambda i, j: (i, 0)),
                  pl.BlockSpec((1, d), lambda i, j: (0, 0)),
                  pl.BlockSpec((8, 3 * d), lambda i, j: (0, 0)),
                  pl.BlockSpec((d, tn), lambda i, j: (0, j))],
        out_specs=pl.BlockSpec((tm, tn), lambda i, j: (i, j)),
        out_shape=jax.ShapeDtypeStruct((rt, nw), F32),
        scratch_shapes=[pltpu.VMEM((tm, d), BF16)],
        compiler_params=_cp(("parallel", "arbitrary")),
        name="norm_inproj",
    )(xs2, norm_g.reshape(1, d), mods, w_bf)


def _rope_tables(s_len, ctx_len):
    m = AXIS_ROPE_DIM // 2
    inv_freq = ROPE_THETA ** (-np.arange(0, AXIS_ROPE_DIM, 2, dtype=np.float64) / AXIS_ROPE_DIM)
    t = np.arange(s_len)
    ang_r = (t // GRID_W)[:, None] * inv_freq
    ang_c = (t % GRID_W)[:, None] * inv_freq
    cos = np.concatenate([np.cos(ang_r)] * 2 + [np.cos(ang_c)] * 2, axis=-1)
    sin = np.concatenate([-np.sin(ang_r), np.sin(ang_r), -np.sin(ang_c), np.sin(ang_c)], axis=-1)
    assert cos.shape[1] == 4 * m == HEAD_DIM
    cos = np.concatenate([cos, np.ones((ctx_len, HEAD_DIM))], axis=0)
    sin = np.concatenate([sin, np.zeros((ctx_len, HEAD_DIM))], axis=0)
    return jnp.asarray(cos, F32), jnp.asarray(sin, F32)


def _norm_rope(x, g, cs, sn):
    y = x * lax.rsqrt(jnp.mean(x * x, axis=-1, keepdims=True) + NORM_EPS) * g
    lane = lax.broadcasted_iota(jnp.int32, (1, HEAD_DIM), 1)
    first = (lane % AXIS_ROPE_DIM) < (AXIS_ROPE_DIM // 2)
    q = AXIS_ROPE_DIM // 2
    partner = jnp.where(first, pltpu.roll(y, HEAD_DIM - q, 1), pltpu.roll(y, q, 1))
    return y * cs + partner * sn


def _attn_kernel(q_ref, k_ref, v_ref, gt_ref, cos_ref, sin_ref, qg_ref, kg_ref, o_ref, ks_ref, vs_ref,
                 *, tq, r_len, s_len, kv_chunk):
    i = pl.program_id(2)

    @pl.when(i == 0)
    def _():
        def body(c, carry):
            r0 = pl.multiple_of(c * kv_chunk, kv_chunk)
            kk = _norm_rope(k_ref[0, pl.ds(r0, kv_chunk), :], kg_ref[...],
                            cos_ref[pl.ds(r0, kv_chunk), :], sin_ref[pl.ds(r0, kv_chunk), :])
            ks_ref[pl.ds(r0, kv_chunk), :] = kk.astype(BF16)
            vs_ref[pl.ds(r0, kv_chunk), :] = v_ref[0, pl.ds(r0, kv_chunk), :].astype(BF16)
            return carry
        lax.fori_loop(0, r_len // kv_chunk, body, 0)

    r0 = pl.multiple_of(i * tq, tq)
    cs = cos_ref[pl.ds(r0, tq), :]
    sn = sin_ref[pl.ds(r0, tq), :]
    scale = HEAD_DIM ** -0.5

    def heads(k_lo, k_hi):
        for g in range(ATT_GROUP):
            sl = slice(g * HEAD_DIM, (g + 1) * HEAD_DIM)
            qh = (_norm_rope(q_ref[0, :, sl], qg_ref[...], cs, sn) * scale).astype(BF16)
            s = lax.dot_general(qh, ks_ref[k_lo:k_hi, :], (((1,), (1,)), ((), ())),
                                preferred_element_type=F32)
            e = jnp.exp(s - jnp.max(s, axis=-1, keepdims=True))
            den = jnp.sum(e, axis=-1, keepdims=True)
            o = _dot(e.astype(BF16), vs_ref[k_lo:k_hi, :]) / den
            o_ref[0, :, sl] = (o * _silu(gt_ref[0, :, sl])).astype(o_ref.dtype)

    @pl.when(i < s_len // tq)
    def _():
        heads(0, r_len)

    @pl.when(i >= s_len // tq)
    def _():
        heads(s_len, r_len)


def _attn_call(p3, cos_t, sin_t, q_g, k_g, *, s_len):
    n_batch, r_len, _ = p3.shape
    ctx_len = r_len - s_len
    tq = 256 if (ctx_len % 256 == 0 and s_len % 256 == 0) else 128
    assert ctx_len % tq == 0 and s_len % tq == 0
    gw = ATT_GROUP * HEAD_DIM
    kern = functools.partial(_attn_kernel, tq=tq, r_len=r_len, s_len=s_len, kv_chunk=tq)
    return pl.pallas_call(
        kern,
        grid=(n_batch, ATT_KV_HEADS, r_len // tq),
        in_specs=[pl.BlockSpec((1, tq, gw), lambda b, h, i: (b, i, OFF_AQ // gw + h)),
                  pl.BlockSpec((1, r_len, HEAD_DIM), lambda b, h, i: (b, 0, OFF_AK // HEAD_DIM + h)),
                  pl.BlockSpec((1, r_len, HEAD_DIM), lambda b, h, i: (b, 0, OFF_AV // HEAD_DIM + h)),
                  pl.BlockSpec((1, tq, gw), lambda b, h, i: (b, i, OFF_AG // gw + h)),
                  pl.BlockSpec((r_len, HEAD_DIM), lambda b, h, i: (0, 0)),
                  pl.BlockSpec((r_len, HEAD_DIM), lambda b, h, i: (0, 0)),
                  pl.BlockSpec((1, HEAD_DIM), lambda b, h, i: (0, 0)),
                  pl.BlockSpec((1, HEAD_DIM), lambda b, h, i: (0, 0))],
        out_specs=pl.BlockSpec((1, tq, gw), lambda b, h, i: (b, i, h)),
        out_shape=jax.ShapeDtypeStruct((n_batch, r_len, ATT_W), BF16),
        scratch_shapes=[pltpu.VMEM((r_len, HEAD_DIM), BF16), pltpu.VMEM((r_len, HEAD_DIM), BF16)],
        compiler_params=_cp(("parallel", "parallel", "arbitrary")),
        name="gqa_attention",
    )(p3, p3, p3, p3, cos_t, sin_t, q_g.reshape(1, HEAD_DIM), k_g.reshape(1, HEAD_DIM))


def _conv3(x, w, first, last):
    n = x.shape[0]
    prev = jnp.where(first, 0.0, pltpu.roll(x, 1, 0))
    nxt = jnp.where(last, 0.0, pltpu.roll(x, n - 1, 0))
    return prev * w[0:1, :] + x * w[1:2, :] + nxt * w[2:3, :]


def _fft_split(n):
    if n <= 512:
        return 1, n
    n2 = 64
    return n // n2, n2


@functools.lru_cache(maxsize=None)
def _fft_mats(n, n1, n2):
    half = n2 // 2
    a = np.arange(n1)[:, None, None]
    k2 = np.arange(n2)[None, :, None]

    def g(bs):
        ph = (k2 * (a + n1 * bs[None, None, :])) % n
        ang = -2.0 * np.pi * ph / n
        return np.cos(ang), np.sin(ang)

    gre, gim = g(np.arange(half))
    gal = np.concatenate([gre, gim], axis=1)
    gar = np.concatenate([-gim, gre], axis=1)
    hre = np.swapaxes(gre, 1, 2) / n
    him = -np.swapaxes(gim, 1, 2) / n
    hal = np.concatenate([hre, him], axis=1)
    har = np.concatenate([-him, hre], axis=1)
    fre_f, fim_f = g(np.arange(n2))
    gf = np.concatenate([fre_f, fim_f], axis=1)
    k1 = np.arange(n1)
    ang1 = -2.0 * np.pi * ((k1[:, None] * k1[None, :]) % n1) / n1
    f1re, f1im = np.cos(ang1), np.sin(ang1)
    fbl = np.concatenate([f1re, f1im], axis=0)
    fbr = np.concatenate([-f1im, f1re], axis=0)
    fbil = np.concatenate([f1re, -f1im], axis=0)
    fbir = np.concatenate([f1im, f1re], axis=0)
    cast = lambda m: jnp.asarray(m, BF16)
    return dict(gal=cast(gal), gar=cast(gar), hal=cast(hal), har=cast(har), gf=cast(gf),
                fbl=cast(fbl), fbr=cast(fbr), fbil=cast(fbil), fbir=cast(fbir))


@functools.lru_cache(maxsize=None)
def _hy_tables(n_tok):
    pos = np.arange(n_tok, dtype=np.float64)
    t = pos / max(n_tok - 1, 1)
    bands = np.linspace(1e-4, HY_BANDS - 1, HY_BANDS)
    ang = (2.0 * np.pi / n_tok) * pos[:, None] * bands
    z = np.concatenate([t[:, None], np.cos(ang), np.sin(ang)], axis=-1)
    zrev = np.zeros_like(z)
    zrev[1:] = z[:0:-1]
    ztab = np.zeros((2 * n_tok, LANE))
    ztab[:, :HY_EMB] = np.concatenate([z, zrev], axis=0)
    deltas = np.abs(np.linspace(math.log(HY_DECAY_TARGET) / HY_DECAY_SLOW,
                                math.log(HY_DECAY_TARGET) / HY_DECAY_FAST, HY_W))
    return jnp.asarray(ztab, F32), jnp.asarray(np.tile(deltas, HY_ORDER)[None, :], F32)


def _hyfilt_kernel(z_ref, w1_ref, b1_ref, f1_ref, w2_ref, b2_ref, f2_ref, w3_ref, dl_ref, k_ref, s_ref,
                   *, n_tok, tr):
    i = pl.program_id(0)
    z = z_ref[...]
    h = jnp.sin(f1_ref[...] * (_dot_hp(z, w1_ref[...]) + b1_ref[...]))
    h = jnp.sin(f2_ref[...] * (_dot_hp(h, w2_ref[...]) + b2_ref[...]))
    h = _dot_hp(h, w3_ref[...]) * jnp.exp(-z[:, 0:1] * dl_ref[...])
    row = i * tr + lax.broadcasted_iota(jnp.int32, (tr, 1), 0)
    h = jnp.where(row == n_tok, 0.0, h)
    k_ref[...] = h

    @pl.when(i == 0)
    def _():
        s_ref[...] = jnp.zeros_like(s_ref)

    s_ref[...] += jnp.sum(jnp.abs(h), axis=0, keepdims=True)


def _hyfilt_call(n_tok, w1p, b1, f1, w2, b2, f2, w3):
    ztab, dl = _hy_tables(n_tok)
    n = 2 * n_tok
    tr = _pick(n_tok, 512, 8)
    ow = HY_ORDER * HY_W
    hid = HY_FILTER_HIDDEN
    kern = functools.partial(_hyfilt_kernel, n_tok=n_tok, tr=tr)
    c2 = lambda i: (0, 0)
    return pl.pallas_call(
        kern,
        grid=(n // tr,),
        in_specs=[pl.BlockSpec((tr, LANE), lambda i: (i, 0)),
                  pl.BlockSpec((LANE, hid), c2), pl.BlockSpec((1, hid), c2), pl.BlockSpec((1, hid), c2),
                  pl.BlockSpec((hid, hid), c2), pl.BlockSpec((1, hid), c2), pl.BlockSpec((1, hid), c2),
                  pl.BlockSpec((hid, ow), lambda i: (0, i // (n_tok // tr))),
                  pl.BlockSpec((1, ow), c2)],
        out_specs=[pl.BlockSpec((tr, ow), lambda i: (i, 0)), pl.BlockSpec((1, ow), c2)],
        out_shape=[jax.ShapeDtypeStruct((n, ow), F32), jax.ShapeDtypeStruct((1, ow), F32)],
        compiler_params=_cp(("arbitrary",)),
        name="hyena_filter",
    )(ztab, w1p, b1.reshape(1, hid), f1.reshape(1, hid), w2, b2.reshape(1, hid), f2.reshape(1, hid), w3, dl)


def _filtfft_kernel(k_ref, s_ref, gf_ref, fbl_ref, fbr_ref, o_ref, z_ref, *, n, n1, n2):
    inv = 1.0 / (s_ref[...] + 1e-6)
    if n1 == 1:
        o_ref[...] = _dot(gf_ref[0], k_ref[...].astype(BF16)) * inv
        return

    def stage_a(a, carry):
        rows = k_ref[pl.ds(a, n2, stride=n1), :].astype(BF16)
        out = _dot(gf_ref[a], rows)
        z_ref[pl.ds(a, n2, stride=n1), :] = out[:n2]
        z_ref[pl.ds(n + a, n2, stride=n1), :] = out[n2:]
        return carry
    lax.fori_loop(0, n1, stage_a, 0)

    def stage_b(k2, carry):
        r0 = pl.multiple_of(k2 * n1, n1)
        zr = z_ref[pl.ds(r0, n1), :].astype(BF16)
        zi = z_ref[pl.ds(n + r0, n1), :].astype(BF16)
        x = _dot(fbl_ref[...], zr) + _dot(fbr_ref[...], zi)
        o_ref[pl.ds(r0, n1), :] = x[:n1] * inv
        o_ref[pl.ds(n + r0, n1), :] = x[n1:] * inv
        return carry
    lax.fori_loop(0, n2, stage_b, 0)


def _filtfft_call(k_un, asum):
    n, ow = k_un.shape
    n1, n2 = _fft_split(n)
    m = _fft_mats(n, n1, n2)
    cb = LANE
    kern = functools.partial(_filtfft_kernel, n=n, n1=n1, n2=n2)
    return pl.pallas_call(
        kern,
        grid=(ow // cb,),
        in_specs=[pl.BlockSpec((n, cb), lambda j: (0, j)),
                  pl.BlockSpec((1, cb), lambda j: (0, j)),
                  pl.BlockSpec(m["gf"].shape, lambda j: (0, 0, 0)),
                  pl.BlockSpec(m["fbl"].shape, lambda j: (0, 0)),
                  pl.BlockSpec(m["fbr"].shape, lambda j: (0, 0))],
        out_specs=pl.BlockSpec((2 * n, cb), lambda j: (0, j)),
        out_shape=jax.ShapeDtypeStruct((2 * n, ow), F32),
        scratch_shapes=[pltpu.VMEM((2 * n, cb), F32)],
        compiler_params=_cp(("parallel",)),
        name="hyena_filter_fft",
    )(k_un, asum, m["gf"], m["fbl"], m["fbr"])


def _hyena_kernel(*refs, n_tok, n1, n2, conv_a, has_gate):
    it = iter(refs)
    a_ref, m_ref = next(it), next(it)
    g_ref = next(it) if has_gate else None
    if conv_a:
        cwa_ref, cba_ref = next(it), next(it)
    cwm_ref, cbm_ref, d_ref, ks_ref = next(it), next(it), next(it), next(it)
    gal_ref, gar_ref, hal_ref, har_ref = next(it), next(it), next(it), next(it)
    if n1 > 1:
        fbl_ref, fbr_ref, fbil_ref, fbir_ref = next(it), next(it), next(it), next(it)
    o_ref, ac_ref, z_ref = next(it), next(it), next(it)

    n = 2 * n_tok
    half = n2 // 2
    row = lax.broadcasted_iota(jnp.int32, (n_tok, 1), 0)
    first, last = row == 0, row == n_tok - 1
    for b in range(2):
        a = a_ref[b]
        if conv_a:
            a = _conv3(a, cwa_ref[...], first, last) + cba_ref[...]
        ac_ref[b] = a

    def spectrum_mul(xre, xim, kre, kim):
        return xre * kre - xim * kim, xre * kim + xim * kre

    if n1 == 1:
        x = _dot(gal_ref[0], ac_ref[0].astype(BF16)) + _dot(gar_ref[0], ac_ref[1].astype(BF16))
        yre, yim = spectrum_mul(x[:n], x[n:], ks_ref[0:n, :], ks_ref[n:2 * n, :])
        y = _dot(hal_ref[0], yre.astype(BF16)) + _dot(har_ref[0], yim.astype(BF16))
        for b in range(2):
            ac_ref[b] = y[b * n_tok:(b + 1) * n_tok] + ac_ref[b] * d_ref[0]
    else:
        def stage_a(a, carry):
            r0 = ac_ref[0, pl.ds(a, half, stride=n1), :].astype(BF16)
            r1 = ac_ref[1, pl.ds(a, half, stride=n1), :].astype(BF16)
            out = _dot(gal_ref[a], r0) + _dot(gar_ref[a], r1)
            z_ref[pl.ds(a, n2, stride=n1), :] = out[:n2]
            z_ref[pl.ds(n + a, n2, stride=n1), :] = out[n2:]
            return carry
        lax.fori_loop(0, n1, stage_a, 0)

        def stage_b(k2, carry):
            r0 = pl.multiple_of(k2 * n1, n1)
            zr = z_ref[pl.ds(r0, n1), :].astype(BF16)
            zi = z_ref[pl.ds(n + r0, n1), :].astype(BF16)
            x = _dot(fbl_ref[...], zr) + _dot(fbr_ref[...], zi)
            yre, yim = spectrum_mul(x[:n1], x[n1:], ks_ref[pl.ds(r0, n1), :], ks_ref[pl.ds(n + r0, n1), :])
            v = _dot(fbil_ref[...], yre.astype(BF16)) + _dot(fbir_ref[...], yim.astype(BF16))
            z_ref[pl.ds(r0, n1), :] = v[:n1]
            z_ref[pl.ds(n + r0, n1), :] = v[n1:]
            return carry
        lax.fori_loop(0, n2, stage_b, 0)

        def stage_a_inv(a, carry):
            vr = z_ref[pl.ds(a, n2, stride=n1), :].astype(BF16)
            vi = z_ref[pl.ds(n + a, n2, stride=n1), :].astype(BF16)
            y = _dot(hal_ref[a], vr) + _dot(har_ref[a], vi)
            for b in range(2):
                cur = ac_ref[b, pl.ds(a, half, stride=n1), :]
                ac_ref[b, pl.ds(a, half, stride=n1), :] = y[b * half:(b + 1) * half] + cur * d_ref[0]
            return carry
        lax.fori_loop(0, n1, stage_a_inv, 0)

    for b in range(2):
        mc = _conv3(m_ref[b], cwm_ref[...], first, last) + cbm_ref[...]
        out = mc * ac_ref[b]
        if has_gate:
            out = out * _silu(g_ref[b])
        o_ref[b] = out.astype(o_ref.dtype)


def _hyena_call(a_arr, a_blk, m_arr, m_blk, gate, conv_a, conv_m, d_row, spec, spec_blk, *, n_tok, out_dtype):
    n_batch = a_arr.shape[0]
    assert n_batch % 2 == 0
    n = 2 * n_tok
    n1, n2 = _fft_split(n)
    m = _fft_mats(n, n1, n2)
    cb = LANE
    one = pl.Buffered(1)

    def tile(blk):
        return pl.BlockSpec((2, n_tok, cb), lambda p, j, blk=blk: (p, blk[0], blk[1] + j), pipeline_mode=one)

    def convspec(c0):
        return [pl.BlockSpec((3, cb), lambda p, j, c0=c0: (0, c0 + j)),
                pl.BlockSpec((1, cb), lambda p, j, c0=c0: (0, c0 + j))]

    ins, specs = [a_arr, m_arr], [tile(a_blk), tile(m_blk)]
    if gate is not None:
        ins.append(gate[0])
        specs.append(tile(gate[1]))
    if conv_a is not None:
        ins += [conv_a[0], conv_a[1]]
        specs += convspec(conv_a[2])
    ins += [conv_m[0], conv_m[1], d_row, spec]
    specs += convspec(conv_m[2])
    specs += [pl.BlockSpec((1, 1, cb), lambda p, j: (0, 0, j)),
              pl.BlockSpec((2 * n, cb), lambda p, j, s0=spec_blk: (0, s0 + j), pipeline_mode=one)]
    names = ["gal", "gar", "hal", "har"] + (["fbl", "fbr", "fbil", "fbir"] if n1 > 1 else [])
    for nm in names:
        ins.append(m[nm])
        specs.append(pl.BlockSpec(m[nm].shape, lambda p, j, nd=m[nm].ndim: (0,) * nd, pipeline_mode=one))
    kern = functools.partial(_hyena_kernel, n_tok=n_tok, n1=n1, n2=n2, conv_a=conv_a is not None,
                             has_gate=gate is not None)
    return pl.pallas_call(
        kern,
        grid=(n_batch // 2, HY_W // cb),
        in_specs=specs,
        out_specs=pl.BlockSpec((2, n_tok, cb), lambda p, j: (p, 0, j)),
        out_shape=jax.ShapeDtypeStruct((n_batch, n_tok, HY_W), out_dtype),
        scratch_shapes=[pltpu.VMEM((2, n_tok, cb), F32), pltpu.VMEM((2 * n, cb), F32)],
        compiler_params=_cp(("parallel", "arbitrary"), 56),
        name="hyena_conv",
    )(*ins)


def _hyena_mixer(p3, row_blk, n_tok, conv_w, conv_b, hy_d, filt):
    k_un, asum = _hyfilt_call(n_tok, *filt)
    spec = _filtfft_call(k_un, asum)
    cbias = conv_b.reshape(1, -1)
    cblk = HY_W // LANE
    d3 = hy_d.reshape(HY_ORDER, 1, HY_W)
    z = _hyena_call(p3, (row_blk, OFF_HV // LANE), p3, (row_blk, OFF_HX1 // LANE), None,
                    (conv_w, cbias, 0), (conv_w, cbias, cblk), d3[0:1], spec, 0,
                    n_tok=n_tok, out_dtype=F32)
    return _hyena_call(z, (0, 0), p3, (row_blk, OFF_HX2 // LANE), (p3, (row_blk, OFF_HG // LANE)),
                       None, (conv_w, cbias, 2 * cblk), d3[1:2], spec, cblk,
                       n_tok=n_tok, out_dtype=BF16)


def _softplus(x):
    return jnp.maximum(x, 0.0) + jnp.log1p(jnp.exp(-jnp.abs(x)))


def _dnprep_kernel(x_ref, w_ref, o_ref, *, r_len, s_len):
    j = pl.program_id(1)
    row = lax.broadcasted_iota(jnp.int32, (r_len, 1), 0)
    first = (row == 0) | (row == s_len)
    last = (row == s_len - 1) | (row == r_len - 1)
    u = _silu(_conv3(x_ref[0], w_ref[...], first, last))
    nrm = u * lax.rsqrt(jnp.sum(u * u, axis=-1, keepdims=True) + 1e-6)
    o_ref[0] = jnp.where(j < 2 * DN_QK_HEADS, nrm, u)


def _dnprep_call(p3, conv_w, *, s_len):
    n_batch, r_len, _ = p3.shape
    kern = functools.partial(_dnprep_kernel, r_len=r_len, s_len=s_len)
    return pl.pallas_call(
        kern,
        grid=(n_batch, DN_W // LANE),
        in_specs=[pl.BlockSpec((1, r_len, LANE), lambda b, j: (b, 0, OFF_DQ // LANE + j)),
                  pl.BlockSpec((3, LANE), lambda b, j: (0, j))],
        out_specs=pl.BlockSpec((1, r_len, LANE), lambda b, j: (b, 0, j)),
        out_shape=jax.ShapeDtypeStruct((n_batch, r_len, DN_W), F32),
        compiler_params=_cp(("parallel", "parallel")),
        name="deltanet_prep",
    )(p3, conv_w)


def _tri_mats(tr):
    idx = np.arange(tr)
    same = (idx[:, None] // DN_CHUNK) == (idx[None, :] // DN_CHUNK)
    low = same & (idx[:, None] >= idx[None, :])
    return jnp.asarray(low, BF16), jnp.asarray(low.T, BF16), jnp.asarray(same, BF16)


def _dnintra_kernel(u_ref, la_ref, lb_ref, lat_ref, alr_ref, dtr_ref, alc_ref, dtc_ref, low_ref, upp_ref,
                    one_ref, a_ref, qk_ref, be_ref, eg_ref, ek_ref, gt_ref, *, tr):
    g_col = -jnp.exp(alr_ref[0]) * _softplus(la_ref[0, 0] + dtr_ref[0])
    beta = jax.nn.sigmoid(lb_ref[0, 0])
    g1, g2, g3 = _split3(g_col)
    gc_col = _dot(low_ref[...], g1) + (_dot(low_ref[...], g2) + _dot(low_ref[...], g3))
    gt_col = _dot(one_ref[...], g1) + (_dot(one_ref[...], g2) + _dot(one_ref[...], g3))
    be_ref[0, 0] = beta
    eg_ref[0, 0] = jnp.exp(gc_col)
    ek_ref[0, 0] = jnp.exp(gt_col - gc_col)
    gt_ref[0, 0] = jnp.exp(gt_col)
    g_row = -jnp.exp(alc_ref[0]) * _softplus(lat_ref[0, 0] + dtc_ref[0])
    r1, r2, r3 = _split3(g_row)
    gc_row = _dot(r1, upp_ref[...]) + (_dot(r2, upp_ref[...]) + _dot(r3, upp_ref[...]))

    ii = lax.broadcasted_iota(jnp.int32, (DN_CHUNK, DN_CHUNK), 0)
    jj = lax.broadcasted_iota(jnp.int32, (DN_CHUNK, DN_CHUNK), 1)
    scale = DN_HEAD_DIM ** -0.5
    nt = (((1,), (1,)), ((), ()))
    for c in range(tr // DN_CHUNK):
        rows = slice(c * DN_CHUNK, (c + 1) * DN_CHUNK)
        for hq in range(DN_QK_HEADS):
            q = u_ref[0, 0, rows, hq * DN_HEAD_DIM:(hq + 1) * DN_HEAD_DIM].astype(BF16)
            k = u_ref[0, 0, rows, DN_QK_W + hq * DN_HEAD_DIM:DN_QK_W + (hq + 1) * DN_HEAD_DIM].astype(BF16)
            kk = lax.dot_general(k, k, nt, preferred_element_type=F32)
            qk = lax.dot_general(q, k, nt, preferred_element_type=F32) * scale
            for h in range(hq * (DN_V_HEADS // DN_QK_HEADS), (hq + 1) * (DN_V_HEADS // DN_QK_HEADS)):
                diff = gc_col[rows, h:h + 1] - gc_row[h:h + 1, rows]
                dec = jnp.where(ii >= jj, jnp.exp(jnp.minimum(diff, 0.0)), 0.0)
                a_ref[0, 0, h, c] = jnp.where(ii > jj, kk * beta[rows, h:h + 1] * dec, 0.0)
                qk_ref[0, 0, h, c] = qk * dec


def _dnintra_call(u2, la, lb, lat, alog, dtb):
    n_dir, n_batch, r_len, _ = u2.shape
    tr = 4 * DN_CHUNK
    nc = r_len // DN_CHUNK
    low, upp, one = _tri_mats(tr)
    kern = functools.partial(_dnintra_kernel, tr=tr)
    pad = lambda v: jnp.pad(v, ((0, 0), (0, LANE - DN_V_HEADS))).reshape(n_dir, 1, LANE)
    col = lambda v: v.reshape(n_dir, DN_V_HEADS, 1)
    cm = lambda d, b, t: (0, 0)
    gspec = pl.BlockSpec((1, 1, tr, LANE), lambda d, b, t: (d, b, t, 0))
    mspec = pl.BlockSpec((1, 1, DN_V_HEADS, tr // DN_CHUNK, DN_CHUNK, DN_CHUNK), lambda d, b, t: (d, b, 0, t, 0, 0))
    gshape = jax.ShapeDtypeStruct((n_dir, n_batch, r_len, LANE), F32)
    mshape = jax.ShapeDtypeStruct((n_dir, n_batch, DN_V_HEADS, nc, DN_CHUNK, DN_CHUNK), F32)
    return pl.pallas_call(
        kern,
        grid=(n_dir, n_batch, r_len // tr),
        in_specs=[pl.BlockSpec((1, 1, tr, 2 * DN_QK_W), lambda d, b, t: (d, b, t, 0)),
                  gspec, gspec,
                  pl.BlockSpec((1, 1, DN_V_HEADS, tr), lambda d, b, t: (d, b, 0, t)),
                  pl.BlockSpec((1, 1, LANE), lambda d, b, t: (d, 0, 0)),
                  pl.BlockSpec((1, 1, LANE), lambda d, b, t: (d, 0, 0)),
                  pl.BlockSpec((1, DN_V_HEADS, 1), lambda d, b, t: (d, 0, 0)),
                  pl.BlockSpec((1, DN_V_HEADS, 1), lambda d, b, t: (d, 0, 0)),
                  pl.BlockSpec((tr, tr), cm), pl.BlockSpec((tr, tr), cm), pl.BlockSpec((tr, tr), cm)],
        out_specs=[mspec, mspec, gspec, gspec, gspec, gspec],
        out_shape=[mshape, mshape, gshape, gshape, gshape, gshape],
        compiler_params=_cp(("parallel", "parallel", "parallel")),
        name="deltanet_intra",
    )(u2, la, lb, lat, pad(alog), pad(dtb), col(alog), col(dtb), low, upp, one)


def _dnsolve_kernel(a_ref, o_ref, at_ref, tt_ref):
    c = DN_CHUNK
    for blk in range(c * c // LANE):
        at_ref[blk * LANE:(blk + 1) * LANE, :] = a_ref[:, blk * LANE:(blk + 1) * LANE].T
    tt_ref[...] = jnp.zeros_like(tt_ref)
    for i in range(c):
        nr = 8 * (i // 8 + 1)
        rr = lax.broadcasted_iota(jnp.int32, (nr, LANE), 0)
        acc = jnp.where(rr == i, 1.0, 0.0)

        def body(j, acc, i=i, nr=nr):
            a = at_ref[pl.ds(i * c + j, 1), :]
            return acc - a * tt_ref[pl.ds(pl.multiple_of(j * c, c), nr), :]
        if i > 0:
            acc = lax.fori_loop(0, i, body, acc, unroll=min(i, 8))
        tt_ref[i * c:i * c + nr, :] = acc
    for blk in range(c * c // LANE):
        o_ref[:, blk * LANE:(blk + 1) * LANE] = tt_ref[blk * LANE:(blk + 1) * LANE, :].T


def _dnsolve_call(a2):
    ni, cc = a2.shape
    assert ni % LANE == 0
    return pl.pallas_call(
        _dnsolve_kernel,
        grid=(ni // LANE,),
        in_specs=[pl.BlockSpec((LANE, cc), lambda i: (i, 0))],
        out_specs=pl.BlockSpec((LANE, cc), lambda i: (i, 0)),
        out_shape=jax.ShapeDtypeStruct((ni, cc), F32),
        scratch_shapes=[pltpu.VMEM((cc, LANE), F32), pltpu.VMEM((cc, LANE), F32)],
        compiler_params=_cp(("parallel",)),
        name="deltanet_solve",
    )(a2)


def _dnscan_kernel(u_ref, t_ref, qk_ref, be_ref, eg_ref, ek_ref, gt_ref, o_ref, s_ref, *, n_dir, n_batch):
    @pl.when(pl.program_id(0) == 0)
    def _():
        s_ref[...] = jnp.zeros_like(s_ref)

    scale = DN_HEAD_DIM ** -0.5
    rep = DN_V_HEADS // DN_QK_HEADS
    for d in range(n_dir):
        for b in range(n_batch):
            for h in range(DN_V_HEADS):
                hq = h // rep
                q = u_ref[d, b, :, hq * DN_HEAD_DIM:(hq + 1) * DN_HEAD_DIM]
                k = u_ref[d, b, :, DN_QK_W + hq * DN_HEAD_DIM:DN_QK_W + (hq + 1) * DN_HEAD_DIM]
                v = u_ref[d, b, :, 2 * DN_QK_W + h * DN_HEAD_DIM:2 * DN_QK_W + (h + 1) * DN_HEAD_DIM]
                be = be_ref[d, b, :, h:h + 1]
                eg = eg_ref[d, b, :, h:h + 1]
                ek = ek_ref[d, b, :, h:h + 1]
                gt = gt_ref[d, b, 0:1, h:h + 1]
                kb = k * be
                rhs = jnp.concatenate([v * be, kb * eg], axis=1).astype(BF16)
                uw = _dot(t_ref[d, b, h, 0].astype(BF16), rhs)
                idx = (d * n_batch + b) * DN_V_HEADS + h
                s = s_ref[idx]
                lhs = jnp.concatenate([uw[:, DN_HEAD_DIM:], q * (scale * eg)], axis=0).astype(BF16)
                ws_qs = _dot(lhs, s.astype(BF16))
                vn = uw[:, :DN_HEAD_DIM] - ws_qs[:DN_CHUNK]
                vnb = vn.astype(BF16)
                o_ref[d, b, :, h * DN_HEAD_DIM:(h + 1) * DN_HEAD_DIM] = (
                    ws_qs[DN_CHUNK:] + _dot(qk_ref[d, b, h, 0].astype(BF16), vnb))
                kd = (k * ek).astype(BF16)
                s_ref[idx] = s * gt + lax.dot_general(kd, vnb, (((0,), (0,)), ((), ())),
                                                      preferred_element_type=F32)


def _dnscan_call(u2, t6, qk6, be, eg, ek, gt, *, s_len):
    n_dir, n_batch, r_len, _ = u2.shape
    nc = r_len // DN_CHUNK
    ncl = s_len // DN_CHUNK

    def cidx(t):
        return jnp.where(t < nc - ncl, ncl + t, t - (nc - ncl))

    kern = functools.partial(_dnscan_kernel, n_dir=n_dir, n_batch=n_batch)
    gspec = pl.BlockSpec((n_dir, n_batch, DN_CHUNK, LANE), lambda t: (0, 0, cidx(t), 0))
    mspec = pl.BlockSpec((n_dir, n_batch, DN_V_HEADS, 1, DN_CHUNK, DN_CHUNK), lambda t: (0, 0, 0, cidx(t), 0, 0))
    return pl.pallas_call(
        kern,
        grid=(nc,),
        in_specs=[pl.BlockSpec((n_dir, n_batch, DN_CHUNK, DN_W), lambda t: (0, 0, cidx(t), 0)),
                  mspec, mspec, gspec, gspec, gspec, gspec],
        out_specs=pl.BlockSpec((n_dir, n_batch, DN_CHUNK, DN_V_W), lambda t: (0, 0, cidx(t), 0)),
        out_shape=jax.ShapeDtypeStruct((n_dir, n_batch, r_len, DN_V_W), F32),
        scratch_shapes=[pltpu.VMEM((n_dir * n_batch * DN_V_HEADS, DN_HEAD_DIM, DN_HEAD_DIM), F32)],
        compiler_params=_cp(("arbitrary",)),
        name="deltanet_scan",
    )(u2, t6, qk6, be, eg, ek, gt)


DN_OUT_COLS = 512


def _dnout_kernel(of_ref, ob_ref, z_ref, g_ref, o_ref):
    for h in range(DN_OUT_COLS // DN_HEAD_DIM):
        sl = slice(h * DN_HEAD_DIM, (h + 1) * DN_HEAD_DIM)
        o = of_ref[0, 0, :, sl] + ob_ref[0, :, sl]
        y = o * lax.rsqrt(jnp.mean(o * o, axis=-1, keepdims=True) + NORM_EPS) * g_ref[...]
        o_ref[0, :, sl] = (y * _silu(z_ref[0, :, sl])).astype(o_ref.dtype)


def _dnout_call(o2, ob, p3, norm_g):
    _, n_batch, r_len, _ = o2.shape
    tr = _pick(r_len, 512, 16)
    cw = DN_OUT_COLS
    assert OFF_DZ % cw == 0
    return pl.pallas_call(
        _dnout_kernel,
        grid=(n_batch, r_len // tr, DN_V_W // cw),
        in_specs=[pl.BlockSpec((1, 1, tr, cw), lambda b, i, j: (0, b, i, j)),
                  pl.BlockSpec((1, tr, cw), lambda b, i, j: (b, i, j)),
                  pl.BlockSpec((1, tr, cw), lambda b, i, j: (b, i, OFF_DZ // cw + j)),
                  pl.BlockSpec((1, DN_HEAD_DIM), lambda b, i, j: (0, 0))],
        out_specs=pl.BlockSpec((1, tr, cw), lambda b, i, j: (b, i, j)),
        out_shape=jax.ShapeDtypeStruct((n_batch, r_len, DN_V_W), BF16),
        compiler_params=_cp(("parallel", "parallel", "parallel")),
        name="deltanet_out",
    )(o2, ob, p3, norm_g.reshape(1, DN_HEAD_DIM))


def _seq_flip(a, s_len, axis):
    lat, ctx = jnp.split(a, [s_len], axis=axis)
    return jnp.concatenate([jnp.flip(lat, axis), jnp.flip(ctx, axis)], axis=axis)


def _deltanet_mixer(p3, conv_w, a_log, dt_bias, norm_g, *, s_len):
    n_batch, r_len, _ = p3.shape
    u = _dnprep_call(p3, conv_w, s_len=s_len)
    u2 = jnp.stack([u, _seq_flip(u, s_len, 1)])
    lg = p3[:, :, OFF_DL:OFF_DL + N_LOGITS].reshape(n_batch, r_len, 2, 2, DN_V_HEADS)

    def dirs(x):
        return jnp.stack([x[:, :, 0], _seq_flip(x[:, :, 1], s_len, 1)])

    la, lb = dirs(lg[:, :, 0]), dirs(lg[:, :, 1])
    padl = lambda x: jnp.pad(x, ((0, 0), (0, 0), (0, 0), (0, LANE - DN_V_HEADS)))
    a6, qk6, be, eg, ek, gt = _dnintra_call(u2, padl(la), padl(lb), jnp.swapaxes(la, 2, 3), a_log, dt_bias)
    t6 = _dnsolve_call(a6.reshape(-1, DN_CHUNK * DN_CHUNK)).reshape(a6.shape)
    o2 = _dnscan_call(u2, t6, qk6, be, eg, ek, gt, s_len=s_len)
    return _dnout_call(o2, _seq_flip(o2[1], s_len, 1), p3, norm_g)


def _merge_kernel(ya_ref, yb_ref, yc_ref, wa_ref, wb_ref, wc_ref, ga_ref, gb_ref, gc_ref, o_ref):
    m = (jax.nn.sigmoid(ga_ref[...]) * _dot(ya_ref[...], wa_ref[...])
         + jax.nn.sigmoid(gb_ref[...]) * _dot(yb_ref[...], wb_ref[...])
         + jax.nn.sigmoid(gc_ref[...]) * _dot(yc_ref[...], wc_ref[...]))
    o_ref[...] = m.astype(o_ref.dtype)


def _merge_call(ya, yb, yc, wa, wb, wc, p2, *, r_len):
    rt = ya.shape[0]
    d = wa.shape[1]
    tm = _pick(r_len, 1088, 16)
    tn = _pick(d, MERGE_ALIGN)
    assert OFF_MG % tn == 0
    yspec = lambda w: pl.BlockSpec((tm, w), lambda i, j: (i, 0))
    wspec = lambda w: pl.BlockSpec((w, tn), lambda i, j: (0, j))
    gspec = lambda br: pl.BlockSpec((tm, tn), lambda i, j, br=br: (i, (OFF_MG + br * d) // tn + j))
    return pl.pallas_call(
        _merge_kernel,
        grid=(rt // tm, d // tn),
        in_specs=[yspec(ATT_W), yspec(HY_W), yspec(DN_V_W), wspec(ATT_W), wspec(HY_W), wspec(DN_V_W),
                  gspec(0), gspec(1), gspec(2)],
        out_specs=pl.BlockSpec((tm, tn), lambda i, j: (i, j)),
        out_shape=jax.ShapeDtypeStruct((rt, d), BF16),
        compiler_params=_cp(("parallel", "parallel")),
        name="branch_merge",
    )(ya, yb, yc, wa, wb, wc, p2, p2, p2)


def _outproj_kernel(m_ref, w_ref, x_ref, gate_ref, o_ref, *, tm, tiles_per_batch, s_len, n_batch):
    is_ctx, gl, gc = _row_mods(gate_ref, pl.program_id(0), 0, tm, tm, tiles_per_batch, s_len, n_batch)
    o_ref[...] = x_ref[...] + jnp.where(is_ctx, gc, gl) * _dot(m_ref[...], w_ref[...])


def _outproj_call(m, w_out, xs2, mods, *, n_batch, r_len, s_len):
    rt, d = xs2.shape
    tm = _pick(r_len, 1088, 16)
    tn = _pick(d, 512)
    kern = functools.partial(_outproj_kernel, tm=tm, tiles_per_batch=r_len // tm, s_len=s_len, n_batch=n_batch)
    return pl.pallas_call(
        kern,
        grid=(rt // tm, d // tn),
        in_specs=[pl.BlockSpec((tm, d), lambda i, j: (i, 0)),
                  pl.BlockSpec((d, tn), lambda i, j: (0, j)),
                  pl.BlockSpec((tm, tn), lambda i, j: (i, j)),
                  pl.BlockSpec((8, tn), lambda i, j: (0, 2 * d // tn + j))],
        out_specs=pl.BlockSpec((tm, tn), lambda i, j: (i, j)),
        out_shape=jax.ShapeDtypeStruct((rt, d), F32),
        compiler_params=_cp(("parallel", "parallel")),
        name="out_proj_residual",
    )(m, w_out, xs2, mods)


def _finalnorm_kernel(x_ref, g_ref, o_ref):
    x = x_ref[0]
    o_ref[0] = x * lax.rsqrt(jnp.mean(x * x, axis=-1, keepdims=True) + NORM_EPS) * g_ref[...]


def _finalnorm_call(xs, final_g, *, s_len):
    n_batch, _, d = xs.shape
    tr = _pick(s_len, 512, 8)
    return pl.pallas_call(
        _finalnorm_kernel,
        grid=(n_batch, s_len // tr),
        in_specs=[pl.BlockSpec((1, tr, d), lambda b, i: (b, i, 0)), pl.BlockSpec((1, d), lambda b, i: (0, 0))],
        out_specs=pl.BlockSpec((1, tr, d), lambda b, i: (b, i, 0)),
        out_shape=jax.ShapeDtypeStruct((n_batch, s_len, d), F32),
        compiler_params=_cp(("parallel", "parallel")),
        name="final_norm",
    )(xs, final_g.reshape(1, d))


def _pad_w_in(w_in_l):
    d = w_in_l.shape[0]
    pad = jnp.zeros((d, OFF_MG - (OFF_DL + N_LOGITS)), w_in_l.dtype)
    return jnp.concatenate([w_in_l[:, :OFF_DL + N_LOGITS], pad, w_in_l[:, OFF_DL + N_LOGITS:]], axis=1).astype(BF16)


def kernel(x, c, ctx, c_ctx, norm_g, w_mod, b_mod, w_in, q_norm_g, k_norm_g, hy_conv_w, hy_conv_b, hy_w1, hy_b1, hy_freq1, hy_w2, hy_b2, hy_freq2, hy_w3, hy_d, dn_conv_w, dn_a_log, dn_dt_bias, dn_norm_g, w_pa, w_pb, w_pc, w_out, final_g):
    n_batch, s_len, d = x.shape
    ctx_len = ctx.shape[1]
    r_len = s_len + ctx_len
    depth = w_in.shape[0]
    assert n_batch + 1 <= 8 and w_in.shape[2] == OFF_DL + N_LOGITS + N_BRANCH * d

    xs = jnp.concatenate([x, ctx], axis=1)
    cs = jnp.zeros((8, d), F32).at[:n_batch].set(c).at[n_batch].set(c_ctx)
    mods = _mod_call(cs, w_mod, b_mod)
    cos_t, sin_t = _rope_tables(s_len, ctx_len)
    w1p = jnp.pad(hy_w1, ((0, 0), (0, LANE - HY_EMB), (0, 0)))

    for layer in range(depth):
        need_ctx = layer < depth - 1
        xs2 = xs.reshape(n_batch * r_len, d)
        p2 = _inproj_call(xs2, norm_g[layer], mods[layer], _pad_w_in(w_in[layer]),
                          n_batch=n_batch, r_len=r_len, s_len=s_len)
        p3 = p2.reshape(n_batch, r_len, -1)

        ya = _attn_call(p3, cos_t, sin_t, q_norm_g[layer], k_norm_g[layer], s_len=s_len)

        filt = (w1p[layer], hy_b1[layer], hy_freq1[layer], hy_w2[layer], hy_b2[layer], hy_freq2[layer],
                hy_w3[layer])
        yb = _hyena_mixer(p3, 0, s_len, hy_conv_w[layer], hy_conv_b[layer], hy_d[layer], filt)
        if need_ctx:
            yb_c = _hyena_mixer(p3, s_len // ctx_len, ctx_len, hy_conv_w[layer], hy_conv_b[layer],
                                hy_d[layer], filt)
        else:
            yb_c = jnp.zeros((n_batch, ctx_len, HY_W), BF16)
        yb = jnp.concatenate([yb, yb_c], axis=1)

        yc = _deltanet_mixer(p3, dn_conv_w[layer], dn_a_log[layer], dn_dt_bias[layer], dn_norm_g[layer],
                             s_len=s_len)

        rt = n_batch * r_len
        m = _merge_call(ya.reshape(rt, ATT_W), yb.reshape(rt, HY_W), yc.reshape(rt, DN_V_W),
                        w_pa[layer].astype(BF16), w_pb[layer].astype(BF16), w_pc[layer].astype(BF16),
                        p2, r_len=r_len)
        xs = _outproj_call(m, w_out[layer].astype(BF16), xs2, mods[layer],
                           n_batch=n_batch, r_len=r_len, s_len=s_len).reshape(n_batch, r_len, d)

    return _finalnorm_call(xs, final_g, s_len=s_len)
```

```python
import functools
import math

import jax
import jax.numpy as jnp
import numpy as np
from jax import lax
from jax.experimental import pallas as pl
from jax.experimental.pallas import tpu as pltpu

F32 = jnp.float32
BF16 = jnp.bfloat16

GRID_W = 64
NORM_EPS = 1e-6
N_BRANCH = 3

ATT_HEADS = 8
ATT_KV_HEADS = 2
HEAD_DIM = 128
ATT_GROUP = ATT_HEADS // ATT_KV_HEADS
ATT_W = ATT_HEADS * HEAD_DIM
ATT_KV_W = ATT_KV_HEADS * HEAD_DIM
AXIS_ROPE_DIM = HEAD_DIM // 2
ROPE_THETA = 10000.0

HY_W = 1024
HY_ORDER = 2
HY_BANDS = 16
HY_EMB = 1 + 2 * HY_BANDS
HY_FILTER_HIDDEN = 64
HY_DECAY_TARGET = 1e-2
HY_DECAY_FAST = 0.3
HY_DECAY_SLOW = 1.5

DN_QK_HEADS = 4
DN_V_HEADS = 8
DN_HEAD_DIM = 128
DN_QK_W = DN_QK_HEADS * DN_HEAD_DIM
DN_V_W = DN_V_HEADS * DN_HEAD_DIM
DN_CHUNK = 64
DN_W = 2 * DN_QK_W + DN_V_W

LANE = 128
MIB = 1024 * 1024

OFF_AQ = 0
OFF_AK = OFF_AQ + ATT_W
OFF_AV = OFF_AK + ATT_KV_W
OFF_AG = OFF_AV + ATT_KV_W
OFF_HV = OFF_AG + ATT_W
OFF_HX1 = OFF_HV + HY_W
OFF_HX2 = OFF_HX1 + HY_W
OFF_HG = OFF_HX2 + HY_W
OFF_DQ = OFF_HG + HY_W
OFF_DV = OFF_DQ + 2 * DN_QK_W
OFF_DZ = OFF_DV + DN_V_W
OFF_DL = OFF_DZ + DN_V_W
N_LOGITS = 4 * DN_V_HEADS
MERGE_ALIGN = 512
OFF_MG = -(-(OFF_DL + N_LOGITS) // MERGE_ALIGN) * MERGE_ALIGN


def _pick(n, cap, mult=LANE):
    best = None
    for t in range(mult, min(n, cap) + 1, mult):
        if n % t == 0:
            best = t
    assert best is not None, (n, cap, mult)
    return best


def _cp(sem, vmem_mib=48):
    return pltpu.CompilerParams(dimension_semantics=sem, vmem_limit_bytes=vmem_mib * MIB)


def _silu(x):
    return x * jax.nn.sigmoid(x)


def _split3(x):
    x1 = x.astype(BF16)
    r = x - x1.astype(F32)
    x2 = r.astype(BF16)
    x3 = (r - x2.astype(F32)).astype(BF16)
    return x1, x2, x3


def _dot(a, b):
    return jnp.dot(a, b, preferred_element_type=F32)


def _dot_hp(a, b):
    a1 = a.astype(BF16)
    a2 = (a - a1.astype(F32)).astype(BF16)
    b1 = b.astype(BF16)
    b2 = (b - b1.astype(F32)).astype(BF16)
    return _dot(a1, b1) + (_dot(a1, b2) + _dot(a2, b1))


def _mod_kernel(cs_ref, w_ref, b_ref, o_ref):
    cs = cs_ref[...]
    o_ref[0] = _dot(_silu(cs).astype(BF16), w_ref[0].astype(BF16)) + b_ref[0]


def _mod_call(cs, w_mod, b_mod):
    depth, d, d3 = w_mod.shape
    tn = _pick(d3, 512)
    return pl.pallas_call(
        _mod_kernel,
        grid=(depth, d3 // tn),
        in_specs=[pl.BlockSpec((8, d), lambda l, j: (0, 0)),
                  pl.BlockSpec((1, d, tn), lambda l, j: (l, 0, j)),
                  pl.BlockSpec((1, 1, tn), lambda l, j: (l, 0, j))],
        out_specs=pl.BlockSpec((1, 8, tn), lambda l, j: (l, 0, j)),
        out_shape=jax.ShapeDtypeStruct((depth, 8, d3), F32),
        compiler_params=_cp(("parallel", "parallel")),
        name="adaln_mod",
    )(cs, w_mod, b_mod.reshape(depth, 1, d3))


def _row_mods(mod_ref, i, r0, rows, tm, tiles_per_batch, s_len, n_batch):
    b = i // tiles_per_batch
    row = (i % tiles_per_batch) * tm + r0 + lax.broadcasted_iota(jnp.int32, (rows, 1), 0)
    return row >= s_len, mod_ref[pl.ds(b, 1), :], mod_ref[n_batch:n_batch + 1, :]


NORM_ROWS = 64


def _inproj_kernel(x_ref, g_ref, mod_ref, w_ref, o_ref, h_ref, *, tm, tiles_per_batch, s_len, n_batch, d):
    i = pl.program_id(0)

    @pl.when(pl.program_id(1) == 0)
    def _():
        def body(c, carry):
            r0 = pl.multiple_of(c * NORM_ROWS, NORM_ROWS)
            x = x_ref[pl.ds(r0, NORM_ROWS), :]
            y = x * lax.rsqrt(jnp.mean(x * x, axis=-1, keepdims=True) + NORM_EPS) * g_ref[...]
            is_ctx, ml, mc = _row_mods(mod_ref, i, r0, NORM_ROWS, tm, tiles_per_batch, s_len, n_batch)
            shift = jnp.where(is_ctx, mc[:, :d], ml[:, :d])
            scale = jnp.where(is_ctx, mc[:, d:2 * d], ml[:, d:2 * d])
            h_ref[pl.ds(r0, NORM_ROWS), :] = (y * (1.0 + scale) + shift).astype(BF16)
            return carry
        lax.fori_loop(0, tm // NORM_ROWS, body, 0)

    o_ref[...] = _dot(h_ref[...], w_ref[...])


def _inproj_call(xs2, norm_g, mods, w_bf, *, n_batch, r_len, s_len):
    rt, d = xs2.shape
    nw = w_bf.shape[1]
    tm = _pick(r_len, 1088, NORM_ROWS)
    tn = _pick(nw, 1024)
    kern = functools.partial(_inproj_kernel, tm=tm, tiles_per_batch=r_len // tm, s_len=s_len,
                             n_batch=n_batch, d=d)
    return pl.pallas_call(
        kern,
        grid=(rt // tm, nw // tn),
        in_specs=[pl.BlockSpec((tm, d), lambda i, j: (i, 0)),
                  pl.BlockSpec((1, d), lambda i, j: (0, 0)),
                  pl.BlockSpec((8, 3 * d), lambda i, j: (0, 0)),
                  pl.BlockSpec((d, tn), lambda i, j: (0, j))],
        out_specs=pl.BlockSpec((tm, tn), lambda i, j: (i, j)),
        out_shape=jax.ShapeDtypeStruct((rt, nw), F32),
        scratch_shapes=[pltpu.VMEM((tm, d), BF16)],
        compiler_params=_cp(("parallel", "arbitrary")),
        name="norm_inproj",
    )(xs2, norm_g.reshape(1, d), mods, w_bf)


def _rope_tables(s_len, ctx_len):
    m = AXIS_ROPE_DIM // 2
    inv_freq = ROPE_THETA ** (-np.arange(0, AXIS_ROPE_DIM, 2, dtype=np.float64) / AXIS_ROPE_DIM)
    t = np.arange(s_len)
    ang_r = (t // GRID_W)[:, None] * inv_freq
    ang_c = (t % GRID_W)[:, None] * inv_freq
    cos = np.concatenate([np.cos(ang_r)] * 2 + [np.cos(ang_c)] * 2, axis=-1)
    sin = np.concatenate([-np.sin(ang_r), np.sin(ang_r), -np.sin(ang_c), np.sin(ang_c)], axis=-1)
    assert cos.shape[1] == 4 * m == HEAD_DIM
    cos = np.concatenate([cos, np.ones((ctx_len, HEAD_DIM))], axis=0)
    sin = np.concatenate([sin, np.zeros((ctx_len, HEAD_DIM))], axis=0)
    return jnp.asarray(cos, F32), jnp.asarray(sin, F32)


def _norm_rope(x, g, cs, sn):
    y = x * lax.rsqrt(jnp.mean(x * x, axis=-1, keepdims=True) + NORM_EPS) * g
    lane = lax.broadcasted_iota(jnp.int32, (1, HEAD_DIM), 1)
    first = (lane % AXIS_ROPE_DIM) < (AXIS_ROPE_DIM // 2)
    q = AXIS_ROPE_DIM // 2
    partner = jnp.where(first, pltpu.roll(y, HEAD_DIM - q, 1), pltpu.roll(y, q, 1))
    return y * cs + partner * sn


def _attn_kernel(q_ref, k_ref, v_ref, gt_ref, cos_ref, sin_ref, qg_ref, kg_ref, o_ref, ks_ref, vs_ref,
                 *, tq, r_len, s_len, kv_chunk):
    i = pl.program_id(2)

    @pl.when(i == 0)
    def _():
        def body(c, carry):
            r0 = pl.multiple_of(c * kv_chunk, kv_chunk)
            kk = _norm_rope(k_ref[0, pl.ds(r0, kv_chunk), :], kg_ref[...],
                            cos_ref[pl.ds(r0, kv_chunk), :], sin_ref[pl.ds(r0, kv_chunk), :])
            ks_ref[pl.ds(r0, kv_chunk), :] = kk.astype(BF16)
            vs_ref[pl.ds(r0, kv_chunk), :] = v_ref[0, pl.ds(r0, kv_chunk), :].astype(BF16)
            return carry
        lax.fori_loop(0, r_len // kv_chunk, body, 0)

    r0 = pl.multiple_of(i * tq, tq)
    cs = cos_ref[pl.ds(r0, tq), :]
    sn = sin_ref[pl.ds(r0, tq), :]
    scale = HEAD_DIM ** -0.5

    def heads(k_lo, k_hi):
        for g in range(ATT_GROUP):
            sl = slice(g * HEAD_DIM, (g + 1) * HEAD_DIM)
            qh = (_norm_rope(q_ref[0, :, sl], qg_ref[...], cs, sn) * scale).astype(BF16)
            s = lax.dot_general(qh, ks_ref[k_lo:k_hi, :], (((1,), (1,)), ((), ())),
                                preferred_element_type=F32)
            e = jnp.exp(s - jnp.max(s, axis=-1, keepdims=True))
            den = jnp.sum(e, axis=-1, keepdims=True)
            o = _dot(e.astype(BF16), vs_ref[k_lo:k_hi, :]) / den
            o_ref[0, :, sl] = (o * _silu(gt_ref[0, :, sl])).astype(o_ref.dtype)

    @pl.when(i < s_len // tq)
    def _():
        heads(0, r_len)

    @pl.when(i >= s_len // tq)
    def _():
        heads(s_len, r_len)


def _attn_call(p3, cos_t, sin_t, q_g, k_g, *, s_len):
    n_batch, r_len, _ = p3.shape
    ctx_len = r_len - s_len
    tq = 256 if (ctx_len % 256 == 0 and s_len % 256 == 0) else 128
    assert ctx_len % tq == 0 and s_len % tq == 0
    gw = ATT_GROUP * HEAD_DIM
    kern = functools.partial(_attn_kernel, tq=tq, r_len=r_len, s_len=s_len, kv_chunk=tq)
    return pl.pallas_call(
        kern,
        grid=(n_batch, ATT_KV_HEADS, r_len // tq),
        in_specs=[pl.BlockSpec((1, tq, gw), lambda b, h, i: (b, i, OFF_AQ // gw + h)),
                  pl.BlockSpec((1, r_len, HEAD_DIM), lambda b, h, i: (b, 0, OFF_AK // HEAD_DIM + h)),
                  pl.BlockSpec((1, r_len, HEAD_DIM), lambda b, h, i: (b, 0, OFF_AV // HEAD_DIM + h)),
                  pl.BlockSpec((1, tq, gw), lambda b, h, i: (b, i, OFF_AG // gw + h)),
                  pl.BlockSpec((r_len, HEAD_DIM), lambda b, h, i: (0, 0)),
                  pl.BlockSpec((r_len, HEAD_DIM), lambda b, h, i: (0, 0)),
                  pl.BlockSpec((1, HEAD_DIM), lambda b, h, i: (0, 0)),
                  pl.BlockSpec((1, HEAD_DIM), lambda b, h, i: (0, 0))],
        out_specs=pl.BlockSpec((1, tq, gw), lambda b, h, i: (b, i, h)),
        out_shape=jax.ShapeDtypeStruct((n_batch, r_len, ATT_W), BF16),
        scratch_shapes=[pltpu.VMEM((r_len, HEAD_DIM), BF16), pltpu.VMEM((r_len, HEAD_DIM), BF16)],
        compiler_params=_cp(("parallel", "parallel", "arbitrary")),
        name="gqa_attention",
    )(p3, p3, p3, p3, cos_t, sin_t, q_g.reshape(1, HEAD_DIM), k_g.reshape(1, HEAD_DIM))


def _conv3(x, w, first, last):
    n = x.shape[0]
    prev = jnp.where(first, 0.0, pltpu.roll(x, 1, 0))
    nxt = jnp.where(last, 0.0, pltpu.roll(x, n - 1, 0))
    return prev * w[0:1, :] + x * w[1:2, :] + nxt * w[2:3, :]


FFT_UNROLL = 8


def _fft_split(n):
    if n <= 512:
        return 1, n
    n2 = 64
    return n // n2, n2


@functools.lru_cache(maxsize=None)
def _fft_mats(n, n1, n2):
    half = n2 // 2
    a = np.arange(n1)[:, None, None]
    k2 = np.arange(n2)[None, :, None]

    def g(bs):
        ph = (k2 * (a + n1 * bs[None, None, :])) % n
        ang = -2.0 * np.pi * ph / n
        return np.cos(ang), np.sin(ang)

    gre, gim = g(np.arange(half))
    gal = np.concatenate([gre, gim], axis=1)
    gar = np.concatenate([-gim, gre], axis=1)
    hre = np.swapaxes(gre, 1, 2) / n
    him = -np.swapaxes(gim, 1, 2) / n
    hal = np.concatenate([hre, him], axis=1)
    har = np.concatenate([-him, hre], axis=1)
    fre_f, fim_f = g(np.arange(n2))
    gf = np.concatenate([fre_f, fim_f], axis=1)
    k1 = np.arange(n1)
    ang1 = -2.0 * np.pi * ((k1[:, None] * k1[None, :]) % n1) / n1
    f1re, f1im = np.cos(ang1), np.sin(ang1)
    fbl = np.concatenate([f1re, f1im], axis=0)
    fbr = np.concatenate([-f1im, f1re], axis=0)
    fbil = np.concatenate([f1re, -f1im], axis=0)
    fbir = np.concatenate([f1im, f1re], axis=0)
    cast = lambda m: jnp.asarray(m, BF16)
    return dict(ga=cast(np.concatenate([gal, gar], axis=2)), ha=cast(np.concatenate([hal, har], axis=2)),
                gf=cast(gf), fb=cast(np.concatenate([fbl, fbr], axis=1)),
                fbi=cast(np.concatenate([fbil, fbir], axis=1)))


@functools.lru_cache(maxsize=None)
def _hy_tables(n_tok):
    pos = np.arange(n_tok, dtype=np.float64)
    t = pos / max(n_tok - 1, 1)
    bands = np.linspace(1e-4, HY_BANDS - 1, HY_BANDS)
    ang = (2.0 * np.pi / n_tok) * pos[:, None] * bands
    z = np.concatenate([t[:, None], np.cos(ang), np.sin(ang)], axis=-1)
    zrev = np.zeros_like(z)
    zrev[1:] = z[:0:-1]
    ztab = np.zeros((2 * n_tok, LANE))
    ztab[:, :HY_EMB] = np.concatenate([z, zrev], axis=0)
    deltas = np.abs(np.linspace(math.log(HY_DECAY_TARGET) / HY_DECAY_SLOW,
                                math.log(HY_DECAY_TARGET) / HY_DECAY_FAST, HY_W))
    return jnp.asarray(ztab, F32), jnp.asarray(np.tile(deltas, HY_ORDER)[None, :], F32)


def _hyfilt_kernel(z_ref, w1_ref, b1_ref, f1_ref, w2_ref, b2_ref, f2_ref, w3_ref, dl_ref, k_ref, s_ref,
                   *, n_tok, tr):
    i = pl.program_id(0)
    z = z_ref[...]
    h = jnp.sin(f1_ref[...] * (_dot_hp(z, w1_ref[...]) + b1_ref[...]))
    h = jnp.sin(f2_ref[...] * (_dot_hp(h, w2_ref[...]) + b2_ref[...]))
    h = _dot_hp(h, w3_ref[...]) * jnp.exp(-z[:, 0:1] * dl_ref[...])
    row = i * tr + lax.broadcasted_iota(jnp.int32, (tr, 1), 0)
    h = jnp.where(row == n_tok, 0.0, h)
    k_ref[...] = h

    @pl.when(i == 0)
    def _():
        s_ref[...] = jnp.zeros_like(s_ref)

    s_ref[...] += jnp.sum(jnp.abs(h), axis=0, keepdims=True)


def _hyfilt_call(n_tok, w1p, b1, f1, w2, b2, f2, w3):
    ztab, dl = _hy_tables(n_tok)
    n = 2 * n_tok
    tr = _pick(n_tok, 512, 8)
    ow = HY_ORDER * HY_W
    hid = HY_FILTER_HIDDEN
    kern = functools.partial(_hyfilt_kernel, n_tok=n_tok, tr=tr)
    c2 = lambda i: (0, 0)
    return pl.pallas_call(
        kern,
        grid=(n // tr,),
        in_specs=[pl.BlockSpec((tr, LANE), lambda i: (i, 0)),
                  pl.BlockSpec((LANE, hid), c2), pl.BlockSpec((1, hid), c2), pl.BlockSpec((1, hid), c2),
                  pl.BlockSpec((hid, hid), c2), pl.BlockSpec((1, hid), c2), pl.BlockSpec((1, hid), c2),
                  pl.BlockSpec((hid, ow), lambda i: (0, i // (n_tok // tr))),
                  pl.BlockSpec((1, ow), c2)],
        out_specs=[pl.BlockSpec((tr, ow), lambda i: (i, 0)), pl.BlockSpec((1, ow), c2)],
        out_shape=[jax.ShapeDtypeStruct((n, ow), F32), jax.ShapeDtypeStruct((1, ow), F32)],
        compiler_params=_cp(("arbitrary",)),
        name="hyena_filter",
    )(ztab, w1p, b1.reshape(1, hid), f1.reshape(1, hid), w2, b2.reshape(1, hid), f2.reshape(1, hid), w3, dl)


FFT_GROUP = 4


def _filtfft_kernel(k_ref, s_ref, gf_ref, fb_ref, o_ref, z_ref, *, n, n1, n2):
    inv = 1.0 / (s_ref[...] + 1e-6)
    if n1 == 1:
        o_ref[...] = _dot(gf_ref[0], k_ref[...].astype(BF16)) * inv
        return
    slab = 2 * n1

    def stage_a(a, carry):
        out = _dot(gf_ref[a], k_ref[pl.ds(a, n2, stride=n1), :].astype(BF16))
        z_ref[pl.ds(a, n2, stride=slab), :] = out[:n2]
        z_ref[pl.ds(n1 + a, n2, stride=slab), :] = out[n2:]
        return carry
    lax.fori_loop(0, n1, stage_a, 0, unroll=FFT_UNROLL)

    def stage_b(g, carry):
        for u in range(FFT_GROUP):
            r0 = pl.multiple_of((g * FFT_GROUP + u) * slab, slab)
            o_ref[pl.ds(r0, slab), :] = _dot(fb_ref[...], z_ref[pl.ds(r0, slab), :].astype(BF16)) * inv
        return carry
    lax.fori_loop(0, n2 // FFT_GROUP, stage_b, 0)


def _filtfft_call(k_un, asum):
    n, ow = k_un.shape
    n1, n2 = _fft_split(n)
    m = _fft_mats(n, n1, n2)
    cb = LANE
    kern = functools.partial(_filtfft_kernel, n=n, n1=n1, n2=n2)
    return pl.pallas_call(
        kern,
        grid=(ow // cb,),
        in_specs=[pl.BlockSpec((n, cb), lambda j: (0, j)),
                  pl.BlockSpec((1, cb), lambda j: (0, j)),
                  pl.BlockSpec(m["gf"].shape, lambda j: (0, 0, 0)),
                  pl.BlockSpec(m["fb"].shape, lambda j: (0, 0))],
        out_specs=pl.BlockSpec((2 * n, cb), lambda j: (0, j)),
        out_shape=jax.ShapeDtypeStruct((2 * n, ow), F32),
        scratch_shapes=[pltpu.VMEM((2 * n, cb), F32)],
        compiler_params=_cp(("parallel",)),
        name="hyena_filter_fft",
    )(k_un, asum, m["gf"], m["fb"])


def _hyena_kernel(*refs, n_tok, n1, n2, conv_a, has_gate):
    it = iter(refs)
    a_ref, m_ref = next(it), next(it)
    g_ref = next(it) if has_gate else None
    if conv_a:
        cwa_ref, cba_ref = next(it), next(it)
    cwm_ref, cbm_ref, d_ref, ks_ref = next(it), next(it), next(it), next(it)
    ga_ref, ha_ref = next(it), next(it)
    if n1 > 1:
        fb_ref, fbi_ref = next(it), next(it)
    o_ref, ac_ref, z_ref = next(it), next(it), next(it)

    n = 2 * n_tok
    half = n2 // 2
    row = lax.broadcasted_iota(jnp.int32, (n_tok, 1), 0)
    first, last = row == 0, row == n_tok - 1
    for b in range(2):
        a = a_ref[b]
        if conv_a:
            a = _conv3(a, cwa_ref[...], first, last) + cba_ref[...]
        ac_ref[b] = a

    def spectrum_mul(x, k, h):
        xre, xim, kre, kim = x[:h], x[h:], k[:h], k[h:]
        return jnp.concatenate([xre * kre - xim * kim, xre * kim + xim * kre], axis=0).astype(BF16)

    if n1 == 1:
        x = _dot(ga_ref[0], jnp.concatenate([ac_ref[0], ac_ref[1]], axis=0).astype(BF16))
        y = _dot(ha_ref[0], spectrum_mul(x, ks_ref[...], n))
        for b in range(2):
            ac_ref[b] = y[b * n_tok:(b + 1) * n_tok] + ac_ref[b] * d_ref[0]
    else:
        slab = 2 * n1

        def stage_a(a, carry):
            rows = jnp.concatenate([ac_ref[0, pl.ds(a, half, stride=n1), :],
                                    ac_ref[1, pl.ds(a, half, stride=n1), :]], axis=0).astype(BF16)
            out = _dot(ga_ref[a], rows)
            z_ref[pl.ds(a, n2, stride=slab), :] = out[:n2]
            z_ref[pl.ds(n1 + a, n2, stride=slab), :] = out[n2:]
            return carry
        lax.fori_loop(0, n1, stage_a, 0, unroll=FFT_UNROLL)

        def stage_b(g, carry):
            r0s = [pl.multiple_of((g * FFT_GROUP + u) * slab, slab) for u in range(FFT_GROUP)]
            xs = [_dot(fb_ref[...], z_ref[pl.ds(r0, slab), :].astype(BF16)) for r0 in r0s]
            ys = [spectrum_mul(x, ks_ref[pl.ds(r0, slab), :], n1) for x, r0 in zip(xs, r0s)]
            for y, r0 in zip(ys, r0s):
                z_ref[pl.ds(r0, slab), :] = _dot(fbi_ref[...], y)
            return carry
        lax.fori_loop(0, n2 // FFT_GROUP, stage_b, 0)

        def stage_a_inv(a, carry):
            rows = jnp.concatenate([z_ref[pl.ds(a, n2, stride=slab), :],
                                    z_ref[pl.ds(n1 + a, n2, stride=slab), :]], axis=0).astype(BF16)
            y = _dot(ha_ref[a], rows)
            for b in range(2):
                cur = ac_ref[b, pl.ds(a, half, stride=n1), :]
                ac_ref[b, pl.ds(a, half, stride=n1), :] = y[b * half:(b + 1) * half] + cur * d_ref[0]
            return carry
        lax.fori_loop(0, n1, stage_a_inv, 0, unroll=FFT_UNROLL)

    for b in range(2):
        mc = _conv3(m_ref[b], cwm_ref[...], first, last) + cbm_ref[...]
        out = mc * ac_ref[b]
        if has_gate:
            out = out * _silu(g_ref[b])
        o_ref[b] = out.astype(o_ref.dtype)


def _hyena_call(a_arr, a_blk, m_arr, m_blk, gate, conv_a, conv_m, d_row, spec, spec_blk, *, n_tok, out_dtype):
    n_batch = a_arr.shape[0]
    assert n_batch % 2 == 0
    n = 2 * n_tok
    n1, n2 = _fft_split(n)
    m = _fft_mats(n, n1, n2)
    cb = LANE
    one = pl.Buffered(1)

    def tile(blk):
        return pl.BlockSpec((2, n_tok, cb), lambda p, j, blk=blk: (p, blk[0], blk[1] + j), pipeline_mode=one)

    def convspec(c0):
        return [pl.BlockSpec((3, cb), lambda p, j, c0=c0: (0, c0 + j)),
                pl.BlockSpec((1, cb), lambda p, j, c0=c0: (0, c0 + j))]

    ins, specs = [a_arr, m_arr], [tile(a_blk), tile(m_blk)]
    if gate is not None:
        ins.append(gate[0])
        specs.append(tile(gate[1]))
    if conv_a is not None:
        ins += [conv_a[0], conv_a[1]]
        specs += convspec(conv_a[2])
    ins += [conv_m[0], conv_m[1], d_row, spec]
    specs += convspec(conv_m[2])
    specs += [pl.BlockSpec((1, 1, cb), lambda p, j: (0, 0, j)),
              pl.BlockSpec((2 * n, cb), lambda p, j, s0=spec_blk: (0, s0 + j), pipeline_mode=one)]
    names = ["ga", "ha"] + (["fb", "fbi"] if n1 > 1 else [])
    for nm in names:
        ins.append(m[nm])
        specs.append(pl.BlockSpec(m[nm].shape, lambda p, j, nd=m[nm].ndim: (0,) * nd, pipeline_mode=one))
    kern = functools.partial(_hyena_kernel, n_tok=n_tok, n1=n1, n2=n2, conv_a=conv_a is not None,
                             has_gate=gate is not None)
    return pl.pallas_call(
        kern,
        grid=(n_batch // 2, HY_W // cb),
        in_specs=specs,
        out_specs=pl.BlockSpec((2, n_tok, cb), lambda p, j: (p, 0, j)),
        out_shape=jax.ShapeDtypeStruct((n_batch, n_tok, HY_W), out_dtype),
        scratch_shapes=[pltpu.VMEM((2, n_tok, cb), F32), pltpu.VMEM((2 * n, cb), F32)],
        compiler_params=_cp(("parallel", "arbitrary"), 56),
        name="hyena_conv",
    )(*ins)


def _hyena_mixer(p3, row_blk, n_tok, conv_w, conv_b, hy_d, filt):
    k_un, asum = _hyfilt_call(n_tok, *filt)
    spec = _filtfft_call(k_un, asum)
    cbias = conv_b.reshape(1, -1)
    cblk = HY_W // LANE
    d3 = hy_d.reshape(HY_ORDER, 1, HY_W)
    z = _hyena_call(p3, (row_blk, OFF_HV // LANE), p3, (row_blk, OFF_HX1 // LANE), None,
                    (conv_w, cbias, 0), (conv_w, cbias, cblk), d3[0:1], spec, 0,
                    n_tok=n_tok, out_dtype=F32)
    return _hyena_call(z, (0, 0), p3, (row_blk, OFF_HX2 // LANE), (p3, (row_blk, OFF_HG // LANE)),
                       None, (conv_w, cbias, 2 * cblk), d3[1:2], spec, cblk,
                       n_tok=n_tok, out_dtype=BF16)


def _softplus(x):
    return jnp.maximum(x, 0.0) + jnp.log1p(jnp.exp(-jnp.abs(x)))


FLIP_ROWS = 256


def _exchange_matrix():
    return jnp.asarray(np.eye(FLIP_ROWS)[::-1], BF16)


def _flip_rows(jm, x):
    x1, x2, x3 = _split3(x)
    return _dot(jm, x1) + (_dot(jm, x2) + _dot(jm, x3))


def _dnprep_kernel(x_ref, w_ref, jm_ref, o_ref, *, r_len, s_len):
    j = pl.program_id(1)
    row = lax.broadcasted_iota(jnp.int32, (r_len, 1), 0)
    first = (row == 0) | (row == s_len)
    last = (row == s_len - 1) | (row == r_len - 1)
    u = _silu(_conv3(x_ref[0], w_ref[...], first, last))
    nrm = u * lax.rsqrt(jnp.sum(u * u, axis=-1, keepdims=True) + 1e-6)
    o_ref[0, 0] = jnp.where(j < 2 * DN_QK_HEADS, nrm, u)
    for seg0, seg_len in ((0, s_len), (s_len, r_len - s_len)):
        nt = seg_len // FLIP_ROWS
        for t in range(nt):
            src = seg0 + t * FLIP_ROWS
            dst = seg0 + (nt - 1 - t) * FLIP_ROWS
            o_ref[1, 0, dst:dst + FLIP_ROWS, :] = _flip_rows(jm_ref[...], o_ref[0, 0, src:src + FLIP_ROWS, :])


def _dnprep_call(p3, conv_w, *, s_len):
    n_batch, r_len, _ = p3.shape
    assert s_len % FLIP_ROWS == 0 and (r_len - s_len) % FLIP_ROWS == 0
    kern = functools.partial(_dnprep_kernel, r_len=r_len, s_len=s_len)
    return pl.pallas_call(
        kern,
        grid=(n_batch, DN_W // LANE),
        in_specs=[pl.BlockSpec((1, r_len, LANE), lambda b, j: (b, 0, OFF_DQ // LANE + j)),
                  pl.BlockSpec((3, LANE), lambda b, j: (0, j)),
                  pl.BlockSpec((FLIP_ROWS, FLIP_ROWS), lambda b, j: (0, 0))],
        out_specs=pl.BlockSpec((2, 1, r_len, LANE), lambda b, j: (0, b, 0, j)),
        out_shape=jax.ShapeDtypeStruct((2, n_batch, r_len, DN_W), F32),
        compiler_params=_cp(("parallel", "parallel")),
        name="deltanet_prep",
    )(p3, conv_w, _exchange_matrix())


def _tri_mats(tr):
    idx = np.arange(tr)
    same = (idx[:, None] // DN_CHUNK) == (idx[None, :] // DN_CHUNK)
    low = same & (idx[:, None] >= idx[None, :])
    return jnp.asarray(low, BF16), jnp.asarray(low.T, BF16), jnp.asarray(same, BF16)


def _dnintra_kernel(u_ref, la_ref, lb_ref, lat_ref, alr_ref, dtr_ref, alc_ref, dtc_ref, low_ref, upp_ref,
                    one_ref, a_ref, qk_ref, be_ref, eg_ref, ek_ref, gt_ref, *, tr):
    g_col = -jnp.exp(alr_ref[0]) * _softplus(la_ref[0, 0] + dtr_ref[0])
    beta = jax.nn.sigmoid(lb_ref[0, 0])
    g1, g2, g3 = _split3(g_col)
    gc_col = _dot(low_ref[...], g1) + (_dot(low_ref[...], g2) + _dot(low_ref[...], g3))
    gt_col = _dot(one_ref[...], g1) + (_dot(one_ref[...], g2) + _dot(one_ref[...], g3))
    be_ref[0, 0] = beta
    eg_ref[0, 0] = jnp.exp(gc_col)
    ek_ref[0, 0] = jnp.exp(gt_col - gc_col)
    gt_ref[0, 0] = jnp.exp(gt_col)
    g_row = -jnp.exp(alc_ref[0]) * _softplus(lat_ref[0, 0] + dtc_ref[0])
    r1, r2, r3 = _split3(g_row)
    gc_row = _dot(r1, upp_ref[...]) + (_dot(r2, upp_ref[...]) + _dot(r3, upp_ref[...]))

    ii = lax.broadcasted_iota(jnp.int32, (DN_CHUNK, DN_CHUNK), 0)
    jj = lax.broadcasted_iota(jnp.int32, (DN_CHUNK, DN_CHUNK), 1)
    scale = DN_HEAD_DIM ** -0.5
    nt = (((1,), (1,)), ((), ()))
    for c in range(tr // DN_CHUNK):
        rows = slice(c * DN_CHUNK, (c + 1) * DN_CHUNK)
        for hq in range(DN_QK_HEADS):
            q = u_ref[0, 0, rows, hq * DN_HEAD_DIM:(hq + 1) * DN_HEAD_DIM].astype(BF16)
            k = u_ref[0, 0, rows, DN_QK_W + hq * DN_HEAD_DIM:DN_QK_W + (hq + 1) * DN_HEAD_DIM].astype(BF16)
            kk = lax.dot_general(k, k, nt, preferred_element_type=F32)
            qk = lax.dot_general(q, k, nt, preferred_element_type=F32) * scale
            for h in range(hq * (DN_V_HEADS // DN_QK_HEADS), (hq + 1) * (DN_V_HEADS // DN_QK_HEADS)):
                diff = gc_col[rows, h:h + 1] - gc_row[h:h + 1, rows]
                dec = jnp.where(ii >= jj, jnp.exp(jnp.minimum(diff, 0.0)), 0.0)
                a_ref[0, 0, h, c] = jnp.where(ii > jj, kk * beta[rows, h:h + 1] * dec, 0.0)
                qk_ref[0, 0, h, c] = qk * dec


def _dnintra_call(u2, la, lb, lat, alog, dtb):
    n_dir, n_batch, r_len, _ = u2.shape
    tr = 4 * DN_CHUNK
    nc = r_len // DN_CHUNK
    low, upp, one = _tri_mats(tr)
    kern = functools.partial(_dnintra_kernel, tr=tr)
    pad = lambda v: jnp.pad(v, ((0, 0), (0, LANE - DN_V_HEADS))).reshape(n_dir, 1, LANE)
    col = lambda v: v.reshape(n_dir, DN_V_HEADS, 1)
    cm = lambda d, b, t: (0, 0)
    gspec = pl.BlockSpec((1, 1, tr, LANE), lambda d, b, t: (d, b, t, 0))
    mspec = pl.BlockSpec((1, 1, DN_V_HEADS, tr // DN_CHUNK, DN_CHUNK, DN_CHUNK), lambda d, b, t: (d, b, 0, t, 0, 0))
    gshape = jax.ShapeDtypeStruct((n_dir, n_batch, r_len, LANE), F32)
    mshape = jax.ShapeDtypeStruct((n_dir, n_batch, DN_V_HEADS, nc, DN_CHUNK, DN_CHUNK), F32)
    return pl.pallas_call(
        kern,
        grid=(n_dir, n_batch, r_len // tr),
        in_specs=[pl.BlockSpec((1, 1, tr, 2 * DN_QK_W), lambda d, b, t: (d, b, t, 0)),
                  gspec, gspec,
                  pl.BlockSpec((1, 1, DN_V_HEADS, tr), lambda d, b, t: (d, b, 0, t)),
                  pl.BlockSpec((1, 1, LANE), lambda d, b, t: (d, 0, 0)),
                  pl.BlockSpec((1, 1, LANE), lambda d, b, t: (d, 0, 0)),
                  pl.BlockSpec((1, DN_V_HEADS, 1), lambda d, b, t: (d, 0, 0)),
                  pl.BlockSpec((1, DN_V_HEADS, 1), lambda d, b, t: (d, 0, 0)),
                  pl.BlockSpec((tr, tr), cm), pl.BlockSpec((tr, tr), cm), pl.BlockSpec((tr, tr), cm)],
        out_specs=[mspec, mspec, gspec, gspec, gspec, gspec],
        out_shape=[mshape, mshape, gshape, gshape, gshape, gshape],
        compiler_params=_cp(("parallel", "parallel", "parallel")),
        name="deltanet_intra",
    )(u2, la, lb, lat, pad(alog), pad(dtb), col(alog), col(dtb), low, upp, one)


def _dnsolve_kernel(a_ref, o_ref, at_ref, tt_ref):
    c = DN_CHUNK
    for blk in range(c * c // LANE):
        at_ref[blk * LANE:(blk + 1) * LANE, :] = a_ref[:, blk * LANE:(blk + 1) * LANE].T
    tt_ref[...] = jnp.zeros_like(tt_ref)
    for i in range(c):
        nr = 8 * (i // 8 + 1)
        rr = lax.broadcasted_iota(jnp.int32, (nr, LANE), 0)
        acc = jnp.where(rr == i, 1.0, 0.0)

        def body(j, acc, i=i, nr=nr):
            a = at_ref[pl.ds(i * c + j, 1), :]
            return acc - a * tt_ref[pl.ds(pl.multiple_of(j * c, c), nr), :]
        if i > 0:
            acc = lax.fori_loop(0, i, body, acc, unroll=min(i, 8))
        tt_ref[i * c:i * c + nr, :] = acc
    for blk in range(c * c // LANE):
        o_ref[:, blk * LANE:(blk + 1) * LANE] = tt_ref[blk * LANE:(blk + 1) * LANE, :].T


def _dnsolve_call(a2):
    ni, cc = a2.shape
    assert ni % LANE == 0
    return pl.pallas_call(
        _dnsolve_kernel,
        grid=(ni // LANE,),
        in_specs=[pl.BlockSpec((LANE, cc), lambda i: (i, 0))],
        out_specs=pl.BlockSpec((LANE, cc), lambda i: (i, 0)),
        out_shape=jax.ShapeDtypeStruct((ni, cc), F32),
        scratch_shapes=[pltpu.VMEM((cc, LANE), F32), pltpu.VMEM((cc, LANE), F32)],
        compiler_params=_cp(("parallel",)),
        name="deltanet_solve",
    )(a2)


def _dnscan_kernel(u_ref, t_ref, qk_ref, be_ref, eg_ref, ek_ref, gt_ref, o_ref, *s_refs, n_dir, n_batch):
    @pl.when(pl.program_id(0) == 0)
    def _():
        for s_ref in s_refs:
            s_ref[...] = jnp.zeros_like(s_ref)

    scale = DN_HEAD_DIM ** -0.5
    rep = DN_V_HEADS // DN_QK_HEADS
    streams = [(d, b, h) for d in range(n_dir) for b in range(n_batch) for h in range(DN_V_HEADS)]

    def qkv(d, b, h):
        hq = h // rep
        q = u_ref[d, b, :, hq * DN_HEAD_DIM:(hq + 1) * DN_HEAD_DIM]
        k = u_ref[d, b, :, DN_QK_W + hq * DN_HEAD_DIM:DN_QK_W + (hq + 1) * DN_HEAD_DIM]
        v = u_ref[d, b, :, 2 * DN_QK_W + h * DN_HEAD_DIM:2 * DN_QK_W + (h + 1) * DN_HEAD_DIM]
        return q, k, v

    uws = []
    for d, b, h in streams:
        _, k, v = qkv(d, b, h)
        be = be_ref[d, b, :, h:h + 1]
        rhs = jnp.concatenate([v * be, (k * be) * eg_ref[d, b, :, h:h + 1]], axis=1).astype(BF16)
        uws.append(_dot(t_ref[d, b, h, 0].astype(BF16), rhs))
    wqs = []
    for i, (d, b, h) in enumerate(streams):
        q, _, _ = qkv(d, b, h)
        lhs = jnp.concatenate([uws[i][:, DN_HEAD_DIM:], q * (scale * eg_ref[d, b, :, h:h + 1])], axis=0)
        wqs.append(_dot(lhs.astype(BF16), s_refs[i][...].astype(BF16)))
    for i, (d, b, h) in enumerate(streams):
        _, k, _ = qkv(d, b, h)
        vnb = (uws[i][:, :DN_HEAD_DIM] - wqs[i][:DN_CHUNK]).astype(BF16)
        o_ref[d, b, :, h * DN_HEAD_DIM:(h + 1) * DN_HEAD_DIM] = (
            wqs[i][DN_CHUNK:] + _dot(qk_ref[d, b, h, 0].astype(BF16), vnb))
        kd = (k * ek_ref[d, b, :, h:h + 1]).astype(BF16)
        s_refs[i][...] = s_refs[i][...] * gt_ref[d, b, 0:1, h:h + 1] + lax.dot_general(
            kd, vnb, (((0,), (0,)), ((), ())), preferred_element_type=F32)


def _dnscan_call(u2, t6, qk6, be, eg, ek, gt, *, s_len):
    n_dir, n_batch, r_len, _ = u2.shape
    nc = r_len // DN_CHUNK
    ncl = s_len // DN_CHUNK

    def cidx(t):
        return jnp.where(t < nc - ncl, ncl + t, t - (nc - ncl))

    kern = functools.partial(_dnscan_kernel, n_dir=n_dir, n_batch=n_batch)
    gspec = pl.BlockSpec((n_dir, n_batch, DN_CHUNK, LANE), lambda t: (0, 0, cidx(t), 0))
    mspec = pl.BlockSpec((n_dir, n_batch, DN_V_HEADS, 1, DN_CHUNK, DN_CHUNK), lambda t: (0, 0, 0, cidx(t), 0, 0))
    return pl.pallas_call(
        kern,
        grid=(nc,),
        in_specs=[pl.BlockSpec((n_dir, n_batch, DN_CHUNK, DN_W), lambda t: (0, 0, cidx(t), 0)),
                  mspec, mspec, gspec, gspec, gspec, gspec],
        out_specs=pl.BlockSpec((n_dir, n_batch, DN_CHUNK, DN_V_W), lambda t: (0, 0, cidx(t), 0)),
        out_shape=jax.ShapeDtypeStruct((n_dir, n_batch, r_len, DN_V_W), F32),
        scratch_shapes=[pltpu.VMEM((DN_HEAD_DIM, DN_HEAD_DIM), F32)] * (n_dir * n_batch * DN_V_HEADS),
        compiler_params=_cp(("arbitrary",)),
        name="deltanet_scan",
    )(u2, t6, qk6, be, eg, ek, gt)


DN_OUT_COLS = 512


def _dnout_kernel(of_ref, ob_ref, z_ref, g_ref, jm_ref, o_ref):
    for h in range(DN_OUT_COLS // DN_HEAD_DIM):
        sl = slice(h * DN_HEAD_DIM, (h + 1) * DN_HEAD_DIM)
        o = of_ref[0, 0, :, sl] + _flip_rows(jm_ref[...], ob_ref[0, 0, :, sl])
        y = o * lax.rsqrt(jnp.mean(o * o, axis=-1, keepdims=True) + NORM_EPS) * g_ref[...]
        o_ref[0, :, sl] = (y * _silu(z_ref[0, :, sl])).astype(o_ref.dtype)


def _dnout_call(o2, p3, norm_g, *, s_len):
    _, n_batch, r_len, _ = o2.shape
    tr = FLIP_ROWS
    cw = DN_OUT_COLS
    ns, nc = s_len // tr, (r_len - s_len) // tr
    assert OFF_DZ % cw == 0

    def mirror(i):
        return jnp.where(i < ns, ns - 1 - i, 2 * ns + nc - 1 - i)

    return pl.pallas_call(
        _dnout_kernel,
        grid=(n_batch, r_len // tr, DN_V_W // cw),
        in_specs=[pl.BlockSpec((1, 1, tr, cw), lambda b, i, j: (0, b, i, j)),
                  pl.BlockSpec((1, 1, tr, cw), lambda b, i, j: (1, b, mirror(i), j)),
                  pl.BlockSpec((1, tr, cw), lambda b, i, j: (b, i, OFF_DZ // cw + j)),
                  pl.BlockSpec((1, DN_HEAD_DIM), lambda b, i, j: (0, 0)),
                  pl.BlockSpec((FLIP_ROWS, FLIP_ROWS), lambda b, i, j: (0, 0))],
        out_specs=pl.BlockSpec((1, tr, cw), lambda b, i, j: (b, i, j)),
        out_shape=jax.ShapeDtypeStruct((n_batch, r_len, DN_V_W), BF16),
        compiler_params=_cp(("parallel", "parallel", "parallel")),
        name="deltanet_out",
    )(o2, o2, p3, norm_g.reshape(1, DN_HEAD_DIM), _exchange_matrix())


def _seq_flip(a, s_len, axis):
    lat, ctx = jnp.split(a, [s_len], axis=axis)
    return jnp.concatenate([jnp.flip(lat, axis), jnp.flip(ctx, axis)], axis=axis)


def _deltanet_mixer(p3, conv_w, a_log, dt_bias, norm_g, *, s_len):
    n_batch, r_len, _ = p3.shape
    u2 = _dnprep_call(p3, conv_w, s_len=s_len)
    lg =p3[:, :, OFF_DL:OFF_DL + N_LOGITS].reshape(n_batch, r_len, 2, 2, DN_V_HEADS)

    def dirs(x):
        return jnp.stack([x[:, :, 0], _seq_flip(x[:, :, 1], s_len, 1)])

    la, lb = dirs(lg[:, :, 0]), dirs(lg[:, :, 1])
    padl = lambda x: jnp.pad(x, ((0, 0), (0, 0), (0, 0), (0, LANE - DN_V_HEADS)))
    a6, qk6, be, eg, ek, gt = _dnintra_call(u2, padl(la), padl(lb), jnp.swapaxes(la, 2, 3), a_log, dt_bias)
    t6 = _dnsolve_call(a6.reshape(-1, DN_CHUNK * DN_CHUNK)).reshape(a6.shape)
    o2 = _dnscan_call(u2, t6, qk6, be, eg, ek, gt, s_len=s_len)
    return _dnout_call(o2, p3, norm_g, s_len=s_len)


def _merge_kernel(ya_ref, yb_ref, yc_ref, wa_ref, wb_ref, wc_ref, ga_ref, gb_ref, gc_ref, o_ref):
    m = (jax.nn.sigmoid(ga_ref[...]) * _dot(ya_ref[...], wa_ref[...])
         + jax.nn.sigmoid(gb_ref[...]) * _dot(yb_ref[...], wb_ref[...])
         + jax.nn.sigmoid(gc_ref[...]) * _dot(yc_ref[...], wc_ref[...]))
    o_ref[...] = m.astype(o_ref.dtype)


def _merge_call(ya, yb, yc, wa, wb, wc, p2, *, r_len):
    rt = ya.shape[0]
    d = wa.shape[1]
    tm = _pick(r_len, 1088, 16)
    tn = _pick(d, MERGE_ALIGN)
    assert OFF_MG % tn == 0
    yspec = lambda w: pl.BlockSpec((tm, w), lambda i, j: (i, 0))
    wspec = lambda w: pl.BlockSpec((w, tn), lambda i, j: (0, j))
    gspec = lambda br: pl.BlockSpec((tm, tn), lambda i, j, br=br: (i, (OFF_MG + br * d) // tn + j))
    return pl.pallas_call(
        _merge_kernel,
        grid=(rt // tm, d // tn),
        in_specs=[yspec(ATT_W), yspec(HY_W), yspec(DN_V_W), wspec(ATT_W), wspec(HY_W), wspec(DN_V_W),
                  gspec(0), gspec(1), gspec(2)],
        out_specs=pl.BlockSpec((tm, tn), lambda i, j: (i, j)),
        out_shape=jax.ShapeDtypeStruct((rt, d), BF16),
        compiler_params=_cp(("parallel", "parallel")),
        name="branch_merge",
    )(ya, yb, yc, wa, wb, wc, p2, p2, p2)


def _outproj_kernel(m_ref, w_ref, x_ref, gate_ref, o_ref, *, tm, tiles_per_batch, s_len, n_batch):
    is_ctx, gl, gc = _row_mods(gate_ref, pl.program_id(0), 0, tm, tm, tiles_per_batch, s_len, n_batch)
    o_ref[...] = x_ref[...] + jnp.where(is_ctx, gc, gl) * _dot(m_ref[...], w_ref[...])


def _outproj_call(m, w_out, xs2, mods, *, n_batch, r_len, s_len):
    rt, d = xs2.shape
    tm = _pick(r_len, 1088, 16)
    tn = _pick(d, 512)
    kern = functools.partial(_outproj_kernel, tm=tm, tiles_per_batch=r_len // tm, s_len=s_len, n_batch=n_batch)
    return pl.pallas_call(
        kern,
        grid=(rt // tm, d // tn),
        in_specs=[pl.BlockSpec((tm, d), lambda i, j: (i, 0)),
                  pl.BlockSpec((d, tn), lambda i, j: (0, j)),
                  pl.BlockSpec((tm, tn), lambda i, j: (i, j)),
                  pl.BlockSpec((8, tn), lambda i, j: (0, 2 * d // tn + j))],
        out_specs=pl.BlockSpec((tm, tn), lambda i, j: (i, j)),
        out_shape=jax.ShapeDtypeStruct((rt, d), F32),
        compiler_params=_cp(("parallel", "parallel")),
        name="out_proj_residual",
    )(m, w_out, xs2, mods)


def _finalnorm_kernel(x_ref, g_ref, o_ref):
    x = x_ref[0]
    o_ref[0] = x * lax.rsqrt(jnp.mean(x * x, axis=-1, keepdims=True) + NORM_EPS) * g_ref[...]


def _finalnorm_call(xs, final_g, *, s_len):
    n_batch, _, d = xs.shape
    tr = _pick(s_len, 512, 8)
    return pl.pallas_call(
        _finalnorm_kernel,
        grid=(n_batch, s_len // tr),
        in_specs=[pl.BlockSpec((1, tr, d), lambda b, i: (b, i, 0)), pl.BlockSpec((1, d), lambda b, i: (0, 0))],
        out_specs=pl.BlockSpec((1, tr, d), lambda b, i: (b, i, 0)),
        out_shape=jax.ShapeDtypeStruct((n_batch, s_len, d), F32),
        compiler_params=_cp(("parallel", "parallel")),
        name="final_norm",
    )(xs, final_g.reshape(1, d))


def _pad_w_in(w_in_l):
    d = w_in_l.shape[0]
    pad = jnp.zeros((d, OFF_MG - (OFF_DL + N_LOGITS)), w_in_l.dtype)
    return jnp.concatenate([w_in_l[:, :OFF_DL + N_LOGITS], pad, w_in_l[:, OFF_DL + N_LOGITS:]], axis=1).astype(BF16)


def kernel(x, c, ctx, c_ctx, norm_g, w_mod, b_mod, w_in, q_norm_g, k_norm_g, hy_conv_w, hy_conv_b, hy_w1, hy_b1, hy_freq1, hy_w2, hy_b2, hy_freq2, hy_w3, hy_d, dn_conv_w, dn_a_log, dn_dt_bias, dn_norm_g, w_pa, w_pb, w_pc, w_out, final_g):
    n_batch, s_len, d = x.shape
    ctx_len = ctx.shape[1]
    r_len = s_len + ctx_len
    depth = w_in.shape[0]
    assert n_batch + 1 <= 8 and w_in.shape[2] == OFF_DL + N_LOGITS + N_BRANCH * d

    xs = jnp.concatenate([x, ctx], axis=1)
    cs = jnp.zeros((8, d), F32).at[:n_batch].set(c).at[n_batch].set(c_ctx)
    mods = _mod_call(cs, w_mod, b_mod)
    cos_t, sin_t = _rope_tables(s_len, ctx_len)
    w1p = jnp.pad(hy_w1, ((0, 0), (0, LANE - HY_EMB), (0, 0)))

    for layer in range(depth):
        need_ctx = layer < depth - 1
        xs2 = xs.reshape(n_batch * r_len, d)
        p2 = _inproj_call(xs2, norm_g[layer], mods[layer], _pad_w_in(w_in[layer]),
                          n_batch=n_batch, r_len=r_len, s_len=s_len)
        p3 = p2.reshape(n_batch, r_len, -1)

        ya = _attn_call(p3, cos_t, sin_t, q_norm_g[layer], k_norm_g[layer], s_len=s_len)

        filt = (w1p[layer], hy_b1[layer], hy_freq1[layer], hy_w2[layer], hy_b2[layer], hy_freq2[layer],
                hy_w3[layer])
        yb = _hyena_mixer(p3, 0, s_len, hy_conv_w[layer], hy_conv_b[layer], hy_d[layer], filt)
        if need_ctx:
            yb_c = _hyena_mixer(p3, s_len // ctx_len, ctx_len, hy_conv_w[layer], hy_conv_b[layer],
                                hy_d[layer], filt)
        else:
            yb_c = jnp.zeros((n_batch, ctx_len, HY_W), BF16)
        yb = jnp.concatenate([yb, yb_c], axis=1)

        yc = _deltanet_mixer(p3, dn_conv_w[layer], dn_a_log[layer], dn_dt_bias[layer], dn_norm_g[layer],
                             s_len=s_len)

        rt = n_batch * r_len
        m = _merge_call(ya.reshape(rt, ATT_W), yb.reshape(rt, HY_W), yc.reshape(rt, DN_V_W),
                        w_pa[layer].astype(BF16), w_pb[layer].astype(BF16), w_pc[layer].astype(BF16),
                        p2, r_len=r_len)
        xs = _outproj_call(m, w_out[layer].astype(BF16), xs2, mods[layer],
                           n_batch=n_batch, r_len=r_len, s_len=s_len).reshape(n_batch, r_len, d)

    return _finalnorm_call(xs, final_g, s_len=s_len)
```

```python
import functools
import math

import jax
import jax.numpy as jnp
import numpy as np
from jax import lax
from jax.experimental import pallas as pl
from jax.experimental.pallas import tpu as pltpu

F32 = jnp.float32
BF16 = jnp.bfloat16

GRID_W = 64
NORM_EPS = 1e-6
N_BRANCH = 3

ATT_HEADS = 8
ATT_KV_HEADS = 2
HEAD_DIM = 128
ATT_GROUP = ATT_HEADS // ATT_KV_HEADS
ATT_W = ATT_HEADS * HEAD_DIM
ATT_KV_W = ATT_KV_HEADS * HEAD_DIM
AXIS_ROPE_DIM = HEAD_DIM // 2
ROPE_THETA = 10000.0

HY_W = 1024
HY_ORDER = 2
HY_BANDS = 16
HY_EMB = 1 + 2 * HY_BANDS
HY_FILTER_HIDDEN = 64
HY_DECAY_TARGET = 1e-2
HY_DECAY_FAST = 0.3
HY_DECAY_SLOW = 1.5

DN_QK_HEADS = 4
DN_V_HEADS = 8
DN_HEAD_DIM = 128
DN_QK_W = DN_QK_HEADS * DN_HEAD_DIM
DN_V_W = DN_V_HEADS * DN_HEAD_DIM
DN_CHUNK = 64
DN_W = 2 * DN_QK_W + DN_V_W

LANE = 128
MIB = 1024 * 1024

OFF_AQ = 0
OFF_AK = OFF_AQ + ATT_W
OFF_AV = OFF_AK + ATT_KV_W
OFF_AG = OFF_AV + ATT_KV_W
OFF_HV = OFF_AG + ATT_W
OFF_HX1 = OFF_HV + HY_W
OFF_HX2 = OFF_HX1 + HY_W
OFF_HG = OFF_HX2 + HY_W
OFF_DQ = OFF_HG + HY_W
OFF_DV = OFF_DQ + 2 * DN_QK_W
OFF_DZ = OFF_DV + DN_V_W
OFF_DL = OFF_DZ + DN_V_W
N_LOGITS = 4 * DN_V_HEADS
MERGE_ALIGN = 512
OFF_MG = -(-(OFF_DL + N_LOGITS) // MERGE_ALIGN) * MERGE_ALIGN


def _pick(n, cap, mult=LANE):
    best = None
    for t in range(mult, min(n, cap) + 1, mult):
        if n % t == 0:
            best = t
    assert best is not None, (n, cap, mult)
    return best


def _cp(sem, vmem_mib=48):
    return pltpu.CompilerParams(dimension_semantics=sem, vmem_limit_bytes=vmem_mib * MIB)


def _silu(x):
    return x * jax.nn.sigmoid(x)


def _split3(x):
    x1 = x.astype(BF16)
    r = x - x1.astype(F32)
    x2 = r.astype(BF16)
    x3 = (r - x2.astype(F32)).astype(BF16)
    return x1, x2, x3


def _dot(a, b):
    return jnp.dot(a, b, preferred_element_type=F32)


def _dot_hp(a, b):
    a1 = a.astype(BF16)
    a2 = (a - a1.astype(F32)).astype(BF16)
    b1 = b.astype(BF16)
    b2 = (b - b1.astype(F32)).astype(BF16)
    return _dot(a1, b1) + (_dot(a1, b2) + _dot(a2, b1))


def _mod_kernel(cs_ref, w_ref, b_ref, o_ref):
    cs = cs_ref[...]
    o_ref[0] = _dot(_silu(cs).astype(BF16), w_ref[0].astype(BF16)) + b_ref[0]


def _mod_call(cs, w_mod, b_mod):
    depth, d, d3 = w_mod.shape
    tn = _pick(d3, 512)
    return pl.pallas_call(
        _mod_kernel,
        grid=(depth, d3 // tn),
        in_specs=[pl.BlockSpec((8, d), lambda l, j: (0, 0)),
                  pl.BlockSpec((1, d, tn), lambda l, j: (l, 0, j)),
                  pl.BlockSpec((1, 1, tn), lambda l, j: (l, 0, j))],
        out_specs=pl.BlockSpec((1, 8, tn), lambda l, j: (l, 0, j)),
        out_shape=jax.ShapeDtypeStruct((depth, 8, d3), F32),
        compiler_params=_cp(("parallel", "parallel")),
        name="adaln_mod",
    )(cs, w_mod, b_mod.reshape(depth, 1, d3))


def _row_mods(mod_ref, i, r0, rows, tm, tiles_per_batch, s_len, n_batch):
    b = i // tiles_per_batch
    row = (i % tiles_per_batch) * tm + r0 + lax.broadcasted_iota(jnp.int32, (rows, 1), 0)
    return row >= s_len, mod_ref[pl.ds(b, 1), :], mod_ref[n_batch:n_batch + 1, :]


NORM_ROWS = 64


def _inproj_kernel(x_ref, g_ref, mod_ref, wm_ref, wt_ref, o_ref, h_ref,
                   *, tm, tiles_per_batch, s_len, n_batch, d, n_main):
    i = pl.program_id(0)
    j = pl.program_id(1)

    @pl.when(j == 0)
    def _():
        def body(c, carry):
            r0 = pl.multiple_of(c * NORM_ROWS, NORM_ROWS)
            x = x_ref[pl.ds(r0, NORM_ROWS), :]
            y = x * lax.rsqrt(jnp.mean(x * x, axis=-1, keepdims=True) + NORM_EPS) * g_ref[...]
            is_ctx, ml, mc = _row_mods(mod_ref, i, r0, NORM_ROWS, tm, tiles_per_batch, s_len, n_batch)
            shift = jnp.where(is_ctx, mc[:, :d], ml[:, :d])
            scale = jnp.where(is_ctx, mc[:, d:2 * d], ml[:, d:2 * d])
            h_ref[pl.ds(r0, NORM_ROWS), :] = (y * (1.0 + scale) + shift).astype(BF16)
            return carry
        lax.fori_loop(0, tm // NORM_ROWS, body, 0)

    @pl.when(j < n_main)
    def _():
        o_ref[...] = _dot(h_ref[...], wm_ref[...].astype(BF16))

    @pl.when(j >= n_main)
    def _():
        o_ref[...] = _dot(h_ref[...], wt_ref[...])


def _inproj_tiles(d):
    for tn in (1024, 512, 256, 128):
        main = OFF_DL // tn * tn
        if (OFF_MG - main + N_BRANCH * d) % tn == 0:
            return tn, main // tn
    raise ValueError(d)


def _inproj_tail(w_in_l):
    d = w_in_l.shape[0]
    tn, n_main = _inproj_tiles(d)
    cut = OFF_DL + N_LOGITS
    pad = jnp.zeros((d, OFF_MG - cut), w_in_l.dtype)
    return jnp.concatenate([w_in_l[:, n_main * tn:cut], pad, w_in_l[:, cut:]], axis=1).astype(BF16)


def _inproj_call(xs2, norm_g, mods, w_in_l, w_tail, *, n_batch, r_len, s_len):
    rt, d = xs2.shape
    tn, n_main = _inproj_tiles(d)
    nw = n_main * tn + w_tail.shape[1]
    assert nw == OFF_MG + N_BRANCH * d
    tm = _pick(r_len, 1088, NORM_ROWS)
    kern = functools.partial(_inproj_kernel, tm=tm, tiles_per_batch=r_len // tm, s_len=s_len,
                             n_batch=n_batch, d=d, n_main=n_main)
    return pl.pallas_call(
        kern,
        grid=(rt // tm, nw // tn),
        in_specs=[pl.BlockSpec((tm, d), lambda i, j: (i, 0), pipeline_mode=pl.Buffered(1)),
                  pl.BlockSpec((1, d), lambda i, j: (0, 0)),
                  pl.BlockSpec((8, 3 * d), lambda i, j: (0, 0)),
                  pl.BlockSpec((d, tn), lambda i, j: (0, jnp.minimum(j, n_main - 1))),
                  pl.BlockSpec((d, tn), lambda i, j: (0, jnp.maximum(j - n_main, 0)))],
        out_specs=pl.BlockSpec((tm, tn), lambda i, j: (i, j)),
        out_shape=jax.ShapeDtypeStruct((rt, nw), F32),
        scratch_shapes=[pltpu.VMEM((tm, d), BF16)],
        compiler_params=_cp(("parallel", "arbitrary"), 56),
        name="norm_inproj",
    )(xs2, norm_g.reshape(1, d), mods, w_in_l, w_tail)


def _rope_tables(s_len, ctx_len):
    m = AXIS_ROPE_DIM // 2
    inv_freq = ROPE_THETA ** (-np.arange(0, AXIS_ROPE_DIM, 2, dtype=np.float64) / AXIS_ROPE_DIM)
    t = np.arange(s_len)
    ang_r = (t // GRID_W)[:, None] * inv_freq
    ang_c = (t % GRID_W)[:, None] * inv_freq
    cos = np.concatenate([np.cos(ang_r)] * 2 + [np.cos(ang_c)] * 2, axis=-1)
    sin = np.concatenate([-np.sin(ang_r), np.sin(ang_r), -np.sin(ang_c), np.sin(ang_c)], axis=-1)
    assert cos.shape[1] == 4 * m == HEAD_DIM
    cos = np.concatenate([cos, np.ones((ctx_len, HEAD_DIM))], axis=0)
    sin = np.concatenate([sin, np.zeros((ctx_len, HEAD_DIM))], axis=0)
    return jnp.asarray(cos, F32), jnp.asarray(sin, F32)


def _norm_rope(x, g, cs, sn):
    y = x * lax.rsqrt(jnp.mean(x * x, axis=-1, keepdims=True) + NORM_EPS) * g
    lane = lax.broadcasted_iota(jnp.int32, (1, HEAD_DIM), 1)
    first = (lane % AXIS_ROPE_DIM) < (AXIS_ROPE_DIM // 2)
    q = AXIS_ROPE_DIM // 2
    partner = jnp.where(first, pltpu.roll(y, HEAD_DIM - q, 1), pltpu.roll(y, q, 1))
    return y * cs + partner * sn


def _attn_kernel(q_ref, k_ref, v_ref, gt_ref, cos_ref, sin_ref, qg_ref, kg_ref, o_ref, ks_ref, vs_ref,
                 *, tq, r_len, s_len, kv_chunk):
    i = pl.program_id(2)

    @pl.when(i == 0)
    def _():
        def body(c, carry):
            r0 = pl.multiple_of(c * kv_chunk, kv_chunk)
            kk = _norm_rope(k_ref[0, pl.ds(r0, kv_chunk), :], kg_ref[...],
                            cos_ref[pl.ds(r0, kv_chunk), :], sin_ref[pl.ds(r0, kv_chunk), :])
            ks_ref[pl.ds(r0, kv_chunk), :] = kk.astype(BF16)
            vs_ref[pl.ds(r0, kv_chunk), :] = v_ref[0, pl.ds(r0, kv_chunk), :].astype(BF16)
            return carry
        lax.fori_loop(0, r_len // kv_chunk, body, 0)

    r0 = pl.multiple_of(i * tq, tq)
    cs = cos_ref[pl.ds(r0, tq), :]
    sn = sin_ref[pl.ds(r0, tq), :]
    scale = HEAD_DIM ** -0.5

    def heads(k_lo, k_hi):
        for g in range(ATT_GROUP):
            sl = slice(g * HEAD_DIM, (g + 1) * HEAD_DIM)
            qh = (_norm_rope(q_ref[0, :, sl], qg_ref[...], cs, sn) * scale).astype(BF16)
            s = lax.dot_general(qh, ks_ref[k_lo:k_hi, :], (((1,), (1,)), ((), ())),
                                preferred_element_type=F32)
            e = jnp.exp(s - jnp.max(s, axis=-1, keepdims=True))
            den = jnp.sum(e, axis=-1, keepdims=True)
            o = _dot(e.astype(BF16), vs_ref[k_lo:k_hi, :]) / den
            o_ref[0, :, sl] = (o * _silu(gt_ref[0, :, sl])).astype(o_ref.dtype)

    @pl.when(i < s_len // tq)
    def _():
        heads(0, r_len)

    @pl.when(i >= s_len // tq)
    def _():
        heads(s_len, r_len)


def _attn_call(p3, cos_t, sin_t, q_g, k_g, *, s_len):
    n_batch, r_len, _ = p3.shape
    ctx_len = r_len - s_len
    tq = 256 if (ctx_len % 256 == 0 and s_len % 256 == 0) else 128
    assert ctx_len % tq == 0 and s_len % tq == 0
    gw = ATT_GROUP * HEAD_DIM
    kern = functools.partial(_attn_kernel, tq=tq, r_len=r_len, s_len=s_len, kv_chunk=tq)
    return pl.pallas_call(
        kern,
        grid=(n_batch, ATT_KV_HEADS, r_len // tq),
        in_specs=[pl.BlockSpec((1, tq, gw), lambda b, h, i: (b, i, OFF_AQ // gw + h)),
                  pl.BlockSpec((1, r_len, HEAD_DIM), lambda b, h, i: (b, 0, OFF_AK // HEAD_DIM + h)),
                  pl.BlockSpec((1, r_len, HEAD_DIM), lambda b, h, i: (b, 0, OFF_AV // HEAD_DIM + h)),
                  pl.BlockSpec((1, tq, gw), lambda b, h, i: (b, i, OFF_AG // gw + h)),
                  pl.BlockSpec((r_len, HEAD_DIM), lambda b, h, i: (0, 0)),
                  pl.BlockSpec((r_len, HEAD_DIM), lambda b, h, i: (0, 0)),
                  pl.BlockSpec((1, HEAD_DIM), lambda b, h, i: (0, 0)),
                  pl.BlockSpec((1, HEAD_DIM), lambda b, h, i: (0, 0))],
        out_specs=pl.BlockSpec((1, tq, gw), lambda b, h, i: (b, i, h)),
        out_shape=jax.ShapeDtypeStruct((n_batch, r_len, ATT_W), BF16),
        scratch_shapes=[pltpu.VMEM((r_len, HEAD_DIM), BF16), pltpu.VMEM((r_len, HEAD_DIM), BF16)],
        compiler_params=_cp(("parallel", "parallel", "arbitrary")),
        name="gqa_attention",
    )(p3, p3, p3, p3, cos_t, sin_t, q_g.reshape(1, HEAD_DIM), k_g.reshape(1, HEAD_DIM))


def _conv3(x, w, first, last):
    n = x.shape[0]
    prev = jnp.where(first, 0.0, pltpu.roll(x, 1, 0))
    nxt = jnp.where(last, 0.0, pltpu.roll(x, n - 1, 0))
    return prev * w[0:1, :] + x * w[1:2, :] + nxt * w[2:3, :]


FFT_UNROLL = 8


def _fft_split(n):
    if n <= 512:
        return 1, n
    n2 = 64
    return n // n2, n2


@functools.lru_cache(maxsize=None)
def _fft_mats(n, n1, n2):
    half = n2 // 2
    a = np.arange(n1)[:, None, None]
    k2 = np.arange(n2)[None, :, None]

    def g(bs):
        ph = (k2 * (a + n1 * bs[None, None, :])) % n
        ang = -2.0 * np.pi * ph / n
        return np.cos(ang), np.sin(ang)

    gre, gim = g(np.arange(half))
    gal = np.concatenate([gre, gim], axis=1)
    gar = np.concatenate([-gim, gre], axis=1)
    hre = np.swapaxes(gre, 1, 2) / n
    him = -np.swapaxes(gim, 1, 2) / n
    hal = np.concatenate([hre, him], axis=1)
    har = np.concatenate([-him, hre], axis=1)
    fre_f, fim_f = g(np.arange(n2))
    gf = np.concatenate([fre_f, fim_f], axis=1)
    k1 = np.arange(n1)
    ang1 = -2.0 * np.pi * ((k1[:, None] * k1[None, :]) % n1) / n1
    f1re, f1im = np.cos(ang1), np.sin(ang1)
    fbl = np.concatenate([f1re, f1im], axis=0)
    fbr = np.concatenate([-f1im, f1re], axis=0)
    fbil = np.concatenate([f1re, -f1im], axis=0)
    fbir = np.concatenate([f1im, f1re], axis=0)
    cast = lambda m: jnp.asarray(m, BF16)
    return dict(ga=cast(np.concatenate([gal, gar], axis=2)), ha=cast(np.concatenate([hal, har], axis=2)),
                gf=cast(gf), fb=cast(np.concatenate([fbl, fbr], axis=1)),
                fbi=cast(np.concatenate([fbil, fbir], axis=1)))


@functools.lru_cache(maxsize=None)
def _hy_tables(n_tok):
    pos = np.arange(n_tok, dtype=np.float64)
    t = pos / max(n_tok - 1, 1)
    bands = np.linspace(1e-4, HY_BANDS - 1, HY_BANDS)
    ang = (2.0 * np.pi / n_tok) * pos[:, None] * bands
    z = np.concatenate([t[:, None], np.cos(ang), np.sin(ang)], axis=-1)
    zrev = np.zeros_like(z)
    zrev[1:] = z[:0:-1]
    ztab = np.zeros((2 * n_tok, LANE))
    ztab[:, :HY_EMB] = np.concatenate([z, zrev], axis=0)
    deltas = np.abs(np.linspace(math.log(HY_DECAY_TARGET) / HY_DECAY_SLOW,
                                math.log(HY_DECAY_TARGET) / HY_DECAY_FAST, HY_W))
    return jnp.asarray(ztab, F32), jnp.asarray(np.tile(deltas, HY_ORDER)[None, :], F32)


SLAB_PAD = 8


def _hyfilt_kernel(z_ref, w1_ref, b1_ref, f1_ref, w2_ref, b2_ref, f2_ref, w3_ref, dl_ref, k_ref, s_ref,
                   *, n_tok, tr, slab_rows):
    i = pl.program_id(0)
    z = z_ref[...]
    h = jnp.sin(f1_ref[...] * (_dot_hp(z, w1_ref[...]) + b1_ref[...]))
    h = jnp.sin(f2_ref[...] * (_dot_hp(h, w2_ref[...]) + b2_ref[...]))
    h = _dot_hp(h, w3_ref[...]) * jnp.exp(-z[:, 0:1] * dl_ref[...])
    row = i * tr + lax.broadcasted_iota(jnp.int32, (tr, 1), 0)
    h = jnp.where(row == n_tok, 0.0, h)
    if slab_rows is None:
        k_ref[...] = h
    else:
        step = slab_rows + SLAB_PAD
        for s in range(tr // slab_rows):
            k_ref[s * step:s * step + slab_rows, :] = h[s * slab_rows:(s + 1) * slab_rows]
            k_ref[s * step + slab_rows:(s + 1) * step, :] = jnp.zeros((SLAB_PAD, h.shape[1]), F32)

    @pl.when(i == 0)
    def _():
        s_ref[...] = jnp.zeros_like(s_ref)

    s_ref[...] += jnp.sum(jnp.abs(h), axis=0, keepdims=True)


def _hyfilt_call(n_tok, w1p, b1, f1, w2, b2, f2, w3):
    ztab, dl = _hy_tables(n_tok)
    n = 2 * n_tok
    tr = _pick(n_tok, 512, 8)
    ow = HY_ORDER * HY_W
    hid = HY_FILTER_HIDDEN
    n1, _ = _fft_split(n)
    slab_rows = n1 if n1 > 1 else None
    assert slab_rows is None or tr % slab_rows == 0
    out_tr = tr if slab_rows is None else (tr // slab_rows) * (slab_rows + SLAB_PAD)
    kern = functools.partial(_hyfilt_kernel, n_tok=n_tok, tr=tr, slab_rows=slab_rows)
    c2 = lambda i: (0, 0)
    return pl.pallas_call(
        kern,
        grid=(n // tr,),
        in_specs=[pl.BlockSpec((tr, LANE), lambda i: (i, 0)),
                  pl.BlockSpec((LANE, hid), c2), pl.BlockSpec((1, hid), c2), pl.BlockSpec((1, hid), c2),
                  pl.BlockSpec((hid, hid), c2), pl.BlockSpec((1, hid), c2), pl.BlockSpec((1, hid), c2),
                  pl.BlockSpec((hid, ow), lambda i: (0, i // (n_tok // tr))),
                  pl.BlockSpec((1, ow), c2)],
        out_specs=[pl.BlockSpec((out_tr, ow), lambda i: (i, 0)), pl.BlockSpec((1, ow), c2)],
        out_shape=[jax.ShapeDtypeStruct((n // tr * out_tr, ow), F32), jax.ShapeDtypeStruct((1, ow), F32)],
        compiler_params=_cp(("arbitrary",)),
        name="hyena_filter",
    )(ztab, w1p, b1.reshape(1, hid), f1.reshape(1, hid), w2, b2.reshape(1, hid), f2.reshape(1, hid), w3, dl)


FFT_GROUP = 4


def _filtfft_kernel(k_ref, s_ref, gf_ref, fb_ref, o_ref, z_ref, *, n, n1, n2):
    inv = 1.0 / (s_ref[...] + 1e-6)
    if n1 == 1:
        o_ref[...] = _dot(gf_ref[0], k_ref[...].astype(BF16)) * inv
        return
    slab = 2 * n1
    zs, ks = slab + SLAB_PAD, n1 + SLAB_PAD

    def stage_a(a, carry):
        out = _dot(gf_ref[a], k_ref[pl.ds(a, n2, stride=ks), :].astype(BF16))
        z_ref[pl.ds(a, n2, stride=zs), :] = out[:n2]
        z_ref[pl.ds(n1 + a, n2, stride=zs), :] = out[n2:]
        return carry
    lax.fori_loop(0, n1, stage_a, 0, unroll=FFT_UNROLL)

    def stage_b(g, carry):
        for u in range(FFT_GROUP):
            k2 = g * FFT_GROUP + u
            z = z_ref[pl.ds(pl.multiple_of(k2 * zs, 8), slab), :].astype(BF16)
            o_ref[pl.ds(pl.multiple_of(k2 * slab, slab), slab), :] = _dot(fb_ref[...], z) * inv
        return carry
    lax.fori_loop(0, n2 // FFT_GROUP, stage_b, 0)


def _filtfft_call(k_un, asum, n_tok):
    k_rows, ow = k_un.shape
    n = 2 * n_tok
    n1, n2 = _fft_split(n)
    assert k_rows == (n if n1 == 1 else n2 * (n1 + SLAB_PAD))
    m = _fft_mats(n, n1, n2)
    cb = LANE
    kern = functools.partial(_filtfft_kernel, n=n, n1=n1, n2=n2)
    return pl.pallas_call(
        kern,
        grid=(ow // cb,),
        in_specs=[pl.BlockSpec((k_rows, cb), lambda j: (0, j)),
                  pl.BlockSpec((1, cb), lambda j: (0, j)),
                  pl.BlockSpec(m["gf"].shape, lambda j: (0, 0, 0)),
                  pl.BlockSpec(m["fb"].shape, lambda j: (0, 0))],
        out_specs=pl.BlockSpec((2 * n, cb), lambda j: (0, j)),
        out_shape=jax.ShapeDtypeStruct((2 * n, ow), F32),
        scratch_shapes=[pltpu.VMEM((n2 * (2 * n1 + SLAB_PAD) if n1 > 1 else 8, cb), F32)],
        compiler_params=_cp(("parallel",)),
        name="hyena_filter_fft",
    )(k_un, asum, m["gf"], m["fb"])


def _hyena_kernel(*refs, n_tok, n1, n2, conv_a, has_gate):
    it = iter(refs)
    a_ref, m_ref = next(it), next(it)
    g_ref = next(it) if has_gate else None
    if conv_a:
        cwa_ref, cba_ref = next(it), next(it)
    cwm_ref, cbm_ref, d_ref, ks_ref = next(it), next(it), next(it), next(it)
    ga_ref, ha_ref = next(it), next(it)
    if n1 > 1:
        fb_ref, fbi_ref = next(it), next(it)
    o_ref, ac_ref, z_ref = next(it), next(it), next(it)

    n = 2 * n_tok
    half = n2 // 2
    row = lax.broadcasted_iota(jnp.int32, (n_tok, 1), 0)
    first, last = row == 0, row == n_tok - 1
    sr = n1 if n1 > 1 else n_tok
    astep = sr + SLAB_PAD if n1 > 1 else sr
    for b in range(2):
        a = a_ref[b]
        if conv_a:
            a = _conv3(a, cwa_ref[...], first, last) + cba_ref[...]
        for s in range(n_tok // sr):
            ac_ref[b, s * astep:s * astep + sr, :] = a[s * sr:(s + 1) * sr]

    def spectrum_mul(x, k, h):
        xre, xim, kre, kim = x[:h], x[h:], k[:h], k[h:]
        return jnp.concatenate([xre * kre - xim * kim, xre * kim + xim * kre], axis=0).astype(BF16)

    if n1 == 1:
        x = _dot(ga_ref[0], jnp.concatenate([ac_ref[0], ac_ref[1]], axis=0).astype(BF16))
        y = _dot(ha_ref[0], spectrum_mul(x, ks_ref[...], n))
        for b in range(2):
            ac_ref[b] = y[b * n_tok:(b + 1) * n_tok] + ac_ref[b] * d_ref[0]
    else:
        slab = 2 * n1
        zs = slab + SLAB_PAD

        def stage_a(a, carry):
            rows = jnp.concatenate([ac_ref[0, pl.ds(a, half, stride=astep), :],
                                    ac_ref[1, pl.ds(a, half, stride=astep), :]], axis=0).astype(BF16)
            out = _dot(ga_ref[a], rows)
            z_ref[pl.ds(a, n2, stride=zs), :] = out[:n2]
            z_ref[pl.ds(n1 + a, n2, stride=zs), :] = out[n2:]
            return carry
        lax.fori_loop(0, n1, stage_a, 0, unroll=FFT_UNROLL)

        def stage_b(g, carry):
            k2s = [g * FFT_GROUP + u for u in range(FFT_GROUP)]
            zrows = [pl.ds(pl.multiple_of(k2 * zs, 8), slab) for k2 in k2s]
            xs = [_dot(fb_ref[...], z_ref[zr, :].astype(BF16)) for zr in zrows]
            ys = [spectrum_mul(x, ks_ref[pl.ds(pl.multiple_of(k2 * slab, slab), slab), :], n1)
                  for x, k2 in zip(xs, k2s)]
            for y, zr in zip(ys, zrows):
                z_ref[zr, :] = _dot(fbi_ref[...], y)
            return carry
        lax.fori_loop(0, n2 // FFT_GROUP, stage_b, 0)

        def stage_a_inv(a, carry):
            rows = jnp.concatenate([z_ref[pl.ds(a, n2, stride=zs), :],
                                    z_ref[pl.ds(n1 + a, n2, stride=zs), :]], axis=0).astype(BF16)
            y = _dot(ha_ref[a], rows)
            for b in range(2):
                cur = ac_ref[b, pl.ds(a, half, stride=astep), :]
                ac_ref[b, pl.ds(a, half, stride=astep), :] = y[b * half:(b + 1) * half] + cur * d_ref[0]
            return carry
        lax.fori_loop(0, n1, stage_a_inv, 0, unroll=FFT_UNROLL)

    for b in range(2):
        mc = _conv3(m_ref[b], cwm_ref[...], first, last) + cbm_ref[...]
        for s in range(n_tok // sr):
            rows = slice(s * sr, (s + 1) * sr)
            out = mc[rows] * ac_ref[b, s * astep:s * astep + sr, :]
            if has_gate:
                out = out * _silu(g_ref[b, rows, :])
            o_ref[b, rows, :] = out.astype(o_ref.dtype)


def _hyena_call(a_arr, a_blk, m_arr, m_blk, gate, conv_a, conv_m, d_row, spec, spec_blk, *, n_tok, out_dtype):
    n_batch = a_arr.shape[0]
    assert n_batch % 2 == 0
    n = 2 * n_tok
    n1, n2 = _fft_split(n)
    m = _fft_mats(n, n1, n2)
    cb = LANE
    one = pl.Buffered(1)

    def tile(blk):
        return pl.BlockSpec((2, n_tok, cb), lambda p, j, blk=blk: (p, blk[0], blk[1] + j), pipeline_mode=one)

    def convspec(c0):
        return [pl.BlockSpec((3, cb), lambda p, j, c0=c0: (0, c0 + j)),
                pl.BlockSpec((1, cb), lambda p, j, c0=c0: (0, c0 + j))]

    ins, specs = [a_arr, m_arr], [tile(a_blk), tile(m_blk)]
    if gate is not None:
        ins.append(gate[0])
        specs.append(tile(gate[1]))
    if conv_a is not None:
        ins += [conv_a[0], conv_a[1]]
        specs += convspec(conv_a[2])
    ins += [conv_m[0], conv_m[1], d_row, spec]
    specs += convspec(conv_m[2])
    specs += [pl.BlockSpec((1, 1, cb), lambda p, j: (0, 0, j)),
              pl.BlockSpec((2 * n, cb), lambda p, j, s0=spec_blk: (0, s0 + j), pipeline_mode=one)]
    names = ["ga", "ha"] + (["fb", "fbi"] if n1 > 1 else [])
    for nm in names:
        ins.append(m[nm])
        specs.append(pl.BlockSpec(m[nm].shape, lambda p, j, nd=m[nm].ndim: (0,) * nd, pipeline_mode=one))
    kern = functools.partial(_hyena_kernel, n_tok=n_tok, n1=n1, n2=n2, conv_a=conv_a is not None,
                             has_gate=gate is not None)
    return pl.pallas_call(
        kern,
        grid=(n_batch // 2, HY_W // cb),
        in_specs=specs,
        out_specs=pl.BlockSpec((2, n_tok, cb), lambda p, j: (p, 0, j)),
        out_shape=jax.ShapeDtypeStruct((n_batch, n_tok, HY_W), out_dtype),
        scratch_shapes=[pltpu.VMEM((2, n_tok // n1 * (n1 + SLAB_PAD) if n1 > 1 else n_tok, cb), F32),
                        pltpu.VMEM((n2 * (2 * n1 + SLAB_PAD) if n1 > 1 else 8, cb), F32)],
        compiler_params=_cp(("parallel", "arbitrary"), 56),
        name="hyena_conv",
    )(*ins)


def _hyena_mixer(p3, row_blk, n_tok, conv_w, conv_b, hy_d, filt):
    k_un, asum = _hyfilt_call(n_tok, *filt)
    spec = _filtfft_call(k_un, asum, n_tok)
    cbias = conv_b.reshape(1, -1)
    cblk = HY_W // LANE
    d3 = hy_d.reshape(HY_ORDER, 1, HY_W)
    z = _hyena_call(p3, (row_blk, OFF_HV // LANE), p3, (row_blk, OFF_HX1 // LANE), None,
                    (conv_w, cbias, 0), (conv_w, cbias, cblk), d3[0:1], spec, 0,
                    n_tok=n_tok, out_dtype=F32)
    return _hyena_call(z, (0, 0), p3, (row_blk, OFF_HX2 // LANE), (p3, (row_blk, OFF_HG // LANE)),
                       None, (conv_w, cbias, 2 * cblk), d3[1:2], spec, cblk,
                       n_tok=n_tok, out_dtype=BF16)


def _softplus(x):
    return jnp.maximum(x, 0.0) + jnp.log1p(jnp.exp(-jnp.abs(x)))


FLIP_ROWS = 256


def _exchange_matrix():
    return jnp.asarray(np.eye(FLIP_ROWS)[::-1], BF16)


def _flip_rows(jm, x):
    x1, x2, x3 = _split3(x)
    return _dot(jm, x1) + (_dot(jm, x2) + _dot(jm, x3))


def _dnprep_kernel(x_ref, w_ref, jm_ref, o_ref, *, r_len, s_len):
    j = pl.program_id(1)
    row = lax.broadcasted_iota(jnp.int32, (r_len, 1), 0)
    first = (row == 0) | (row == s_len)
    last = (row == s_len - 1) | (row == r_len - 1)
    u = _silu(_conv3(x_ref[0], w_ref[...], first, last))
    nrm = u * lax.rsqrt(jnp.sum(u * u, axis=-1, keepdims=True) + 1e-6)
    o_ref[0, 0] = jnp.where(j < 2 * DN_QK_HEADS, nrm, u)
    for seg0, seg_len in ((0, s_len), (s_len, r_len - s_len)):
        nt = seg_len // FLIP_ROWS
        for t in range(nt):
            src = seg0 + t * FLIP_ROWS
            dst = seg0 + (nt - 1 - t) * FLIP_ROWS
            o_ref[1, 0, dst:dst + FLIP_ROWS, :] = _flip_rows(jm_ref[...], o_ref[0, 0, src:src + FLIP_ROWS, :])


def _dnprep_call(p3, conv_w, *, s_len):
    n_batch, r_len, _ = p3.shape
    assert s_len % FLIP_ROWS == 0 and (r_len - s_len) % FLIP_ROWS == 0
    kern = functools.partial(_dnprep_kernel, r_len=r_len, s_len=s_len)
    return pl.pallas_call(
        kern,
        grid=(n_batch, DN_W // LANE),
        in_specs=[pl.BlockSpec((1, r_len, LANE), lambda b, j: (b, 0, OFF_DQ // LANE + j)),
                  pl.BlockSpec((3, LANE), lambda b, j: (0, j)),
                  pl.BlockSpec((FLIP_ROWS, FLIP_ROWS), lambda b, j: (0, 0))],
        out_specs=pl.BlockSpec((2, 1, r_len, LANE), lambda b, j: (0, b, 0, j)),
        out_shape=jax.ShapeDtypeStruct((2, n_batch, r_len, DN_W), F32),
        compiler_params=_cp(("parallel", "parallel")),
        name="deltanet_prep",
    )(p3, conv_w, _exchange_matrix())


def _tri_mats(tr):
    idx = np.arange(tr)
    same = (idx[:, None] // DN_CHUNK) == (idx[None, :] // DN_CHUNK)
    low = same & (idx[:, None] >= idx[None, :])
    return jnp.asarray(low, BF16), jnp.asarray(low.T, BF16), jnp.asarray(same, BF16)


def _dnintra_kernel(u_ref, la_ref, lb_ref, lat_ref, alr_ref, dtr_ref, alc_ref, dtc_ref, low_ref, upp_ref,
                    one_ref, a_ref, qk_ref, be_ref, eg_ref, ek_ref, gt_ref, *, tr):
    g_col = -jnp.exp(alr_ref[0]) * _softplus(la_ref[0, 0] + dtr_ref[0])
    beta = jax.nn.sigmoid(lb_ref[0, 0])
    g1, g2, g3 = _split3(g_col)
    gc_col = _dot(low_ref[...], g1) + (_dot(low_ref[...], g2) + _dot(low_ref[...], g3))
    gt_col = _dot(one_ref[...], g1) + (_dot(one_ref[...], g2) + _dot(one_ref[...], g3))
    be_ref[0, 0] = beta
    eg_ref[0, 0] = jnp.exp(gc_col)
    ek_ref[0, 0] = jnp.exp(gt_col - gc_col)
    gt_ref[0, 0] = jnp.exp(gt_col)
    g_row = -jnp.exp(alc_ref[0]) * _softplus(lat_ref[0, 0] + dtc_ref[0])
    r1, r2, r3 = _split3(g_row)
    gc_row = _dot(r1, upp_ref[...]) + (_dot(r2, upp_ref[...]) + _dot(r3, upp_ref[...]))

    ii = lax.broadcasted_iota(jnp.int32, (DN_CHUNK, DN_CHUNK), 0)
    jj = lax.broadcasted_iota(jnp.int32, (DN_CHUNK, DN_CHUNK), 1)
    scale = DN_HEAD_DIM ** -0.5
    nt = (((1,), (1,)), ((), ()))
    for c in range(tr // DN_CHUNK):
        rows = slice(c * DN_CHUNK, (c + 1) * DN_CHUNK)
        for hq in range(DN_QK_HEADS):
            q = u_ref[0, 0, rows, hq * DN_HEAD_DIM:(hq + 1) * DN_HEAD_DIM].astype(BF16)
            k = u_ref[0, 0, rows, DN_QK_W + hq * DN_HEAD_DIM:DN_QK_W + (hq + 1) * DN_HEAD_DIM].astype(BF16)
            kk = lax.dot_general(k, k, nt, preferred_element_type=F32)
            qk = lax.dot_general(q, k, nt, preferred_element_type=F32) * scale
            for h in range(hq * (DN_V_HEADS // DN_QK_HEADS), (hq + 1) * (DN_V_HEADS // DN_QK_HEADS)):
                diff = gc_col[rows, h:h + 1] - gc_row[h:h + 1, rows]
                dec = jnp.where(ii >= jj, jnp.exp(jnp.minimum(diff, 0.0)), 0.0)
                a_ref[0, 0, h, c] = jnp.where(ii > jj, kk * beta[rows, h:h + 1] * dec, 0.0)
                qk_ref[0, 0, h, c] = qk * dec


def _dnintra_call(u2, la, lb, lat, alog, dtb):
    n_dir, n_batch, r_len, _ = u2.shape
    tr = 4 * DN_CHUNK
    nc = r_len // DN_CHUNK
    low, upp, one = _tri_mats(tr)
    kern = functools.partial(_dnintra_kernel, tr=tr)
    pad = lambda v: jnp.pad(v, ((0, 0), (0, LANE - DN_V_HEADS))).reshape(n_dir, 1, LANE)
    col = lambda v: v.reshape(n_dir, DN_V_HEADS, 1)
    cm = lambda d, b, t: (0, 0)
    gspec = pl.BlockSpec((1, 1, tr, LANE), lambda d, b, t: (d, b, t, 0))
    mspec = pl.BlockSpec((1, 1, DN_V_HEADS, tr // DN_CHUNK, DN_CHUNK, DN_CHUNK), lambda d, b, t: (d, b, 0, t, 0, 0))
    gshape = jax.ShapeDtypeStruct((n_dir, n_batch, r_len, LANE), F32)
    mshape = jax.ShapeDtypeStruct((n_dir, n_batch, DN_V_HEADS, nc, DN_CHUNK, DN_CHUNK), F32)
    return pl.pallas_call(
        kern,
        grid=(n_dir, n_batch, r_len // tr),
        in_specs=[pl.BlockSpec((1, 1, tr, 2 * DN_QK_W), lambda d, b, t: (d, b, t, 0)),
                  gspec, gspec,
                  pl.BlockSpec((1, 1, DN_V_HEADS, tr), lambda d, b, t: (d, b, 0, t)),
                  pl.BlockSpec((1, 1, LANE), lambda d, b, t: (d, 0, 0)),
                  pl.BlockSpec((1, 1, LANE), lambda d, b, t: (d, 0, 0)),
                  pl.BlockSpec((1, DN_V_HEADS, 1), lambda d, b, t: (d, 0, 0)),
                  pl.BlockSpec((1, DN_V_HEADS, 1), lambda d, b, t: (d, 0, 0)),
                  pl.BlockSpec((tr, tr), cm), pl.BlockSpec((tr, tr), cm), pl.BlockSpec((tr, tr), cm)],
        out_specs=[mspec, mspec, gspec, gspec, gspec, gspec],
        out_shape=[mshape, mshape, gshape, gshape, gshape, gshape],
        compiler_params=_cp(("parallel", "parallel", "parallel")),
        name="deltanet_intra",
    )(u2, la, lb, lat, pad(alog), pad(dtb), col(alog), col(dtb), low, upp, one)


def _dnsolve_kernel(a_ref, o_ref, at_ref, tt_ref):
    c = DN_CHUNK
    for blk in range(c * c // LANE):
        at_ref[blk * LANE:(blk + 1) * LANE, :] = a_ref[:, blk * LANE:(blk + 1) * LANE].T
    tt_ref[...] = jnp.zeros_like(tt_ref)
    for i in range(c):
        nr = 8 * (i // 8 + 1)
        rr = lax.broadcasted_iota(jnp.int32, (nr, LANE), 0)
        acc = jnp.where(rr == i, 1.0, 0.0)

        def body(j, acc, i=i, nr=nr):
            a = at_ref[pl.ds(i * c + j, 1), :]
            return acc - a * tt_ref[pl.ds(pl.multiple_of(j * c, c), nr), :]
        if i > 0:
            acc = lax.fori_loop(0, i, body, acc, unroll=min(i, 8))
        tt_ref[i * c:i * c + nr, :] = acc
    for blk in range(c * c // LANE):
        o_ref[:, blk * LANE:(blk + 1) * LANE] = tt_ref[blk * LANE:(blk + 1) * LANE, :].T


def _dnsolve_call(a2):
    ni, cc = a2.shape
    assert ni % LANE == 0
    return pl.pallas_call(
        _dnsolve_kernel,
        grid=(ni // LANE,),
        in_specs=[pl.BlockSpec((LANE, cc), lambda i: (i, 0))],
        out_specs=pl.BlockSpec((LANE, cc), lambda i: (i, 0)),
        out_shape=jax.ShapeDtypeStruct((ni, cc), F32),
        scratch_shapes=[pltpu.VMEM((cc, LANE), F32), pltpu.VMEM((cc, LANE), F32)],
        compiler_params=_cp(("parallel",)),
        name="deltanet_solve",
    )(a2)


def _dnscan_kernel(u_ref, t_ref, qk_ref, be_ref, eg_ref, ek_ref, gt_ref, o_ref, *s_refs, n_dir, n_batch):
    @pl.when(pl.program_id(0) == 0)
    def _():
        for s_ref in s_refs:
            s_ref[...] = jnp.zeros_like(s_ref)

    scale = DN_HEAD_DIM ** -0.5
    rep = DN_V_HEADS // DN_QK_HEADS
    streams = [(d, b, h) for d in range(n_dir) for b in range(n_batch) for h in range(DN_V_HEADS)]

    def qkv(d, b, h):
        hq = h // rep
        q = u_ref[d, b, :, hq * DN_HEAD_DIM:(hq + 1) * DN_HEAD_DIM]
        k = u_ref[d, b, :, DN_QK_W + hq * DN_HEAD_DIM:DN_QK_W + (hq + 1) * DN_HEAD_DIM]
        v = u_ref[d, b, :, 2 * DN_QK_W + h * DN_HEAD_DIM:2 * DN_QK_W + (h + 1) * DN_HEAD_DIM]
        return q, k, v

    uws = []
    for d, b, h in streams:
        _, k, v = qkv(d, b, h)
        be = be_ref[d, b, :, h:h + 1]
        rhs = jnp.concatenate([v * be, (k * be) * eg_ref[d, b, :, h:h + 1]], axis=1).astype(BF16)
        uws.append(_dot(t_ref[d, b, h, 0].astype(BF16), rhs))
    wqs = []
    for i, (d, b, h) in enumerate(streams):
        q, _, _ = qkv(d, b, h)
        lhs = jnp.concatenate([uws[i][:, DN_HEAD_DIM:], q * (scale * eg_ref[d, b, :, h:h + 1])], axis=0)
        wqs.append(_dot(lhs.astype(BF16), s_refs[i][...].astype(BF16)))
    for i, (d, b, h) in enumerate(streams):
        _, k, _ = qkv(d, b, h)
        vnb = (uws[i][:, :DN_HEAD_DIM] - wqs[i][:DN_CHUNK]).astype(BF16)
        o_ref[d, b, :, h * DN_HEAD_DIM:(h + 1) * DN_HEAD_DIM] = (
            wqs[i][DN_CHUNK:] + _dot(qk_ref[d, b, h, 0].astype(BF16), vnb))
        kd = (k * ek_ref[d, b, :, h:h + 1]).astype(BF16)
        s_refs[i][...] = s_refs[i][...] * gt_ref[d, b, 0:1, h:h + 1] + lax.dot_general(
            kd, vnb, (((0,), (0,)), ((), ())), preferred_element_type=F32)


def _dnscan_call(u2, t6, qk6, be, eg, ek, gt, *, s_len):
    n_dir, n_batch, r_len, _ = u2.shape
    nc = r_len // DN_CHUNK
    ncl = s_len // DN_CHUNK

    def cidx(t):
        return jnp.where(t < nc - ncl, ncl + t, t - (nc - ncl))

    kern = functools.partial(_dnscan_kernel, n_dir=n_dir, n_batch=n_batch)
    gspec = pl.BlockSpec((n_dir, n_batch, DN_CHUNK, LANE), lambda t: (0, 0, cidx(t), 0))
    mspec = pl.BlockSpec((n_dir, n_batch, DN_V_HEADS, 1, DN_CHUNK, DN_CHUNK), lambda t: (0, 0, 0, cidx(t), 0, 0))
    return pl.pallas_call(
        kern,
        grid=(nc,),
        in_specs=[pl.BlockSpec((n_dir, n_batch, DN_CHUNK, DN_W), lambda t: (0, 0, cidx(t), 0)),
                  mspec, mspec, gspec, gspec, gspec, gspec],
        out_specs=pl.BlockSpec((n_dir, n_batch, DN_CHUNK, DN_V_W), lambda t: (0, 0, cidx(t), 0)),
        out_shape=jax.ShapeDtypeStruct((n_dir, n_batch, r_len, DN_V_W), F32),
        scratch_shapes=[pltpu.VMEM((DN_HEAD_DIM, DN_HEAD_DIM), F32)] * (n_dir * n_batch * DN_V_HEADS),
        compiler_params=_cp(("arbitrary",)),
        name="deltanet_scan",
    )(u2, t6, qk6, be, eg, ek, gt)


DN_OUT_COLS = 512


def _dnout_kernel(of_ref, ob_ref, z_ref, g_ref, jm_ref, o_ref):
    for h in range(DN_OUT_COLS // DN_HEAD_DIM):
        sl = slice(h * DN_HEAD_DIM, (h + 1) * DN_HEAD_DIM)
        o = of_ref[0, 0, :, sl] + _flip_rows(jm_ref[...], ob_ref[0, 0, :, sl])
        y = o * lax.rsqrt(jnp.mean(o * o, axis=-1, keepdims=True) + NORM_EPS) * g_ref[...]
        o_ref[0, :, sl] = (y * _silu(z_ref[0, :, sl])).astype(o_ref.dtype)


def _dnout_call(o2, p3, norm_g, *, s_len):
    _, n_batch, r_len, _ = o2.shape
    tr = FLIP_ROWS
    cw = DN_OUT_COLS
    ns, nc = s_len // tr, (r_len - s_len) // tr
    assert OFF_DZ % cw == 0

    def mirror(i):
        return jnp.where(i < ns, ns - 1 - i, 2 * ns + nc - 1 - i)

    return pl.pallas_call(
        _dnout_kernel,
        grid=(n_batch, r_len // tr, DN_V_W // cw),
        in_specs=[pl.BlockSpec((1, 1, tr, cw), lambda b, i, j: (0, b, i, j)),
                  pl.BlockSpec((1, 1, tr, cw), lambda b, i, j: (1, b, mirror(i), j)),
                  pl.BlockSpec((1, tr, cw), lambda b, i, j: (b, i, OFF_DZ // cw + j)),
                  pl.BlockSpec((1, DN_HEAD_DIM), lambda b, i, j: (0, 0)),
                  pl.BlockSpec((FLIP_ROWS, FLIP_ROWS), lambda b, i, j: (0, 0))],
        out_specs=pl.BlockSpec((1, tr, cw), lambda b, i, j: (b, i, j)),
        out_shape=jax.ShapeDtypeStruct((n_batch, r_len, DN_V_W), BF16),
        compiler_params=_cp(("parallel", "parallel", "parallel")),
        name="deltanet_out",
    )(o2, o2, p3, norm_g.reshape(1, DN_HEAD_DIM), _exchange_matrix())


def _seq_flip(a, s_len, axis):
    lat, ctx = jnp.split(a, [s_len], axis=axis)
    return jnp.concatenate([jnp.flip(lat, axis), jnp.flip(ctx, axis)], axis=axis)


def _deltanet_mixer(p3, conv_w, a_log, dt_bias, norm_g, *, s_len):
    n_batch, r_len, _ = p3.shape
    u2 = _dnprep_call(p3, conv_w, s_len=s_len)
    lg =p3[:, :, OFF_DL:OFF_DL + N_LOGITS].reshape(n_batch, r_len, 2, 2, DN_V_HEADS)

    def dirs(x):
        return jnp.stack([x[:, :, 0], _seq_flip(x[:, :, 1], s_len, 1)])

    la, lb = dirs(lg[:, :, 0]), dirs(lg[:, :, 1])
    padl = lambda x: jnp.pad(x, ((0, 0), (0, 0), (0, 0), (0, LANE - DN_V_HEADS)))
    a6, qk6, be, eg, ek, gt = _dnintra_call(u2, padl(la), padl(lb), jnp.swapaxes(la, 2, 3), a_log, dt_bias)
    t6 = _dnsolve_call(a6.reshape(-1, DN_CHUNK * DN_CHUNK)).reshape(a6.shape)
    o2 = _dnscan_call(u2, t6, qk6, be, eg, ek, gt, s_len=s_len)
    return _dnout_call(o2, p3, norm_g, s_len=s_len)


def _merge_kernel(ya_ref, yb_ref, yc_ref, wa_ref, wb_ref, wc_ref, ga_ref, gb_ref, gc_ref, o_ref):
    m = (jax.nn.sigmoid(ga_ref[...]) * _dot(ya_ref[...], wa_ref[...].astype(BF16))
         + jax.nn.sigmoid(gb_ref[...]) * _dot(yb_ref[...], wb_ref[...].astype(BF16))
         + jax.nn.sigmoid(gc_ref[...]) * _dot(yc_ref[...], wc_ref[...].astype(BF16)))
    o_ref[...] = m.astype(o_ref.dtype)


def _merge_call(ya, yb, yc, wa, wb, wc, p2, *, r_len):
    rt = ya.shape[0]
    d = wa.shape[1]
    tm = _pick(r_len, 1088, 16)
    tn = _pick(d, MERGE_ALIGN)
    assert OFF_MG % tn == 0
    yspec = lambda w: pl.BlockSpec((tm, w), lambda i, j: (i, 0))
    wspec = lambda w: pl.BlockSpec((w, tn), lambda i, j: (0, j))
    gspec = lambda br: pl.BlockSpec((tm, tn), lambda i, j, br=br: (i, (OFF_MG + br * d) // tn + j))
    return pl.pallas_call(
        _merge_kernel,
        grid=(rt // tm, d // tn),
        in_specs=[yspec(ATT_W), yspec(HY_W), yspec(DN_V_W), wspec(ATT_W), wspec(HY_W), wspec(DN_V_W),
                  gspec(0), gspec(1), gspec(2)],
        out_specs=pl.BlockSpec((tm, tn), lambda i, j: (i, j)),
        out_shape=jax.ShapeDtypeStruct((rt, d), BF16),
        compiler_params=_cp(("parallel", "parallel")),
        name="branch_merge",
    )(ya, yb, yc, wa, wb, wc, p2, p2, p2)


def _outproj_kernel(m_ref, w_ref, x_ref, gate_ref, o_ref, *, tm, tiles_per_batch, s_len, n_batch):
    is_ctx, gl, gc = _row_mods(gate_ref, pl.program_id(0), 0, tm, tm, tiles_per_batch, s_len, n_batch)
    o_ref[...] = x_ref[...] + jnp.where(is_ctx, gc, gl) * _dot(m_ref[...], w_ref[...].astype(BF16))


def _outproj_call(m, w_out, xs2, mods, *, n_batch, r_len, s_len):
    rt, d = xs2.shape
    tm = _pick(r_len, 1088, 16)
    tn = _pick(d, 512)
    kern = functools.partial(_outproj_kernel, tm=tm, tiles_per_batch=r_len // tm, s_len=s_len, n_batch=n_batch)
    return pl.pallas_call(
        kern,
        grid=(rt // tm, d // tn),
        in_specs=[pl.BlockSpec((tm, d), lambda i, j: (i, 0)),
                  pl.BlockSpec((d, tn), lambda i, j: (0, j)),
                  pl.BlockSpec((tm, tn), lambda i, j: (i, j)),
                  pl.BlockSpec((8, tn), lambda i, j: (0, 2 * d // tn + j))],
        out_specs=pl.BlockSpec((tm, tn), lambda i, j: (i, j)),
        out_shape=jax.ShapeDtypeStruct((rt, d), F32),
        compiler_params=_cp(("parallel", "parallel")),
        name="out_proj_residual",
    )(m, w_out, xs2, mods)


def _finalnorm_kernel(x_ref, g_ref, o_ref):
    x = x_ref[0]
    o_ref[0] = x * lax.rsqrt(jnp.mean(x * x, axis=-1, keepdims=True) + NORM_EPS) * g_ref[...]


def _finalnorm_call(xs, final_g, *, s_len):
    n_batch, _, d = xs.shape
    tr = _pick(s_len, 512, 8)
    return pl.pallas_call(
        _finalnorm_kernel,
        grid=(n_batch, s_len // tr),
        in_specs=[pl.BlockSpec((1, tr, d), lambda b, i: (b, i, 0)), pl.BlockSpec((1, d), lambda b, i: (0, 0))],
        out_specs=pl.BlockSpec((1, tr, d), lambda b, i: (b, i, 0)),
        out_shape=jax.ShapeDtypeStruct((n_batch, s_len, d), F32),
        compiler_params=_cp(("parallel", "parallel")),
        name="final_norm",
    )(xs, final_g.reshape(1, d))


def kernel(x, c, ctx, c_ctx, norm_g, w_mod, b_mod, w_in, q_norm_g, k_norm_g, hy_conv_w, hy_conv_b, hy_w1, hy_b1, hy_freq1, hy_w2, hy_b2, hy_freq2, hy_w3, hy_d, dn_conv_w, dn_a_log, dn_dt_bias, dn_norm_g, w_pa, w_pb, w_pc, w_out, final_g):
    n_batch, s_len, d = x.shape
    ctx_len = ctx.shape[1]
    r_len = s_len + ctx_len
    depth = w_in.shape[0]
    assert n_batch + 1 <= 8 and w_in.shape[2] == OFF_DL + N_LOGITS + N_BRANCH * d

    xs = jnp.concatenate([x, ctx], axis=1)
    cs = jnp.zeros((8, d), F32).at[:n_batch].set(c).at[n_batch].set(c_ctx)
    mods = _mod_call(cs, w_mod, b_mod)
    cos_t, sin_t = _rope_tables(s_len, ctx_len)
    w1p = jnp.pad(hy_w1, ((0, 0), (0, LANE - HY_EMB), (0, 0)))

    for layer in range(depth):
        need_ctx = layer < depth - 1
        xs2 = xs.reshape(n_batch * r_len, d)
        p2 = _inproj_call(xs2, norm_g[layer], mods[layer], w_in[layer], _inproj_tail(w_in[layer]),
                          n_batch=n_batch, r_len=r_len, s_len=s_len)
        p3 = p2.reshape(n_batch, r_len, -1)

        ya = _attn_call(p3, cos_t, sin_t, q_norm_g[layer], k_norm_g[layer], s_len=s_len)

        filt = (w1p[layer], hy_b1[layer], hy_freq1[layer], hy_w2[layer], hy_b2[layer], hy_freq2[layer],
                hy_w3[layer])
        yb = _hyena_mixer(p3, 0, s_len, hy_conv_w[layer], hy_conv_b[layer], hy_d[layer], filt)
        if need_ctx:
            yb_c = _hyena_mixer(p3, s_len // ctx_len, ctx_len, hy_conv_w[layer], hy_conv_b[layer],
                                hy_d[layer], filt)
        else:
            yb_c = jnp.zeros((n_batch, ctx_len, HY_W), BF16)
        yb = jnp.concatenate([yb, yb_c], axis=1)

        yc = _deltanet_mixer(p3, dn_conv_w[layer], dn_a_log[layer], dn_dt_bias[layer], dn_norm_g[layer],
                             s_len=s_len)

        rt = n_batch * r_len
        m = _merge_call(ya.reshape(rt, ATT_W), yb.reshape(rt, HY_W), yc.reshape(rt, DN_V_W),
                        w_pa[layer], w_pb[layer], w_pc[layer], p2, r_len=r_len)
        xs = _outproj_call(m, w_out[layer], xs2, mods[layer],
                           n_batch=n_batch, r_len=r_len, s_len=s_len).reshape(n_batch, r_len, d)

    return _finalnorm_call(xs, final_g, s_len=s_len)
```

```python
import functools
import math

import jax
import jax.numpy as jnp
import numpy as np
from jax import lax
from jax.experimental import pallas as pl
from jax.experimental.pallas import tpu as pltpu

F32 = jnp.float32
BF16 = jnp.bfloat16

GRID_W = 64
NORM_EPS = 1e-6
N_BRANCH = 3

ATT_HEADS = 8
ATT_KV_HEADS = 2
HEAD_DIM = 128
ATT_GROUP = ATT_HEADS // ATT_KV_HEADS
ATT_W = ATT_HEADS * HEAD_DIM
ATT_KV_W = ATT_KV_HEADS * HEAD_DIM
AXIS_ROPE_DIM = HEAD_DIM // 2
ROPE_THETA = 10000.0

HY_W = 1024
HY_ORDER = 2
HY_BANDS = 16
HY_EMB = 1 + 2 * HY_BANDS
HY_FILTER_HIDDEN = 64
HY_DECAY_TARGET = 1e-2
HY_DECAY_FAST = 0.3
HY_DECAY_SLOW = 1.5

DN_QK_HEADS = 4
DN_V_HEADS = 8
DN_HEAD_DIM = 128
DN_QK_W = DN_QK_HEADS * DN_HEAD_DIM
DN_V_W = DN_V_HEADS * DN_HEAD_DIM
DN_CHUNK = 64
DN_W = 2 * DN_QK_W + DN_V_W

LANE = 128
MIB = 1024 * 1024

OFF_AQ = 0
OFF_AK = OFF_AQ + ATT_W
OFF_AV = OFF_AK + ATT_KV_W
OFF_AG = OFF_AV + ATT_KV_W
OFF_HV = OFF_AG + ATT_W
OFF_HX1 = OFF_HV + HY_W
OFF_HX2 = OFF_HX1 + HY_W
OFF_HG = OFF_HX2 + HY_W
OFF_DQ = OFF_HG + HY_W
OFF_DV = OFF_DQ + 2 * DN_QK_W
OFF_DZ = OFF_DV + DN_V_W
OFF_DL = OFF_DZ + DN_V_W
N_LOGITS = 4 * DN_V_HEADS
MERGE_ALIGN = 512
OFF_MG = -(-(OFF_DL + N_LOGITS) // MERGE_ALIGN) * MERGE_ALIGN


def _pick(n, cap, mult=LANE):
    best = None
    for t in range(mult, min(n, cap) + 1, mult):
        if n % t == 0:
            best = t
    assert best is not None, (n, cap, mult)
    return best


def _cp(sem, vmem_mib=48):
    return pltpu.CompilerParams(dimension_semantics=sem, vmem_limit_bytes=vmem_mib * MIB)


def _silu(x):
    return x * jax.nn.sigmoid(x)


def _split3(x):
    x1 = x.astype(BF16)
    r = x - x1.astype(F32)
    x2 = r.astype(BF16)
    x3 = (r - x2.astype(F32)).astype(BF16)
    return x1, x2, x3


def _dot(a, b):
    return jnp.dot(a, b, preferred_element_type=F32)


def _dot_hp(a, b):
    a1 = a.astype(BF16)
    a2 = (a - a1.astype(F32)).astype(BF16)
    b1 = b.astype(BF16)
    b2 = (b - b1.astype(F32)).astype(BF16)
    return _dot(a1, b1) + (_dot(a1, b2) + _dot(a2, b1))


def _mod_kernel(cs_ref, w_ref, b_ref, o_ref):
    cs = cs_ref[...]
    o_ref[0] = _dot(_silu(cs).astype(BF16), w_ref[0].astype(BF16)) + b_ref[0]


def _mod_call(cs, w_mod, b_mod):
    depth, d, d3 = w_mod.shape
    tn = _pick(d3, 512)
    return pl.pallas_call(
        _mod_kernel,
        grid=(depth, d3 // tn),
        in_specs=[pl.BlockSpec((8, d), lambda l, j: (0, 0)),
                  pl.BlockSpec((1, d, tn), lambda l, j: (l, 0, j)),
                  pl.BlockSpec((1, 1, tn), lambda l, j: (l, 0, j))],
        out_specs=pl.BlockSpec((1, 8, tn), lambda l, j: (l, 0, j)),
        out_shape=jax.ShapeDtypeStruct((depth, 8, d3), F32),
        compiler_params=_cp(("parallel", "parallel")),
        name="adaln_mod",
    )(cs, w_mod, b_mod.reshape(depth, 1, d3))


def _row_mods(mod_ref, i, r0, rows, tm, tiles_per_batch, s_len, n_batch):
    b = i // tiles_per_batch
    row = (i % tiles_per_batch) * tm + r0 + lax.broadcasted_iota(jnp.int32, (rows, 1), 0)
    return row >= s_len, mod_ref[pl.ds(b, 1), :], mod_ref[n_batch:n_batch + 1, :]


NORM_ROWS = 64


def _modnorm_kernel(x_ref, g_ref, mod_ref, h_ref, *, tm, tiles_per_batch, s_len, n_batch, d):
    i = pl.program_id(0)

    def body(c, carry):
        r0 = pl.multiple_of(c * NORM_ROWS, NORM_ROWS)
        x = x_ref[pl.ds(r0, NORM_ROWS), :]
        y = x * lax.rsqrt(jnp.mean(x * x, axis=-1, keepdims=True) + NORM_EPS) * g_ref[...]
        is_ctx, ml, mc = _row_mods(mod_ref, i, r0, NORM_ROWS, tm, tiles_per_batch, s_len, n_batch)
        shift = jnp.where(is_ctx, mc[:, :d], ml[:, :d])
        scale = jnp.where(is_ctx, mc[:, d:2 * d], ml[:, d:2 * d])
        h_ref[pl.ds(r0, NORM_ROWS), :] = (y * (1.0 + scale) + shift).astype(BF16)
        return carry
    lax.fori_loop(0, tm // NORM_ROWS, body, 0)


def _modnorm_call(xs2, norm_g, mods, *, n_batch, r_len, s_len):
    rt, d = xs2.shape
    tm = _pick(r_len, 576, NORM_ROWS)
    kern = functools.partial(_modnorm_kernel, tm=tm, tiles_per_batch=r_len // tm, s_len=s_len,
                             n_batch=n_batch, d=d)
    return pl.pallas_call(
        kern,
        grid=(rt // tm,),
        in_specs=[pl.BlockSpec((tm, d), lambda i: (i, 0)),
                  pl.BlockSpec((1, d), lambda i: (0, 0)),
                  pl.BlockSpec((8, 3 * d), lambda i: (0, 0))],
        out_specs=pl.BlockSpec((tm, d), lambda i: (i, 0)),
        out_shape=jax.ShapeDtypeStruct((rt, d), BF16),
        compiler_params=_cp(("parallel",)),
        name="mod_norm",
    )(xs2, norm_g.reshape(1, d), mods)


def _inproj_kernel(h_ref, wm_ref, wt_ref, o_ref, *, n_main):
    j = pl.program_id(1)

    @pl.when(j < n_main)
    def _():
        o_ref[...] = _dot(h_ref[...], wm_ref[0].astype(BF16))

    @pl.when(j >= n_main)
    def _():
        o_ref[...] = _dot(h_ref[...], wt_ref[...])


def _inproj_tiles(d):
    for tn in (512, 256, 128):
        main = OFF_DL // tn * tn
        if (OFF_MG - main + N_BRANCH * d) % tn == 0:
            return tn, main // tn
    raise ValueError(d)


def _inproj_tail(w_in_l):
    d = w_in_l.shape[0]
    tn, n_main = _inproj_tiles(d)
    cut = OFF_DL + N_LOGITS
    pad = jnp.zeros((d, OFF_MG - cut), w_in_l.dtype)
    return jnp.concatenate([w_in_l[:, n_main * tn:cut], pad, w_in_l[:, cut:]], axis=1).astype(BF16)


def _inproj_call(h, w_in, layer, w_tail):
    rt, d = h.shape
    tn, n_main = _inproj_tiles(d)
    nw = n_main * tn + w_tail.shape[1]
    assert nw == OFF_MG + N_BRANCH * d
    tm = _pick(rt, 2176, 16)
    kern = functools.partial(_inproj_kernel, n_main=n_main)
    return pl.pallas_call(
        kern,
        grid=(rt // tm, nw // tn),
        in_specs=[pl.BlockSpec((tm, d), lambda i, j: (i, 0)),
                  pl.BlockSpec((1, d, tn), lambda i, j: (layer, 0, jnp.minimum(j, n_main - 1))),
                  pl.BlockSpec((d, tn), lambda i, j: (0, jnp.maximum(j - n_main, 0)))],
        out_specs=pl.BlockSpec((tm, tn), lambda i, j: (i, j)),
        out_shape=jax.ShapeDtypeStruct((rt, nw), F32),
        compiler_params=_cp(("parallel", "arbitrary"), 56),
        name="in_proj",
    )(h, w_in, w_tail)


def _rope_tables(s_len, ctx_len):
    m = AXIS_ROPE_DIM // 2
    inv_freq = ROPE_THETA ** (-np.arange(0, AXIS_ROPE_DIM, 2, dtype=np.float64) / AXIS_ROPE_DIM)
    t = np.arange(s_len)
    ang_r = (t // GRID_W)[:, None] * inv_freq
    ang_c = (t % GRID_W)[:, None] * inv_freq
    cos = np.concatenate([np.cos(ang_r)] * 2 + [np.cos(ang_c)] * 2, axis=-1)
    sin = np.concatenate([-np.sin(ang_r), np.sin(ang_r), -np.sin(ang_c), np.sin(ang_c)], axis=-1)
    assert cos.shape[1] == 4 * m == HEAD_DIM
    cos = np.concatenate([cos, np.ones((ctx_len, HEAD_DIM))], axis=0)
    sin = np.concatenate([sin, np.zeros((ctx_len, HEAD_DIM))], axis=0)
    return jnp.asarray(cos, F32), jnp.asarray(sin, F32)


def _norm_rope(x, g, cs, sn):
    y = x * lax.rsqrt(jnp.mean(x * x, axis=-1, keepdims=True) + NORM_EPS) * g
    lane = lax.broadcasted_iota(jnp.int32, (1, HEAD_DIM), 1)
    first = (lane % AXIS_ROPE_DIM) < (AXIS_ROPE_DIM // 2)
    q = AXIS_ROPE_DIM // 2
    partner = jnp.where(first, pltpu.roll(y, HEAD_DIM - q, 1), pltpu.roll(y, q, 1))
    return y * cs + partner * sn


def _attn_kernel(q_ref, k_ref, v_ref, gt_ref, cos_ref, sin_ref, qg_ref, kg_ref, o_ref, ks_ref, vs_ref,
                 *, tq, r_len, s_len, kv_chunk):
    i = pl.program_id(2)

    @pl.when(i == 0)
    def _():
        def body(c, carry):
            r0 = pl.multiple_of(c * kv_chunk, kv_chunk)
            kk = _norm_rope(k_ref[0, pl.ds(r0, kv_chunk), :], kg_ref[...],
                            cos_ref[pl.ds(r0, kv_chunk), :], sin_ref[pl.ds(r0, kv_chunk), :])
            ks_ref[pl.ds(r0, kv_chunk), :] = kk.astype(BF16)
            vs_ref[pl.ds(r0, kv_chunk), :] = v_ref[0, pl.ds(r0, kv_chunk), :].astype(BF16)
            return carry
        lax.fori_loop(0, r_len // kv_chunk, body, 0)

    r0 = pl.multiple_of(i * tq, tq)
    cs = cos_ref[pl.ds(r0, tq), :]
    sn = sin_ref[pl.ds(r0, tq), :]
    scale = HEAD_DIM ** -0.5

    def heads(k_lo, k_hi):
        for g in range(ATT_GROUP):
            sl = slice(g * HEAD_DIM, (g + 1) * HEAD_DIM)
            qh = (_norm_rope(q_ref[0, :, sl], qg_ref[...], cs, sn) * scale).astype(BF16)
            s = lax.dot_general(qh, ks_ref[k_lo:k_hi, :], (((1,), (1,)), ((), ())),
                                preferred_element_type=F32)
            e = jnp.exp(s - jnp.max(s, axis=-1, keepdims=True))
            den = jnp.sum(e, axis=-1, keepdims=True)
            o = _dot(e.astype(BF16), vs_ref[k_lo:k_hi, :]) / den
            o_ref[0, :, sl] = (o * _silu(gt_ref[0, :, sl])).astype(o_ref.dtype)

    @pl.when(i < s_len // tq)
    def _():
        heads(0, r_len)

    @pl.when(i >= s_len // tq)
    def _():
        heads(s_len, r_len)


def _attn_call(p3, cos_t, sin_t, q_g, k_g, *, s_len):
    n_batch, r_len, _ = p3.shape
    ctx_len = r_len - s_len
    tq = 256 if (ctx_len % 256 == 0 and s_len % 256 == 0) else 128
    assert ctx_len % tq == 0 and s_len % tq == 0
    gw = ATT_GROUP * HEAD_DIM
    kern = functools.partial(_attn_kernel, tq=tq, r_len=r_len, s_len=s_len, kv_chunk=tq)
    return pl.pallas_call(
        kern,
        grid=(n_batch, ATT_KV_HEADS, r_len // tq),
        in_specs=[pl.BlockSpec((1, tq, gw), lambda b, h, i: (b, i, OFF_AQ // gw + h)),
                  pl.BlockSpec((1, r_len, HEAD_DIM), lambda b, h, i: (b, 0, OFF_AK // HEAD_DIM + h)),
                  pl.BlockSpec((1, r_len, HEAD_DIM), lambda b, h, i: (b, 0, OFF_AV // HEAD_DIM + h)),
                  pl.BlockSpec((1, tq, gw), lambda b, h, i: (b, i, OFF_AG // gw + h)),
                  pl.BlockSpec((r_len, HEAD_DIM), lambda b, h, i: (0, 0)),
                  pl.BlockSpec((r_len, HEAD_DIM), lambda b, h, i: (0, 0)),
                  pl.BlockSpec((1, HEAD_DIM), lambda b, h, i: (0, 0)),
                  pl.BlockSpec((1, HEAD_DIM), lambda b, h, i: (0, 0))],
        out_specs=pl.BlockSpec((1, tq, gw), lambda b, h, i: (b, i, h)),
        out_shape=jax.ShapeDtypeStruct((n_batch, r_len, ATT_W), BF16),
        scratch_shapes=[pltpu.VMEM((r_len, HEAD_DIM), BF16), pltpu.VMEM((r_len, HEAD_DIM), BF16)],
        compiler_params=_cp(("parallel", "parallel", "arbitrary")),
        name="gqa_attention",
    )(p3, p3, p3, p3, cos_t, sin_t, q_g.reshape(1, HEAD_DIM), k_g.reshape(1, HEAD_DIM))


def _conv3(x, w, first, last):
    n = x.shape[0]
    prev = jnp.where(first, 0.0, pltpu.roll(x, 1, 0))
    nxt = jnp.where(last, 0.0, pltpu.roll(x, n - 1, 0))
    return prev * w[0:1, :] + x * w[1:2, :] + nxt * w[2:3, :]


FFT_UNROLL = 8


def _fft_split(n):
    if n <= 512:
        return 1, n
    n2 = 64
    return n // n2, n2


@functools.lru_cache(maxsize=None)
def _fft_mats(n, n1, n2):
    half = n2 // 2
    a = np.arange(n1)[:, None, None]
    k2 = np.arange(n2)[None, :, None]

    def g(bs):
        ph = (k2 * (a + n1 * bs[None, None, :])) % n
        ang = -2.0 * np.pi * ph / n
        return np.cos(ang), np.sin(ang)

    gre, gim = g(np.arange(half))
    gal = np.concatenate([gre, gim], axis=1)
    gar = np.concatenate([-gim, gre], axis=1)
    hre = np.swapaxes(gre, 1, 2) / n
    him = -np.swapaxes(gim, 1, 2) / n
    hal = np.concatenate([hre, him], axis=1)
    har = np.concatenate([-him, hre], axis=1)
    fre_f, fim_f = g(np.arange(n2))
    gf = np.concatenate([fre_f, fim_f], axis=1)
    k1 = np.arange(n1)
    ang1 = -2.0 * np.pi * ((k1[:, None] * k1[None, :]) % n1) / n1
    f1re, f1im = np.cos(ang1), np.sin(ang1)
    fbl = np.concatenate([f1re, f1im], axis=0)
    fbr = np.concatenate([-f1im, f1re], axis=0)
    fbil = np.concatenate([f1re, -f1im], axis=0)
    fbir = np.concatenate([f1im, f1re], axis=0)
    cast = lambda m: jnp.asarray(m, BF16)
    return dict(ga=cast(np.concatenate([gal, gar], axis=2)), ha=cast(np.concatenate([hal, har], axis=2)),
                gf=cast(gf), fb=cast(np.concatenate([fbl, fbr], axis=1)),
                fbi=cast(np.concatenate([fbil, fbir], axis=1)))


@functools.lru_cache(maxsize=None)
def _hy_tables(n_tok):
    pos = np.arange(n_tok, dtype=np.float64)
    t = pos / max(n_tok - 1, 1)
    bands = np.linspace(1e-4, HY_BANDS - 1, HY_BANDS)
    ang = (2.0 * np.pi / n_tok) * pos[:, None] * bands
    z = np.concatenate([t[:, None], np.cos(ang), np.sin(ang)], axis=-1)
    zrev = np.zeros_like(z)
    zrev[1:] = z[:0:-1]
    ztab = np.zeros((2 * n_tok, LANE))
    ztab[:, :HY_EMB] = np.concatenate([z, zrev], axis=0)
    deltas = np.abs(np.linspace(math.log(HY_DECAY_TARGET) / HY_DECAY_SLOW,
                                math.log(HY_DECAY_TARGET) / HY_DECAY_FAST, HY_W))
    return jnp.asarray(ztab, F32), jnp.asarray(np.tile(deltas, HY_ORDER)[None, :], F32)


SLAB_PAD = 8


def _hyfilt_kernel(z_ref, w1_ref, b1_ref, f1_ref, w2_ref, b2_ref, f2_ref, w3_ref, dl_ref, k_ref, s_ref,
                   *, n_tok, tr, slab_rows):
    i = pl.program_id(0)
    z = z_ref[...]
    h = jnp.sin(f1_ref[...] * (_dot_hp(z, w1_ref[...]) + b1_ref[...]))
    h = jnp.sin(f2_ref[...] * (_dot_hp(h, w2_ref[...]) + b2_ref[...]))
    h = _dot(h.astype(BF16), w3_ref[...].astype(BF16)) * jnp.exp(-z[:, 0:1] * dl_ref[...])
    row = i * tr + lax.broadcasted_iota(jnp.int32, (tr, 1), 0)
    h = jnp.where(row == n_tok, 0.0, h)
    if slab_rows is None:
        k_ref[...] = h
    else:
        step = slab_rows + SLAB_PAD
        for s in range(tr // slab_rows):
            k_ref[s * step:s * step + slab_rows, :] = h[s * slab_rows:(s + 1) * slab_rows]
            k_ref[s * step + slab_rows:(s + 1) * step, :] = jnp.zeros((SLAB_PAD, h.shape[1]), F32)

    @pl.when(i == 0)
    def _():
        s_ref[...] = jnp.zeros_like(s_ref)

    s_ref[...] += jnp.sum(jnp.abs(h), axis=0, keepdims=True)


def _hyfilt_call(n_tok, w1p, b1, f1, w2, b2, f2, w3):
    ztab, dl = _hy_tables(n_tok)
    n = 2 * n_tok
    tr = _pick(n_tok, 512, 8)
    ow = HY_ORDER * HY_W
    hid = HY_FILTER_HIDDEN
    n1, _ = _fft_split(n)
    slab_rows = n1 if n1 > 1 else None
    assert slab_rows is None or tr % slab_rows == 0
    out_tr = tr if slab_rows is None else (tr // slab_rows) * (slab_rows + SLAB_PAD)
    kern = functools.partial(_hyfilt_kernel, n_tok=n_tok, tr=tr, slab_rows=slab_rows)
    c2 = lambda i: (0, 0)
    return pl.pallas_call(
        kern,
        grid=(n // tr,),
        in_specs=[pl.BlockSpec((tr, LANE), lambda i: (i, 0)),
                  pl.BlockSpec((LANE, hid), c2), pl.BlockSpec((1, hid), c2), pl.BlockSpec((1, hid), c2),
                  pl.BlockSpec((hid, hid), c2), pl.BlockSpec((1, hid), c2), pl.BlockSpec((1, hid), c2),
                  pl.BlockSpec((hid, ow), lambda i: (0, i // (n_tok // tr))),
                  pl.BlockSpec((1, ow), c2)],
        out_specs=[pl.BlockSpec((out_tr, ow), lambda i: (i, 0)), pl.BlockSpec((1, ow), c2)],
        out_shape=[jax.ShapeDtypeStruct((n // tr * out_tr, ow), F32), jax.ShapeDtypeStruct((1, ow), F32)],
        compiler_params=_cp(("arbitrary",)),
        name="hyena_filter",
    )(ztab, w1p, b1.reshape(1, hid), f1.reshape(1, hid), w2, b2.reshape(1, hid), f2.reshape(1, hid), w3, dl)


FFT_GROUP = 4


def _filtfft_kernel(k_ref, s_ref, gf_ref, fb_ref, o_ref, z_ref, *, n, n1, n2):
    inv = 1.0 / (s_ref[...] + 1e-6)
    if n1 == 1:
        o_ref[...] = _dot(gf_ref[0], k_ref[...].astype(BF16)) * inv
        return
    slab = 2 * n1
    zs, ks = slab + SLAB_PAD, n1 + SLAB_PAD

    def stage_a(a, carry):
        out = _dot(gf_ref[a], k_ref[pl.ds(a, n2, stride=ks), :].astype(BF16))
        z_ref[pl.ds(a, n2, stride=zs), :] = out[:n2]
        z_ref[pl.ds(n1 + a, n2, stride=zs), :] = out[n2:]
        return carry
    lax.fori_loop(0, n1, stage_a, 0, unroll=FFT_UNROLL)

    def stage_b(g, carry):
        for u in range(FFT_GROUP):
            k2 = g * FFT_GROUP + u
            z = z_ref[pl.ds(pl.multiple_of(k2 * zs, 8), slab), :].astype(BF16)
            o_ref[pl.ds(pl.multiple_of(k2 * slab, slab), slab), :] = _dot(fb_ref[...], z) * inv
        return carry
    lax.fori_loop(0, n2 // FFT_GROUP, stage_b, 0)


def _filtfft_call(k_un, asum, n_tok):
    k_rows, ow = k_un.shape
    n = 2 * n_tok
    n1, n2 = _fft_split(n)
    assert k_rows == (n if n1 == 1 else n2 * (n1 + SLAB_PAD))
    m = _fft_mats(n, n1, n2)
    cb = LANE
    kern = functools.partial(_filtfft_kernel, n=n, n1=n1, n2=n2)
    return pl.pallas_call(
        kern,
        grid=(ow // cb,),
        in_specs=[pl.BlockSpec((k_rows, cb), lambda j: (0, j)),
                  pl.BlockSpec((1, cb), lambda j: (0, j)),
                  pl.BlockSpec(m["gf"].shape, lambda j: (0, 0, 0)),
                  pl.BlockSpec(m["fb"].shape, lambda j: (0, 0))],
        out_specs=pl.BlockSpec((2 * n, cb), lambda j: (0, j)),
        out_shape=jax.ShapeDtypeStruct((2 * n, ow), F32),
        scratch_shapes=[pltpu.VMEM((n2 * (2 * n1 + SLAB_PAD) if n1 > 1 else 8, cb), F32)],
        compiler_params=_cp(("parallel",)),
        name="hyena_filter_fft",
    )(k_un, asum, m["gf"], m["fb"])


def _hyena_kernel(*refs, n_tok, n1, n2, conv_a, has_gate):
    it = iter(refs)
    a_ref, m_ref = next(it), next(it)
    g_ref = next(it) if has_gate else None
    if conv_a:
        cwa_ref, cba_ref = next(it), next(it)
    cwm_ref, cbm_ref, d_ref, ks_ref = next(it), next(it), next(it), next(it)
    ga_ref, ha_ref = next(it), next(it)
    if n1 > 1:
        fb_ref, fbi_ref = next(it), next(it)
    o_ref, ac_ref, z_ref = next(it), next(it), next(it)

    n = 2 * n_tok
    half = n2 // 2
    row = lax.broadcasted_iota(jnp.int32, (n_tok, 1), 0)
    first, last = row == 0, row == n_tok - 1
    sr = n1 if n1 > 1 else n_tok
    astep = sr + SLAB_PAD if n1 > 1 else sr
    for b in range(2):
        a = a_ref[b]
        if conv_a:
            a = _conv3(a, cwa_ref[...], first, last) + cba_ref[...]
        for s in range(n_tok // sr):
            ac_ref[b, s * astep:s * astep + sr, :] = a[s * sr:(s + 1) * sr]

    def spectrum_mul(x, k, h):
        xre, xim, kre, kim = x[:h], x[h:], k[:h], k[h:]
        return jnp.concatenate([xre * kre - xim * kim, xre * kim + xim * kre], axis=0).astype(BF16)

    if n1 == 1:
        x = _dot(ga_ref[0], jnp.concatenate([ac_ref[0], ac_ref[1]], axis=0).astype(BF16))
        y = _dot(ha_ref[0], spectrum_mul(x, ks_ref[...], n))
        for b in range(2):
            ac_ref[b] = y[b * n_tok:(b + 1) * n_tok] + ac_ref[b] * d_ref[0]
    else:
        slab = 2 * n1
        zs = slab + SLAB_PAD

        def stage_a(a, carry):
            rows = jnp.concatenate([ac_ref[0, pl.ds(a, half, stride=astep), :],
                                    ac_ref[1, pl.ds(a, half, stride=astep), :]], axis=0).astype(BF16)
            out = _dot(ga_ref[a], rows)
            z_ref[pl.ds(a, n2, stride=zs), :] = out[:n2]
            z_ref[pl.ds(n1 + a, n2, stride=zs), :] = out[n2:]
            return carry
        lax.fori_loop(0, n1, stage_a, 0, unroll=FFT_UNROLL)

        def stage_b(g, carry):
            k2s = [g * FFT_GROUP + u for u in range(FFT_GROUP)]
            zrows = [pl.ds(pl.multiple_of(k2 * zs, 8), slab) for k2 in k2s]
            xs = [_dot(fb_ref[...], z_ref[zr, :].astype(BF16)) for zr in zrows]
            ys = [spectrum_mul(x, ks_ref[pl.ds(pl.multiple_of(k2 * slab, slab), slab), :], n1)
                  for x, k2 in zip(xs, k2s)]
            for y, zr in zip(ys, zrows):
                z_ref[zr, :] = _dot(fbi_ref[...], y)
            return carry
        lax.fori_loop(0, n2 // FFT_GROUP, stage_b, 0)

        def stage_a_inv(a, carry):
            rows = jnp.concatenate([z_ref[pl.ds(a, n2, stride=zs), :],
                                    z_ref[pl.ds(n1 + a, n2, stride=zs), :]], axis=0).astype(BF16)
            y = _dot(ha_ref[a], rows)
            for b in range(2):
                cur = ac_ref[b, pl.ds(a, half, stride=astep), :]
                ac_ref[b, pl.ds(a, half, stride=astep), :] = y[b * half:(b + 1) * half] + cur * d_ref[0]
            return carry
        lax.fori_loop(0, n1, stage_a_inv, 0, unroll=FFT_UNROLL)

    for b in range(2):
        mc = _conv3(m_ref[b], cwm_ref[...], first, last) + cbm_ref[...]
        for s in range(n_tok // sr):
            rows = slice(s * sr, (s + 1) * sr)
            out = mc[rows] * ac_ref[b, s * astep:s * astep + sr, :]
            if has_gate:
                out = out * _silu(g_ref[b, rows, :])
            o_ref[b, rows, :] = out.astype(o_ref.dtype)


def _hyena_call(a_arr, a_blk, m_arr, m_blk, gate, conv_a, conv_m, d_row, spec, spec_blk, *, n_tok, out_dtype):
    n_batch = a_arr.shape[0]
    assert n_batch % 2 == 0
    n = 2 * n_tok
    n1, n2 = _fft_split(n)
    m = _fft_mats(n, n1, n2)
    cb = LANE
    one = pl.Buffered(1)

    def tile(blk):
        return pl.BlockSpec((2, n_tok, cb), lambda p, j, blk=blk: (p, blk[0], blk[1] + j), pipeline_mode=one)

    def convspec(c0):
        return [pl.BlockSpec((3, cb), lambda p, j, c0=c0: (0, c0 + j)),
                pl.BlockSpec((1, cb), lambda p, j, c0=c0: (0, c0 + j))]

    ins, specs = [a_arr, m_arr], [tile(a_blk), tile(m_blk)]
    if gate is not None:
        ins.append(gate[0])
        specs.append(tile(gate[1]))
    if conv_a is not None:
        ins += [conv_a[0], conv_a[1]]
        specs += convspec(conv_a[2])
    ins += [conv_m[0], conv_m[1], d_row, spec]
    specs += convspec(conv_m[2])
    specs += [pl.BlockSpec((1, 1, cb), lambda p, j: (0, 0, j)),
              pl.BlockSpec((2 * n, cb), lambda p, j, s0=spec_blk: (0, s0 + j), pipeline_mode=one)]
    names = ["ga", "ha"] + (["fb", "fbi"] if n1 > 1 else [])
    for nm in names:
        ins.append(m[nm])
        specs.append(pl.BlockSpec(m[nm].shape, lambda p, j, nd=m[nm].ndim: (0,) * nd, pipeline_mode=one))
    kern = functools.partial(_hyena_kernel, n_tok=n_tok, n1=n1, n2=n2, conv_a=conv_a is not None,
                             has_gate=gate is not None)
    return pl.pallas_call(
        kern,
        grid=(n_batch // 2, HY_W // cb),
        in_specs=specs,
        out_specs=pl.BlockSpec((2, n_tok, cb), lambda p, j: (p, 0, j)),
        out_shape=jax.ShapeDtypeStruct((n_batch, n_tok, HY_W), out_dtype),
        scratch_shapes=[pltpu.VMEM((2, n_tok // n1 * (n1 + SLAB_PAD) if n1 > 1 else n_tok, cb), F32),
                        pltpu.VMEM((n2 * (2 * n1 + SLAB_PAD) if n1 > 1 else 8, cb), F32)],
        compiler_params=_cp(("parallel", "arbitrary"), 56),
        name="hyena_conv",
    )(*ins)


def _hyena_mixer(p3, row_blk, n_tok, conv_w, conv_b, hy_d, filt):
    k_un, asum = _hyfilt_call(n_tok, *filt)
    spec = _filtfft_call(k_un, asum, n_tok)
    cbias = conv_b.reshape(1, -1)
    cblk = HY_W // LANE
    d3 = hy_d.reshape(HY_ORDER, 1, HY_W)
    z = _hyena_call(p3, (row_blk, OFF_HV // LANE), p3, (row_blk, OFF_HX1 // LANE), None,
                    (conv_w, cbias, 0), (conv_w, cbias, cblk), d3[0:1], spec, 0,
                    n_tok=n_tok, out_dtype=F32)
    return _hyena_call(z, (0, 0), p3, (row_blk, OFF_HX2 // LANE), (p3, (row_blk, OFF_HG // LANE)),
                       None, (conv_w, cbias, 2 * cblk), d3[1:2], spec, cblk,
                       n_tok=n_tok, out_dtype=BF16)


def _softplus(x):
    return jnp.maximum(x, 0.0) + jnp.log1p(jnp.exp(-jnp.abs(x)))


FLIP_ROWS = 256


def _exchange_matrix():
    return jnp.asarray(np.eye(FLIP_ROWS)[::-1], BF16)


def _flip_rows(jm, x):
    x1, x2, x3 = _split3(x)
    return _dot(jm, x1) + (_dot(jm, x2) + _dot(jm, x3))


def _dnprep_kernel(x_ref, w_ref, jm_ref, o_ref, *, r_len, s_len):
    j = pl.program_id(1)
    row = lax.broadcasted_iota(jnp.int32, (r_len, 1), 0)
    first = (row == 0) | (row == s_len)
    last = (row == s_len - 1) | (row == r_len - 1)
    u = _silu(_conv3(x_ref[0], w_ref[...], first, last))
    nrm = u * lax.rsqrt(jnp.sum(u * u, axis=-1, keepdims=True) + 1e-6)
    o_ref[0, 0] = jnp.where(j < 2 * DN_QK_HEADS, nrm, u)
    for seg0, seg_len in ((0, s_len), (s_len, r_len - s_len)):
        nt = seg_len // FLIP_ROWS
        for t in range(nt):
            src = seg0 + t * FLIP_ROWS
            dst = seg0 + (nt - 1 - t) * FLIP_ROWS
            o_ref[1, 0, dst:dst + FLIP_ROWS, :] = _flip_rows(jm_ref[...], o_ref[0, 0, src:src + FLIP_ROWS, :])


def _dnprep_call(p3, conv_w, *, s_len):
    n_batch, r_len, _ = p3.shape
    assert s_len % FLIP_ROWS == 0 and (r_len - s_len) % FLIP_ROWS == 0
    kern = functools.partial(_dnprep_kernel, r_len=r_len, s_len=s_len)
    return pl.pallas_call(
        kern,
        grid=(n_batch, DN_W // LANE),
        in_specs=[pl.BlockSpec((1, r_len, LANE), lambda b, j: (b, 0, OFF_DQ // LANE + j)),
                  pl.BlockSpec((3, LANE), lambda b, j: (0, j)),
                  pl.BlockSpec((FLIP_ROWS, FLIP_ROWS), lambda b, j: (0, 0))],
        out_specs=pl.BlockSpec((2, 1, r_len, LANE), lambda b, j: (0, b, 0, j)),
        out_shape=jax.ShapeDtypeStruct((2, n_batch, r_len, DN_W), F32),
        compiler_params=_cp(("parallel", "parallel")),
        name="deltanet_prep",
    )(p3, conv_w, _exchange_matrix())


def _tri_mats(tr):
    idx = np.arange(tr)
    same = (idx[:, None] // DN_CHUNK) == (idx[None, :] // DN_CHUNK)
    low = same & (idx[:, None] >= idx[None, :])
    return jnp.asarray(low, BF16), jnp.asarray(low.T, BF16), jnp.asarray(same, BF16)


def _dnintra_kernel(u_ref, la_ref, lb_ref, lat_ref, alr_ref, dtr_ref, alc_ref, dtc_ref, low_ref, upp_ref,
                    one_ref, a_ref, qk_ref, be_ref, eg_ref, ek_ref, gt_ref, *, tr):
    g_col = -jnp.exp(alr_ref[0]) * _softplus(la_ref[0, 0] + dtr_ref[0])
    beta = jax.nn.sigmoid(lb_ref[0, 0])
    g1, g2, g3 = _split3(g_col)
    gc_col = _dot(low_ref[...], g1) + (_dot(low_ref[...], g2) + _dot(low_ref[...], g3))
    gt_col = _dot(one_ref[...], g1) + (_dot(one_ref[...], g2) + _dot(one_ref[...], g3))
    be_ref[0, 0] = beta
    eg_ref[0, 0] = jnp.exp(gc_col)
    ek_ref[0, 0] = jnp.exp(gt_col - gc_col)
    gt_ref[0, 0] = jnp.exp(gt_col)
    g_row = -jnp.exp(alc_ref[0]) * _softplus(lat_ref[0, 0] + dtc_ref[0])
    r1, r2, r3 = _split3(g_row)
    gc_row = _dot(r1, upp_ref[...]) + (_dot(r2, upp_ref[...]) + _dot(r3, upp_ref[...]))

    ii = lax.broadcasted_iota(jnp.int32, (DN_CHUNK, DN_CHUNK), 0)
    jj = lax.broadcasted_iota(jnp.int32, (DN_CHUNK, DN_CHUNK), 1)
    scale = DN_HEAD_DIM ** -0.5
    nt = (((1,), (1,)), ((), ()))
    for c in range(tr // DN_CHUNK):
        rows = slice(c * DN_CHUNK, (c + 1) * DN_CHUNK)
        for hq in range(DN_QK_HEADS):
            q = u_ref[0, 0, rows, hq * DN_HEAD_DIM:(hq + 1) * DN_HEAD_DIM].astype(BF16)
            k = u_ref[0, 0, rows, DN_QK_W + hq * DN_HEAD_DIM:DN_QK_W + (hq + 1) * DN_HEAD_DIM].astype(BF16)
            kk = lax.dot_general(k, k, nt, preferred_element_type=F32)
            qk = lax.dot_general(q, k, nt, preferred_element_type=F32) * scale
            for h in range(hq * (DN_V_HEADS // DN_QK_HEADS), (hq + 1) * (DN_V_HEADS // DN_QK_HEADS)):
                diff = gc_col[rows, h:h + 1] - gc_row[h:h + 1, rows]
                dec = jnp.where(ii >= jj, jnp.exp(jnp.minimum(diff, 0.0)), 0.0)
                a_ref[0, 0, h, c] = jnp.where(ii > jj, kk * beta[rows, h:h + 1] * dec, 0.0)
                qk_ref[0, 0, h, c] = qk * dec


def _dnintra_call(u2, la, lb, lat, alog, dtb):
    n_dir, n_batch, r_len, _ = u2.shape
    tr = 4 * DN_CHUNK
    nc = r_len // DN_CHUNK
    low, upp, one = _tri_mats(tr)
    kern = functools.partial(_dnintra_kernel, tr=tr)
    pad = lambda v: jnp.pad(v, ((0, 0), (0, LANE - DN_V_HEADS))).reshape(n_dir, 1, LANE)
    col = lambda v: v.reshape(n_dir, DN_V_HEADS, 1)
    cm = lambda d, b, t: (0, 0)
    gspec = pl.BlockSpec((1, 1, tr, LANE), lambda d, b, t: (d, b, t, 0))
    mspec = pl.BlockSpec((1, 1, DN_V_HEADS, tr // DN_CHUNK, DN_CHUNK, DN_CHUNK), lambda d, b, t: (d, b, 0, t, 0, 0))
    gshape = jax.ShapeDtypeStruct((n_dir, n_batch, r_len, LANE), F32)
    mshape = jax.ShapeDtypeStruct((n_dir, n_batch, DN_V_HEADS, nc, DN_CHUNK, DN_CHUNK), F32)
    return pl.pallas_call(
        kern,
        grid=(n_dir, n_batch, r_len // tr),
        in_specs=[pl.BlockSpec((1, 1, tr, 2 * DN_QK_W), lambda d, b, t: (d, b, t, 0)),
                  gspec, gspec,
                  pl.BlockSpec((1, 1, DN_V_HEADS, tr), lambda d, b, t: (d, b, 0, t)),
                  pl.BlockSpec((1, 1, LANE), lambda d, b, t: (d, 0, 0)),
                  pl.BlockSpec((1, 1, LANE), lambda d, b, t: (d, 0, 0)),
                  pl.BlockSpec((1, DN_V_HEADS, 1), lambda d, b, t: (d, 0, 0)),
                  pl.BlockSpec((1, DN_V_HEADS, 1), lambda d, b, t: (d, 0, 0)),
                  pl.BlockSpec((tr, tr), cm), pl.BlockSpec((tr, tr), cm), pl.BlockSpec((tr, tr), cm)],
        out_specs=[mspec, mspec, gspec, gspec, gspec, gspec],
        out_shape=[mshape, mshape, gshape, gshape, gshape, gshape],
        compiler_params=_cp(("parallel", "parallel", "parallel")),
        name="deltanet_intra",
    )(u2, la, lb, lat, pad(alog), pad(dtb), col(alog), col(dtb), low, upp, one)


def _dnsolve_kernel(a_ref, o_ref, at_ref, tt_ref):
    c = DN_CHUNK
    for blk in range(c * c // LANE):
        at_ref[blk * LANE:(blk + 1) * LANE, :] = a_ref[:, blk * LANE:(blk + 1) * LANE].T
    tt_ref[...] = jnp.zeros_like(tt_ref)
    for i in range(c):
        nr = 8 * (i // 8 + 1)
        rr = lax.broadcasted_iota(jnp.int32, (nr, LANE), 0)
        acc = jnp.where(rr == i, 1.0, 0.0)

        def body(j, acc, i=i, nr=nr):
            a = at_ref[pl.ds(i * c + j, 1), :]
            return acc - a * tt_ref[pl.ds(pl.multiple_of(j * c, c), nr), :]
        if i > 0:
            acc = lax.fori_loop(0, i, body, acc, unroll=min(i, 8))
        tt_ref[i * c:i * c + nr, :] = acc
    for blk in range(c * c // LANE):
        o_ref[:, blk * LANE:(blk + 1) * LANE] = tt_ref[blk * LANE:(blk + 1) * LANE, :].T


def _dnsolve_call(a2):
    ni, cc = a2.shape
    assert ni % LANE == 0
    return pl.pallas_call(
        _dnsolve_kernel,
        grid=(ni // LANE,),
        in_specs=[pl.BlockSpec((LANE, cc), lambda i: (i, 0))],
        out_specs=pl.BlockSpec((LANE, cc), lambda i: (i, 0)),
        out_shape=jax.ShapeDtypeStruct((ni, cc), F32),
        scratch_shapes=[pltpu.VMEM((cc, LANE), F32), pltpu.VMEM((cc, LANE), F32)],
        compiler_params=_cp(("parallel",)),
        name="deltanet_solve",
    )(a2)


def _dnscan_kernel(u_ref, t_ref, qk_ref, be_ref, eg_ref, ek_ref, gt_ref, o_ref, *s_refs, n_dir, n_batch):
    @pl.when(pl.program_id(0) == 0)
    def _():
        for s_ref in s_refs:
            s_ref[...] = jnp.zeros_like(s_ref)

    scale = DN_HEAD_DIM ** -0.5
    rep = DN_V_HEADS // DN_QK_HEADS
    streams = [(d, b, h) for d in range(n_dir) for b in range(n_batch) for h in range(DN_V_HEADS)]

    def qkv(d, b, h):
        hq = h // rep
        q = u_ref[d, b, :, hq * DN_HEAD_DIM:(hq + 1) * DN_HEAD_DIM]
        k = u_ref[d, b, :, DN_QK_W + hq * DN_HEAD_DIM:DN_QK_W + (hq + 1) * DN_HEAD_DIM]
        v = u_ref[d, b, :, 2 * DN_QK_W + h * DN_HEAD_DIM:2 * DN_QK_W + (h + 1) * DN_HEAD_DIM]
        return q, k, v

    uws = []
    for d, b, h in streams:
        _, k, v = qkv(d, b, h)
        be = be_ref[d, b, :, h:h + 1]
        rhs = jnp.concatenate([v * be, (k * be) * eg_ref[d, b, :, h:h + 1]], axis=1).astype(BF16)
        uws.append(_dot(t_ref[d, b, h, 0].astype(BF16), rhs))
    wqs = []
    for i, (d, b, h) in enumerate(streams):
        q, _, _ = qkv(d, b, h)
        lhs = jnp.concatenate([uws[i][:, DN_HEAD_DIM:], q * (scale * eg_ref[d, b, :, h:h + 1])], axis=0)
        wqs.append(_dot(lhs.astype(BF16), s_refs[i][...].astype(BF16)))
    for i, (d, b, h) in enumerate(streams):
        _, k, _ = qkv(d, b, h)
        vnb = (uws[i][:, :DN_HEAD_DIM] - wqs[i][:DN_CHUNK]).astype(BF16)
        o_ref[d, b, :, h * DN_HEAD_DIM:(h + 1) * DN_HEAD_DIM] = (
            wqs[i][DN_CHUNK:] + _dot(qk_ref[d, b, h, 0].astype(BF16), vnb))
        kd = (k * ek_ref[d, b, :, h:h + 1]).astype(BF16)
        s_refs[i][...] = s_refs[i][...] * gt_ref[d, b, 0:1, h:h + 1] + lax.dot_general(
            kd, vnb, (((0,), (0,)), ((), ())), preferred_element_type=F32)


def _dnscan_call(u2, t6, qk6, be, eg, ek, gt, *, s_len):
    n_dir, n_batch, r_len, _ = u2.shape
    nc = r_len // DN_CHUNK
    ncl = s_len // DN_CHUNK

    def cidx(t):
        return jnp.where(t < nc - ncl, ncl + t, t - (nc - ncl))

    kern = functools.partial(_dnscan_kernel, n_dir=n_dir, n_batch=n_batch)
    gspec = pl.BlockSpec((n_dir, n_batch, DN_CHUNK, LANE), lambda t: (0, 0, cidx(t), 0))
    mspec = pl.BlockSpec((n_dir, n_batch, DN_V_HEADS, 1, DN_CHUNK, DN_CHUNK), lambda t: (0, 0, 0, cidx(t), 0, 0))
    return pl.pallas_call(
        kern,
        grid=(nc,),
        in_specs=[pl.BlockSpec((n_dir, n_batch, DN_CHUNK, DN_W), lambda t: (0, 0, cidx(t), 0)),
                  mspec, mspec, gspec, gspec, gspec, gspec],
        out_specs=pl.BlockSpec((n_dir, n_batch, DN_CHUNK, DN_V_W), lambda t: (0, 0, cidx(t), 0)),
        out_shape=jax.ShapeDtypeStruct((n_dir, n_batch, r_len, DN_V_W), F32),
        scratch_shapes=[pltpu.VMEM((DN_HEAD_DIM, DN_HEAD_DIM), F32)] * (n_dir * n_batch * DN_V_HEADS),
        compiler_params=_cp(("arbitrary",)),
        name="deltanet_scan",
    )(u2, t6, qk6, be, eg, ek, gt)


DN_OUT_COLS = 512


def _dnout_kernel(of_ref, ob_ref, z_ref, g_ref, jm_ref, o_ref):
    for h in range(DN_OUT_COLS // DN_HEAD_DIM):
        sl = slice(h * DN_HEAD_DIM, (h + 1) * DN_HEAD_DIM)
        o = of_ref[0, 0, :, sl] + _flip_rows(jm_ref[...], ob_ref[0, 0, :, sl])
        y = o * lax.rsqrt(jnp.mean(o * o, axis=-1, keepdims=True) + NORM_EPS) * g_ref[...]
        o_ref[0, :, sl] = (y * _silu(z_ref[0, :, sl])).astype(o_ref.dtype)


def _dnout_call(o2, p3, norm_g, *, s_len):
    _, n_batch, r_len, _ = o2.shape
    tr = FLIP_ROWS
    cw = DN_OUT_COLS
    ns, nc = s_len // tr, (r_len - s_len) // tr
    assert OFF_DZ % cw == 0

    def mirror(i):
        return jnp.where(i < ns, ns - 1 - i, 2 * ns + nc - 1 - i)

    return pl.pallas_call(
        _dnout_kernel,
        grid=(n_batch, r_len // tr, DN_V_W // cw),
        in_specs=[pl.BlockSpec((1, 1, tr, cw), lambda b, i, j: (0, b, i, j)),
                  pl.BlockSpec((1, 1, tr, cw), lambda b, i, j: (1, b, mirror(i), j)),
                  pl.BlockSpec((1, tr, cw), lambda b, i, j: (b, i, OFF_DZ // cw + j)),
                  pl.BlockSpec((1, DN_HEAD_DIM), lambda b, i, j: (0, 0)),
                  pl.BlockSpec((FLIP_ROWS, FLIP_ROWS), lambda b, i, j: (0, 0))],
        out_specs=pl.BlockSpec((1, tr, cw), lambda b, i, j: (b, i, j)),
        out_shape=jax.ShapeDtypeStruct((n_batch, r_len, DN_V_W), BF16),
        compiler_params=_cp(("parallel", "parallel", "parallel")),
        name="deltanet_out",
    )(o2, o2, p3, norm_g.reshape(1, DN_HEAD_DIM), _exchange_matrix())


def _seq_flip(a, s_len, axis):
    lat, ctx = jnp.split(a, [s_len], axis=axis)
    return jnp.concatenate([jnp.flip(lat, axis), jnp.flip(ctx, axis)], axis=axis)


def _deltanet_mixer(p3, conv_w, a_log, dt_bias, norm_g, *, s_len):
    n_batch, r_len, _ = p3.shape
    u2 = _dnprep_call(p3, conv_w, s_len=s_len)
    lg =p3[:, :, OFF_DL:OFF_DL + N_LOGITS].reshape(n_batch, r_len, 2, 2, DN_V_HEADS)

    def dirs(x):
        return jnp.stack([x[:, :, 0], _seq_flip(x[:, :, 1], s_len, 1)])

    la, lb = dirs(lg[:, :, 0]), dirs(lg[:, :, 1])
    padl = lambda x: jnp.pad(x, ((0, 0), (0, 0), (0, 0), (0, LANE - DN_V_HEADS)))
    a6, qk6, be, eg, ek, gt = _dnintra_call(u2, padl(la), padl(lb), jnp.swapaxes(la, 2, 3), a_log, dt_bias)
    t6 = _dnsolve_call(a6.reshape(-1, DN_CHUNK * DN_CHUNK)).reshape(a6.shape)
    o2 = _dnscan_call(u2, t6, qk6, be, eg, ek, gt, s_len=s_len)
    return _dnout_call(o2, p3, norm_g, s_len=s_len)


def _merge_kernel(ya_ref, yb_ref, yc_ref, wa_ref, wb_ref, wc_ref, ga_ref, gb_ref, gc_ref, o_ref,
                  wa_s, wb_s, wc_s):
    @pl.when(pl.program_id(1) == 0)
    def _():
        wa_s[...] = wa_ref[0].astype(BF16)
        wb_s[...] = wb_ref[0].astype(BF16)
        wc_s[...] = wc_ref[0].astype(BF16)

    m = (jax.nn.sigmoid(ga_ref[...]) * _dot(ya_ref[...], wa_s[...])
         + jax.nn.sigmoid(gb_ref[...]) * _dot(yb_ref[...], wb_s[...])
         + jax.nn.sigmoid(gc_ref[...]) * _dot(yc_ref[...], wc_s[...]))
    o_ref[...] = m.astype(o_ref.dtype)


def _merge_call(ya, yb, yc, w_pa, w_pb, w_pc, layer, p2, *, r_len):
    rt = ya.shape[0]
    d = w_pa.shape[2]
    tm = _pick(r_len, 1088, 16)
    tn = _pick(d, MERGE_ALIGN)
    assert OFF_MG % tn == 0
    yspec = lambda w: pl.BlockSpec((tm, w), lambda j, i: (i, 0))
    wspec = lambda w: pl.BlockSpec((1, w, tn), lambda j, i: (layer, 0, j))
    gspec = lambda br: pl.BlockSpec((tm, tn), lambda j, i, br=br: (i, (OFF_MG + br * d) // tn + j))
    return pl.pallas_call(
        _merge_kernel,
        grid=(d // tn, rt // tm),
        in_specs=[yspec(ATT_W), yspec(HY_W), yspec(DN_V_W), wspec(ATT_W), wspec(HY_W), wspec(DN_V_W),
                  gspec(0), gspec(1), gspec(2)],
        out_specs=pl.BlockSpec((tm, tn), lambda j, i: (i, j)),
        out_shape=jax.ShapeDtypeStruct((rt, d), BF16),
        scratch_shapes=[pltpu.VMEM((ATT_W, tn), BF16), pltpu.VMEM((HY_W, tn), BF16),
                        pltpu.VMEM((DN_V_W, tn), BF16)],
        compiler_params=_cp(("parallel", "arbitrary"), 56),
        name="branch_merge",
    )(ya, yb, yc, w_pa, w_pb, w_pc, p2, p2, p2)


def _outproj_kernel(m_ref, w_ref, x_ref, gate_ref, o_ref, w_s, *, tm, tiles_per_batch, s_len, n_batch):
    @pl.when(pl.program_id(1) == 0)
    def _():
        w_s[...] = w_ref[0].astype(BF16)

    is_ctx, gl, gc = _row_mods(gate_ref, pl.program_id(1), 0, tm, tm, tiles_per_batch, s_len, n_batch)
    o_ref[...] = x_ref[...] + jnp.where(is_ctx, gc, gl) * _dot(m_ref[...], w_s[...])


def _outproj_call(m, w_out, layer, xs2, mods, *, n_batch, r_len, s_len):
    rt, d = xs2.shape
    tm = _pick(r_len, 1088, 16)
    tn = _pick(d, 512)
    kern = functools.partial(_outproj_kernel, tm=tm, tiles_per_batch=r_len // tm, s_len=s_len, n_batch=n_batch)
    return pl.pallas_call(
        kern,
        grid=(d // tn, rt // tm),
        in_specs=[pl.BlockSpec((tm, d), lambda j, i: (i, 0)),
                  pl.BlockSpec((1, d, tn), lambda j, i: (layer, 0, j)),
                  pl.BlockSpec((tm, tn), lambda j, i: (i, j)),
                  pl.BlockSpec((8, tn), lambda j, i: (0, 2 * d // tn + j))],
        out_specs=pl.BlockSpec((tm, tn), lambda j, i: (i, j)),
        out_shape=jax.ShapeDtypeStruct((rt, d), F32),
        scratch_shapes=[pltpu.VMEM((d, tn), BF16)],
        compiler_params=_cp(("parallel", "arbitrary")),
        name="out_proj_residual",
    )(m, w_out, xs2, mods)


def _finalnorm_kernel(x_ref, g_ref, o_ref):
    x = x_ref[0]
    o_ref[0] = x * lax.rsqrt(jnp.mean(x * x, axis=-1, keepdims=True) + NORM_EPS) * g_ref[...]


def _finalnorm_call(xs, final_g, *, s_len):
    n_batch, _, d = xs.shape
    tr = _pick(s_len, 512, 8)
    return pl.pallas_call(
        _finalnorm_kernel,
        grid=(n_batch, s_len // tr),
        in_specs=[pl.BlockSpec((1, tr, d), lambda b, i: (b, i, 0)), pl.BlockSpec((1, d), lambda b, i: (0, 0))],
        out_specs=pl.BlockSpec((1, tr, d), lambda b, i: (b, i, 0)),
        out_shape=jax.ShapeDtypeStruct((n_batch, s_len, d), F32),
        compiler_params=_cp(("parallel", "parallel")),
        name="final_norm",
    )(xs, final_g.reshape(1, d))


def kernel(x, c, ctx, c_ctx, norm_g, w_mod, b_mod, w_in, q_norm_g, k_norm_g, hy_conv_w, hy_conv_b, hy_w1, hy_b1, hy_freq1, hy_w2, hy_b2, hy_freq2, hy_w3, hy_d, dn_conv_w, dn_a_log, dn_dt_bias, dn_norm_g, w_pa, w_pb, w_pc, w_out, final_g):
    n_batch, s_len, d = x.shape
    ctx_len = ctx.shape[1]
    r_len = s_len + ctx_len
    depth = w_in.shape[0]
    assert n_batch + 1 <= 8 and w_in.shape[2] == OFF_DL + N_LOGITS + N_BRANCH * d

    xs = jnp.concatenate([x, ctx], axis=1)
    cs = jnp.zeros((8, d), F32).at[:n_batch].set(c).at[n_batch].set(c_ctx)
    mods = _mod_call(cs, w_mod, b_mod)
    cos_t, sin_t = _rope_tables(s_len, ctx_len)
    w1p = jnp.pad(hy_w1, ((0, 0), (0, LANE - HY_EMB), (0, 0)))

    for layer in range(depth):
        need_ctx = layer < depth - 1
        xs2 = xs.reshape(n_batch * r_len, d)
        h = _modnorm_call(xs2, norm_g[layer], mods[layer], n_batch=n_batch, r_len=r_len, s_len=s_len)
        p2 = _inproj_call(h, w_in, layer, _inproj_tail(w_in[layer]))
        p3 = p2.reshape(n_batch, r_len, -1)

        ya = _attn_call(p3, cos_t, sin_t, q_norm_g[layer], k_norm_g[layer], s_len=s_len)

        filt = (w1p[layer], hy_b1[layer], hy_freq1[layer], hy_w2[layer], hy_b2[layer], hy_freq2[layer],
                hy_w3[layer])
        yb = _hyena_mixer(p3, 0, s_len, hy_conv_w[layer], hy_conv_b[layer], hy_d[layer], filt)
        if need_ctx:
            yb_c = _hyena_mixer(p3, s_len // ctx_len, ctx_len, hy_conv_w[layer], hy_conv_b[layer],
                                hy_d[layer], filt)
        else:
            yb_c = jnp.zeros((n_batch, ctx_len, HY_W), BF16)
        yb = jnp.concatenate([yb, yb_c], axis=1)

        yc = _deltanet_mixer(p3, dn_conv_w[layer], dn_a_log[layer], dn_dt_bias[layer], dn_norm_g[layer],
                             s_len=s_len)

        rt = n_batch * r_len
        m = _merge_call(ya.reshape(rt, ATT_W), yb.reshape(rt, HY_W), yc.reshape(rt, DN_V_W),
                        w_pa, w_pb, w_pc, layer, p2, r_len=r_len)
        xs = _outproj_call(m, w_out, layer, xs2, mods[layer],
                           n_batch=n_batch, r_len=r_len, s_len=s_len).reshape(n_batch, r_len, d)

    return _finalnorm_call(xs, final_g, s_len=s_len)
```

```python
import functools
import math

import jax
import jax.numpy as jnp
import numpy as np
from jax import lax
from jax.experimental import pallas as pl
from jax.experimental.pallas import tpu as pltpu

F32 = jnp.float32
BF16 = jnp.bfloat16

GRID_W = 64
NORM_EPS = 1e-6
N_BRANCH = 3

ATT_HEADS = 8
ATT_KV_HEADS = 2
HEAD_DIM = 128
ATT_GROUP = ATT_HEADS // ATT_KV_HEADS
ATT_W = ATT_HEADS * HEAD_DIM
ATT_KV_W = ATT_KV_HEADS * HEAD_DIM
AXIS_ROPE_DIM = HEAD_DIM // 2
ROPE_THETA = 10000.0

HY_W = 1024
HY_ORDER = 2
HY_BANDS = 16
HY_EMB = 1 + 2 * HY_BANDS
HY_FILTER_HIDDEN = 64
HY_DECAY_TARGET = 1e-2
HY_DECAY_FAST = 0.3
HY_DECAY_SLOW = 1.5

DN_QK_HEADS = 4
DN_V_HEADS = 8
DN_HEAD_DIM = 128
DN_QK_W = DN_QK_HEADS * DN_HEAD_DIM
DN_V_W = DN_V_HEADS * DN_HEAD_DIM
DN_CHUNK = 64
DN_W = 2 * DN_QK_W + DN_V_W

LANE = 128
MIB = 1024 * 1024

OFF_AQ = 0
OFF_AK = OFF_AQ + ATT_W
OFF_AV = OFF_AK + ATT_KV_W
OFF_AG = OFF_AV + ATT_KV_W
OFF_HV = OFF_AG + ATT_W
OFF_HX1 = OFF_HV + HY_W
OFF_HX2 = OFF_HX1 + HY_W
OFF_HG = OFF_HX2 + HY_W
OFF_DQ = OFF_HG + HY_W
OFF_DV = OFF_DQ + 2 * DN_QK_W
OFF_DZ = OFF_DV + DN_V_W
OFF_DL = OFF_DZ + DN_V_W
N_LOGITS = 4 * DN_V_HEADS
MERGE_ALIGN = 512
OFF_MG = -(-(OFF_DL + N_LOGITS) // MERGE_ALIGN) * MERGE_ALIGN


def _pick(n, cap, mult=LANE):
    best = None
    for t in range(mult, min(n, cap) + 1, mult):
        if n % t == 0:
            best = t
    assert best is not None, (n, cap, mult)
    return best


def _cp(sem, vmem_mib=48):
    return pltpu.CompilerParams(dimension_semantics=sem, vmem_limit_bytes=vmem_mib * MIB)


def _silu(x):
    return x * jax.nn.sigmoid(x)


def _split3(x):
    x1 = x.astype(BF16)
    r = x - x1.astype(F32)
    x2 = r.astype(BF16)
    x3 = (r - x2.astype(F32)).astype(BF16)
    return x1, x2, x3


def _dot(a, b):
    return jnp.dot(a, b, preferred_element_type=F32)


def _dot_hp(a, b):
    a1 = a.astype(BF16)
    a2 = (a - a1.astype(F32)).astype(BF16)
    b1 = b.astype(BF16)
    b2 = (b - b1.astype(F32)).astype(BF16)
    return _dot(a1, b1) + (_dot(a1, b2) + _dot(a2, b1))


def _mod_kernel(cs_ref, w_ref, b_ref, o_ref):
    cs = cs_ref[...]
    o_ref[0] = _dot(_silu(cs).astype(BF16), w_ref[0].astype(BF16)) + b_ref[0]


def _mod_call(cs, w_mod, b_mod):
    depth, d, d3 = w_mod.shape
    tn = _pick(d3, 512)
    return pl.pallas_call(
        _mod_kernel,
        grid=(depth, d3 // tn),
        in_specs=[pl.BlockSpec((8, d), lambda l, j: (0, 0)),
                  pl.BlockSpec((1, d, tn), lambda l, j: (l, 0, j)),
                  pl.BlockSpec((1, 1, tn), lambda l, j: (l, 0, j))],
        out_specs=pl.BlockSpec((1, 8, tn), lambda l, j: (l, 0, j)),
        out_shape=jax.ShapeDtypeStruct((depth, 8, d3), F32),
        compiler_params=_cp(("parallel", "parallel")),
        name="adaln_mod",
    )(cs, w_mod, b_mod.reshape(depth, 1, d3))


def _row_mods(mod_ref, i, r0, rows, tm, tiles_per_batch, s_len, n_batch):
    b = i // tiles_per_batch
    row = (i % tiles_per_batch) * tm + r0 + lax.broadcasted_iota(jnp.int32, (rows, 1), 0)
    return row >= s_len, mod_ref[pl.ds(b, 1), :], mod_ref[n_batch:n_batch + 1, :]


NORM_ROWS = 64


def _modnorm_kernel(x_ref, g_ref, mod_ref, h_ref, *, tm, tiles_per_batch, s_len, n_batch, d):
    i = pl.program_id(0)

    def body(c, carry):
        r0 = pl.multiple_of(c * NORM_ROWS, NORM_ROWS)
        x = x_ref[pl.ds(r0, NORM_ROWS), :]
        y = x * lax.rsqrt(jnp.mean(x * x, axis=-1, keepdims=True) + NORM_EPS) * g_ref[...]
        is_ctx, ml, mc = _row_mods(mod_ref, i, r0, NORM_ROWS, tm, tiles_per_batch, s_len, n_batch)
        shift = jnp.where(is_ctx, mc[:, :d], ml[:, :d])
        scale = jnp.where(is_ctx, mc[:, d:2 * d], ml[:, d:2 * d])
        h_ref[pl.ds(r0, NORM_ROWS), :] = (y * (1.0 + scale) + shift).astype(BF16)
        return carry
    lax.fori_loop(0, tm // NORM_ROWS, body, 0)


def _modnorm_call(xs2, norm_g, mods, *, n_batch, r_len, s_len):
    rt, d = xs2.shape
    tm = _pick(r_len, 576, NORM_ROWS)
    kern = functools.partial(_modnorm_kernel, tm=tm, tiles_per_batch=r_len // tm, s_len=s_len,
                             n_batch=n_batch, d=d)
    return pl.pallas_call(
        kern,
        grid=(rt // tm,),
        in_specs=[pl.BlockSpec((tm, d), lambda i: (i, 0)),
                  pl.BlockSpec((1, d), lambda i: (0, 0)),
                  pl.BlockSpec((8, 3 * d), lambda i: (0, 0))],
        out_specs=pl.BlockSpec((tm, d), lambda i: (i, 0)),
        out_shape=jax.ShapeDtypeStruct((rt, d), BF16),
        compiler_params=_cp(("parallel",)),
        name="mod_norm",
    )(xs2, norm_g.reshape(1, d), mods)


def _inproj_kernel(h_ref, w_ref, o_ref):
    o_ref[...] = _dot(h_ref[...], w_ref[...])


def _pad_w_in(w_in_l):
    d = w_in_l.shape[0]
    cut = OFF_DL + N_LOGITS
    pad = jnp.zeros((d, OFF_MG - cut), w_in_l.dtype)
    return jnp.concatenate([w_in_l[:, :cut], pad, w_in_l[:, cut:]], axis=1).astype(BF16)


def _inproj_call(h, w_bf):
    rt, d = h.shape
    nw = w_bf.shape[1]
    tm = _pick(rt, 2176, 16)
    tn = _pick(nw, 1024)
    return pl.pallas_call(
        _inproj_kernel,
        grid=(rt // tm, nw // tn),
        in_specs=[pl.BlockSpec((tm, d), lambda i, j: (i, 0)),
                  pl.BlockSpec((d, tn), lambda i, j: (0, j))],
        out_specs=pl.BlockSpec((tm, tn), lambda i, j: (i, j)),
        out_shape=jax.ShapeDtypeStruct((rt, nw), F32),
        compiler_params=_cp(("parallel", "arbitrary"), 56),
        name="in_proj",
    )(h, w_bf)


def _rope_tables(s_len, ctx_len):
    m = AXIS_ROPE_DIM // 2
    inv_freq = ROPE_THETA ** (-np.arange(0, AXIS_ROPE_DIM, 2, dtype=np.float64) / AXIS_ROPE_DIM)
    t = np.arange(s_len)
    ang_r = (t // GRID_W)[:, None] * inv_freq
    ang_c = (t % GRID_W)[:, None] * inv_freq
    cos = np.concatenate([np.cos(ang_r)] * 2 + [np.cos(ang_c)] * 2, axis=-1)
    sin = np.concatenate([-np.sin(ang_r), np.sin(ang_r), -np.sin(ang_c), np.sin(ang_c)], axis=-1)
    assert cos.shape[1] == 4 * m == HEAD_DIM
    cos = np.concatenate([cos, np.ones((ctx_len, HEAD_DIM))], axis=0)
    sin = np.concatenate([sin, np.zeros((ctx_len, HEAD_DIM))], axis=0)
    return jnp.asarray(cos, F32), jnp.asarray(sin, F32)


def _norm_rope(x, g, cs, sn):
    y = x * lax.rsqrt(jnp.mean(x * x, axis=-1, keepdims=True) + NORM_EPS) * g
    lane = lax.broadcasted_iota(jnp.int32, (1, HEAD_DIM), 1)
    first = (lane % AXIS_ROPE_DIM) < (AXIS_ROPE_DIM // 2)
    q = AXIS_ROPE_DIM // 2
    partner = jnp.where(first, pltpu.roll(y, HEAD_DIM - q, 1), pltpu.roll(y, q, 1))
    return y * cs + partner * sn


def _attn_kernel(q_ref, k_ref, v_ref, gt_ref, cos_ref, sin_ref, qg_ref, kg_ref, o_ref, ks_ref, vs_ref,
                 *, tq, r_len, s_len, kv_chunk):
    i = pl.program_id(2)

    @pl.when(i == 0)
    def _():
        def body(c, carry):
            r0 = pl.multiple_of(c * kv_chunk, kv_chunk)
            kk = _norm_rope(k_ref[0, pl.ds(r0, kv_chunk), :], kg_ref[...],
                            cos_ref[pl.ds(r0, kv_chunk), :], sin_ref[pl.ds(r0, kv_chunk), :])
            ks_ref[pl.ds(r0, kv_chunk), :] = kk.astype(BF16)
            vs_ref[pl.ds(r0, kv_chunk), :] = v_ref[0, pl.ds(r0, kv_chunk), :].astype(BF16)
            return carry
        lax.fori_loop(0, r_len // kv_chunk, body, 0)

    r0 = pl.multiple_of(i * tq, tq)
    cs = cos_ref[pl.ds(r0, tq), :]
    sn = sin_ref[pl.ds(r0, tq), :]
    scale = HEAD_DIM ** -0.5

    def heads(k_lo, k_hi):
        for g in range(ATT_GROUP):
            sl = slice(g * HEAD_DIM, (g + 1) * HEAD_DIM)
            qh = (_norm_rope(q_ref[0, :, sl], qg_ref[...], cs, sn) * scale).astype(BF16)
            s = lax.dot_general(qh, ks_ref[k_lo:k_hi, :], (((1,), (1,)), ((), ())),
                                preferred_element_type=F32)
            e = jnp.exp(s - jnp.max(s, axis=-1, keepdims=True))
            den = jnp.sum(e, axis=-1, keepdims=True)
            o = _dot(e.astype(BF16), vs_ref[k_lo:k_hi, :]) / den
            o_ref[0, :, sl] = (o * _silu(gt_ref[0, :, sl])).astype(o_ref.dtype)

    @pl.when(i < s_len // tq)
    def _():
        heads(0, r_len)

    @pl.when(i >= s_len // tq)
    def _():
        heads(s_len, r_len)


def _attn_call(p3, cos_t, sin_t, q_g, k_g, *, s_len):
    n_batch, r_len, _ = p3.shape
    ctx_len = r_len - s_len
    tq = 256 if (ctx_len % 256 == 0 and s_len % 256 == 0) else 128
    assert ctx_len % tq == 0 and s_len % tq == 0
    gw = ATT_GROUP * HEAD_DIM
    kern = functools.partial(_attn_kernel, tq=tq, r_len=r_len, s_len=s_len, kv_chunk=tq)
    return pl.pallas_call(
        kern,
        grid=(n_batch, ATT_KV_HEADS, r_len // tq),
        in_specs=[pl.BlockSpec((1, tq, gw), lambda b, h, i: (b, i, OFF_AQ // gw + h)),
                  pl.BlockSpec((1, r_len, HEAD_DIM), lambda b, h, i: (b, 0, OFF_AK // HEAD_DIM + h)),
                  pl.BlockSpec((1, r_len, HEAD_DIM), lambda b, h, i: (b, 0, OFF_AV // HEAD_DIM + h)),
                  pl.BlockSpec((1, tq, gw), lambda b, h, i: (b, i, OFF_AG // gw + h)),
                  pl.BlockSpec((r_len, HEAD_DIM), lambda b, h, i: (0, 0)),
                  pl.BlockSpec((r_len, HEAD_DIM), lambda b, h, i: (0, 0)),
                  pl.BlockSpec((1, HEAD_DIM), lambda b, h, i: (0, 0)),
                  pl.BlockSpec((1, HEAD_DIM), lambda b, h, i: (0, 0))],
        out_specs=pl.BlockSpec((1, tq, gw), lambda b, h, i: (b, i, h)),
        out_shape=jax.ShapeDtypeStruct((n_batch, r_len, ATT_W), BF16),
        scratch_shapes=[pltpu.VMEM((r_len, HEAD_DIM), BF16), pltpu.VMEM((r_len, HEAD_DIM), BF16)],
        compiler_params=_cp(("parallel", "parallel", "arbitrary")),
        name="gqa_attention",
    )(p3, p3, p3, p3, cos_t, sin_t, q_g.reshape(1, HEAD_DIM), k_g.reshape(1, HEAD_DIM))


def _conv3(x, w, first, last):
    n = x.shape[0]
    prev = jnp.where(first, 0.0, pltpu.roll(x, 1, 0))
    nxt = jnp.where(last, 0.0, pltpu.roll(x, n - 1, 0))
    return prev * w[0:1, :] + x * w[1:2, :] + nxt * w[2:3, :]


FFT_UNROLL = 16


def _fft_split(n):
    if n <= 512:
        return 1, n
    n2 = 64
    return n // n2, n2


@functools.lru_cache(maxsize=None)
def _fft_mats(n, n1, n2):
    half = n2 // 2
    a = np.arange(n1)[:, None, None]
    k2 = np.arange(n2)[None, :, None]

    def g(bs):
        ph = (k2 * (a + n1 * bs[None, None, :])) % n
        ang = -2.0 * np.pi * ph / n
        return np.cos(ang), np.sin(ang)

    gre, gim = g(np.arange(half))
    gal = np.concatenate([gre, gim], axis=1)
    gar = np.concatenate([-gim, gre], axis=1)
    hre = np.swapaxes(gre, 1, 2) / n
    him = -np.swapaxes(gim, 1, 2) / n
    hal = np.concatenate([hre, him], axis=1)
    har = np.concatenate([-him, hre], axis=1)
    fre_f, fim_f = g(np.arange(n2))
    gf = np.concatenate([fre_f, fim_f], axis=1)
    k1 = np.arange(n1)
    ang1 = -2.0 * np.pi * ((k1[:, None] * k1[None, :]) % n1) / n1
    f1re, f1im = np.cos(ang1), np.sin(ang1)
    fbl = np.concatenate([f1re, f1im], axis=0)
    fbr = np.concatenate([-f1im, f1re], axis=0)
    fbil = np.concatenate([f1re, -f1im], axis=0)
    fbir = np.concatenate([f1im, f1re], axis=0)
    cast = lambda m: jnp.asarray(m, BF16)
    return dict(ga=cast(np.concatenate([gal, gar], axis=2)), ha=cast(np.concatenate([hal, har], axis=2)),
                gf=cast(gf), fb=cast(np.concatenate([fbl, fbr], axis=1)),
                fbi=cast(np.concatenate([fbil, fbir], axis=1)))


@functools.lru_cache(maxsize=None)
def _hy_tables(n_tok):
    pos = np.arange(n_tok, dtype=np.float64)
    t = pos / max(n_tok - 1, 1)
    bands = np.linspace(1e-4, HY_BANDS - 1, HY_BANDS)
    ang = (2.0 * np.pi / n_tok) * pos[:, None] * bands
    z = np.concatenate([t[:, None], np.cos(ang), np.sin(ang)], axis=-1)
    zrev = np.zeros_like(z)
    zrev[1:] = z[:0:-1]
    ztab = np.zeros((2 * n_tok, LANE))
    ztab[:, :HY_EMB] = np.concatenate([z, zrev], axis=0)
    deltas = np.abs(np.linspace(math.log(HY_DECAY_TARGET) / HY_DECAY_SLOW,
                                math.log(HY_DECAY_TARGET) / HY_DECAY_FAST, HY_W))
    return jnp.asarray(ztab, F32), jnp.asarray(np.tile(deltas, HY_ORDER)[None, :], F32)


SLAB_PAD = 8


def _hyfilt_kernel(z_ref, w1_ref, b1_ref, f1_ref, w2_ref, b2_ref, f2_ref, w3_ref, dl_ref, k_ref, s_ref,
                   *, n_tok, tr, slab_rows):
    i = pl.program_id(0)
    z = z_ref[...]
    h = jnp.sin(f1_ref[...] * (_dot_hp(z, w1_ref[...]) + b1_ref[...]))
    h = jnp.sin(f2_ref[...] * (_dot_hp(h, w2_ref[...]) + b2_ref[...]))
    h = _dot(h.astype(BF16), w3_ref[...].astype(BF16)) * jnp.exp(-z[:, 0:1] * dl_ref[...])
    row = i * tr + lax.broadcasted_iota(jnp.int32, (tr, 1), 0)
    h = jnp.where(row == n_tok, 0.0, h)
    if slab_rows is None:
        k_ref[...] = h
    else:
        step = slab_rows + SLAB_PAD
        for s in range(tr // slab_rows):
            k_ref[s * step:s * step + slab_rows, :] = h[s * slab_rows:(s + 1) * slab_rows]
            k_ref[s * step + slab_rows:(s + 1) * step, :] = jnp.zeros((SLAB_PAD, h.shape[1]), F32)

    @pl.when(i == 0)
    def _():
        s_ref[...] = jnp.zeros_like(s_ref)

    s_ref[...] += jnp.sum(jnp.abs(h), axis=0, keepdims=True)


def _hyfilt_call(n_tok, w1p, b1, f1, w2, b2, f2, w3):
    ztab, dl = _hy_tables(n_tok)
    n = 2 * n_tok
    tr = _pick(n_tok, 512, 8)
    ow = HY_ORDER * HY_W
    hid = HY_FILTER_HIDDEN
    n1, _ = _fft_split(n)
    slab_rows = n1 if n1 > 1 else None
    assert slab_rows is None or tr % slab_rows == 0
    out_tr = tr if slab_rows is None else (tr // slab_rows) * (slab_rows + SLAB_PAD)
    kern = functools.partial(_hyfilt_kernel, n_tok=n_tok, tr=tr, slab_rows=slab_rows)
    c2 = lambda i: (0, 0)
    return pl.pallas_call(
        kern,
        grid=(n // tr,),
        in_specs=[pl.BlockSpec((tr, LANE), lambda i: (i, 0)),
                  pl.BlockSpec((LANE, hid), c2), pl.BlockSpec((1, hid), c2), pl.BlockSpec((1, hid), c2),
                  pl.BlockSpec((hid, hid), c2), pl.BlockSpec((1, hid), c2), pl.BlockSpec((1, hid), c2),
                  pl.BlockSpec((hid, ow), lambda i: (0, i // (n_tok // tr))),
                  pl.BlockSpec((1, ow), c2)],
        out_specs=[pl.BlockSpec((out_tr, ow), lambda i: (i, 0)), pl.BlockSpec((1, ow), c2)],
        out_shape=[jax.ShapeDtypeStruct((n // tr * out_tr, ow), F32), jax.ShapeDtypeStruct((1, ow), F32)],
        compiler_params=_cp(("arbitrary",)),
        name="hyena_filter",
    )(ztab, w1p, b1.reshape(1, hid), f1.reshape(1, hid), w2, b2.reshape(1, hid), f2.reshape(1, hid), w3, dl)


FFT_GROUP = 8


def _filtfft_kernel(k_ref, s_ref, gf_ref, fb_ref, o_ref, z_ref, *, n, n1, n2):
    inv = 1.0 / (s_ref[...] + 1e-6)
    if n1 == 1:
        o_ref[...] = _dot(gf_ref[0], k_ref[...].astype(BF16)) * inv
        return
    slab = 2 * n1
    zs, ks = slab + SLAB_PAD, n1 + SLAB_PAD

    def stage_a(a, carry):
        out = _dot(gf_ref[a], k_ref[pl.ds(a, n2, stride=ks), :].astype(BF16))
        z_ref[pl.ds(a, n2, stride=zs), :] = out[:n2]
        z_ref[pl.ds(n1 + a, n2, stride=zs), :] = out[n2:]
        return carry
    lax.fori_loop(0, n1, stage_a, 0, unroll=FFT_UNROLL)

    def stage_b(g, carry):
        for u in range(FFT_GROUP):
            k2 = g * FFT_GROUP + u
            z = z_ref[pl.ds(pl.multiple_of(k2 * zs, 8), slab), :].astype(BF16)
            o_ref[pl.ds(pl.multiple_of(k2 * slab, slab), slab), :] = _dot(fb_ref[...], z) * inv
        return carry
    lax.fori_loop(0, n2 // FFT_GROUP, stage_b, 0)


def _filtfft_call(k_un, asum, n_tok):
    k_rows, ow = k_un.shape
    n = 2 * n_tok
    n1, n2 = _fft_split(n)
    assert k_rows == (n if n1 == 1 else n2 * (n1 + SLAB_PAD))
    m = _fft_mats(n, n1, n2)
    cb = LANE
    kern = functools.partial(_filtfft_kernel, n=n, n1=n1, n2=n2)
    return pl.pallas_call(
        kern,
        grid=(ow // cb,),
        in_specs=[pl.BlockSpec((k_rows, cb), lambda j: (0, j)),
                  pl.BlockSpec((1, cb), lambda j: (0, j)),
                  pl.BlockSpec(m["gf"].shape, lambda j: (0, 0, 0)),
                  pl.BlockSpec(m["fb"].shape, lambda j: (0, 0))],
        out_specs=pl.BlockSpec((2 * n, cb), lambda j: (0, j)),
        out_shape=jax.ShapeDtypeStruct((2 * n, ow), F32),
        scratch_shapes=[pltpu.VMEM((n2 * (2 * n1 + SLAB_PAD) if n1 > 1 else 8, cb), F32)],
        compiler_params=_cp(("parallel",)),
        name="hyena_filter_fft",
    )(k_un, asum, m["gf"], m["fb"])


def _hyena_kernel(*refs, n_tok, n1, n2, conv_a, has_gate):
    it = iter(refs)
    a_ref, m_ref = next(it), next(it)
    g_ref = next(it) if has_gate else None
    if conv_a:
        cwa_ref, cba_ref = next(it), next(it)
    cwm_ref, cbm_ref, d_ref, ks_ref = next(it), next(it), next(it), next(it)
    ga_ref, ha_ref = next(it), next(it)
    if n1 > 1:
        fb_ref, fbi_ref = next(it), next(it)
    o_ref, ac_ref, z_ref = next(it), next(it), next(it)

    n = 2 * n_tok
    half = n2 // 2
    row = lax.broadcasted_iota(jnp.int32, (n_tok, 1), 0)
    first, last = row == 0, row == n_tok - 1
    sr = n1 if n1 > 1 else n_tok
    astep = sr + SLAB_PAD if n1 > 1 else sr
    for b in range(2):
        a = a_ref[b]
        if conv_a:
            a = _conv3(a, cwa_ref[...], first, last) + cba_ref[...]
        for s in range(n_tok // sr):
            ac_ref[b, s * astep:s * astep + sr, :] = a[s * sr:(s + 1) * sr]

    def spectrum_mul(x, k, h):
        xre, xim, kre, kim = x[:h], x[h:], k[:h], k[h:]
        return jnp.concatenate([xre * kre - xim * kim, xre * kim + xim * kre], axis=0).astype(BF16)

    if n1 == 1:
        x = _dot(ga_ref[0], jnp.concatenate([ac_ref[0], ac_ref[1]], axis=0).astype(BF16))
        y = _dot(ha_ref[0], spectrum_mul(x, ks_ref[...], n))
        for b in range(2):
            ac_ref[b] = y[b * n_tok:(b + 1) * n_tok] + ac_ref[b] * d_ref[0]
    else:
        slab = 2 * n1
        zs = slab + SLAB_PAD

        def stage_a(a, carry):
            rows = jnp.concatenate([ac_ref[0, pl.ds(a, half, stride=astep), :],
                                    ac_ref[1, pl.ds(a, half, stride=astep), :]], axis=0).astype(BF16)
            out = _dot(ga_ref[a], rows)
            z_ref[pl.ds(a, n2, stride=zs), :] = out[:n2]
            z_ref[pl.ds(n1 + a, n2, stride=zs), :] = out[n2:]
            return carry
        lax.fori_loop(0, n1, stage_a, 0, unroll=FFT_UNROLL)

        def stage_b(g, carry):
            k2s = [g * FFT_GROUP + u for u in range(FFT_GROUP)]
            zrows = [pl.ds(pl.multiple_of(k2 * zs, 8), slab) for k2 in k2s]
            xs = [_dot(fb_ref[...], z_ref[zr, :].astype(BF16)) for zr in zrows]
            ys = [spectrum_mul(x, ks_ref[pl.ds(pl.multiple_of(k2 * slab, slab), slab), :], n1)
                  for x, k2 in zip(xs, k2s)]
            for y, zr in zip(ys, zrows):
                z_ref[zr, :] = _dot(fbi_ref[...], y)
            return carry
        lax.fori_loop(0, n2 // FFT_GROUP, stage_b, 0)

        def stage_a_inv(a, carry):
            rows = jnp.concatenate([z_ref[pl.ds(a, n2, stride=zs), :],
                                    z_ref[pl.ds(n1 + a, n2, stride=zs), :]], axis=0).astype(BF16)
            y = _dot(ha_ref[a], rows)
            for b in range(2):
                cur = ac_ref[b, pl.ds(a, half, stride=astep), :]
                ac_ref[b, pl.ds(a, half, stride=astep), :] = y[b * half:(b + 1) * half] + cur * d_ref[0]
            return carry
        lax.fori_loop(0, n1, stage_a_inv, 0, unroll=FFT_UNROLL)

    for b in range(2):
        mc = _conv3(m_ref[b], cwm_ref[...], first, last) + cbm_ref[...]
        for s in range(n_tok // sr):
            rows = slice(s * sr, (s + 1) * sr)
            out = mc[rows] * ac_ref[b, s * astep:s * astep + sr, :]
            if has_gate:
                out = out * _silu(g_ref[b, rows, :])
            o_ref[b, rows, :] = out.astype(o_ref.dtype)


def _hyena_call(a_arr, a_blk, m_arr, m_blk, gate, conv_a, conv_m, d_row, spec, spec_blk, *, n_tok, out_dtype):
    n_batch = a_arr.shape[0]
    assert n_batch % 2 == 0
    n = 2 * n_tok
    n1, n2 = _fft_split(n)
    m = _fft_mats(n, n1, n2)
    cb = LANE
    one = pl.Buffered(1)
    tile_mode, spec_mode = (one, pl.Buffered(2)) if gate is not None else (pl.Buffered(2), one)

    def tile(blk):
        return pl.BlockSpec((2, n_tok, cb), lambda p, j, blk=blk: (p, blk[0], blk[1] + j), pipeline_mode=tile_mode)

    def convspec(c0):
        return [pl.BlockSpec((3, cb), lambda p, j, c0=c0: (0, c0 + j)),
                pl.BlockSpec((1, cb), lambda p, j, c0=c0: (0, c0 + j))]

    ins, specs = [a_arr, m_arr], [tile(a_blk), tile(m_blk)]
    if gate is not None:
        ins.append(gate[0])
        specs.append(tile(gate[1]))
    if conv_a is not None:
        ins += [conv_a[0], conv_a[1]]
        specs += convspec(conv_a[2])
    ins += [conv_m[0], conv_m[1], d_row, spec]
    specs += convspec(conv_m[2])
    specs += [pl.BlockSpec((1, 1, cb), lambda p, j: (0, 0, j)),
              pl.BlockSpec((2 * n, cb), lambda p, j, s0=spec_blk: (0, s0 + j), pipeline_mode=spec_mode)]
    names = ["ga", "ha"] + (["fb", "fbi"] if n1 > 1 else [])
    for nm in names:
        ins.append(m[nm])
        specs.append(pl.BlockSpec(m[nm].shape, lambda p, j, nd=m[nm].ndim: (0,) * nd, pipeline_mode=one))
    kern = functools.partial(_hyena_kernel, n_tok=n_tok, n1=n1, n2=n2, conv_a=conv_a is not None,
                             has_gate=gate is not None)
    return pl.pallas_call(
        kern,
        grid=(n_batch // 2, HY_W // cb),
        in_specs=specs,
        out_specs=pl.BlockSpec((2, n_tok, cb), lambda p, j: (p, 0, j)),
        out_shape=jax.ShapeDtypeStruct((n_batch, n_tok, HY_W), out_dtype),
        scratch_shapes=[pltpu.VMEM((2, n_tok // n1 * (n1 + SLAB_PAD) if n1 > 1 else n_tok, cb), F32),
                        pltpu.VMEM((n2 * (2 * n1 + SLAB_PAD) if n1 > 1 else 8, cb), F32)],
        compiler_params=_cp(("parallel", "arbitrary"), 56),
        name="hyena_conv",
    )(*ins)


def _hyena_mixer(p3, row_blk, n_tok, conv_w, conv_b, hy_d, filt):
    k_un, asum = _hyfilt_call(n_tok, *filt)
    spec = _filtfft_call(k_un, asum, n_tok)
    cbias = conv_b.reshape(1, -1)
    cblk = HY_W // LANE
    d3 = hy_d.reshape(HY_ORDER, 1, HY_W)
    z = _hyena_call(p3, (row_blk, OFF_HV // LANE), p3, (row_blk, OFF_HX1 // LANE), None,
                    (conv_w, cbias, 0), (conv_w, cbias, cblk), d3[0:1], spec, 0,
                    n_tok=n_tok, out_dtype=F32)
    return _hyena_call(z, (0, 0), p3, (row_blk, OFF_HX2 // LANE), (p3, (row_blk, OFF_HG // LANE)),
                       None, (conv_w, cbias, 2 * cblk), d3[1:2], spec, cblk,
                       n_tok=n_tok, out_dtype=BF16)


def _softplus(x):
    return jnp.maximum(x, 0.0) + jnp.log1p(jnp.exp(-jnp.abs(x)))


FLIP_ROWS = 256


def _exchange_matrix():
    return jnp.asarray(np.eye(FLIP_ROWS)[::-1], BF16)


def _flip_rows(jm, x):
    x1, x2, x3 = _split3(x)
    return _dot(jm, x1) + (_dot(jm, x2) + _dot(jm, x3))


def _dnprep_kernel(x_ref, w_ref, jm_ref, o_ref, *, r_len, s_len):
    j = pl.program_id(1)
    row = lax.broadcasted_iota(jnp.int32, (r_len, 1), 0)
    first = (row == 0) | (row == s_len)
    last = (row == s_len - 1) | (row == r_len - 1)
    u = _silu(_conv3(x_ref[0], w_ref[...], first, last))
    nrm = u * lax.rsqrt(jnp.sum(u * u, axis=-1, keepdims=True) + 1e-6)
    o_ref[0, 0] = jnp.where(j < 2 * DN_QK_HEADS, nrm, u)
    for seg0, seg_len in ((0, s_len), (s_len, r_len - s_len)):
        nt = seg_len // FLIP_ROWS
        for t in range(nt):
            src = seg0 + t * FLIP_ROWS
            dst = seg0 + (nt - 1 - t) * FLIP_ROWS
            o_ref[1, 0, dst:dst + FLIP_ROWS, :] = _flip_rows(jm_ref[...], o_ref[0, 0, src:src + FLIP_ROWS, :])


def _dnprep_call(p3, conv_w, *, s_len):
    n_batch, r_len, _ = p3.shape
    assert s_len % FLIP_ROWS == 0 and (r_len - s_len) % FLIP_ROWS == 0
    kern = functools.partial(_dnprep_kernel, r_len=r_len, s_len=s_len)
    return pl.pallas_call(
        kern,
        grid=(n_batch, DN_W // LANE),
        in_specs=[pl.BlockSpec((1, r_len, LANE), lambda b, j: (b, 0, OFF_DQ // LANE + j)),
                  pl.BlockSpec((3, LANE), lambda b, j: (0, j)),
                  pl.BlockSpec((FLIP_ROWS, FLIP_ROWS), lambda b, j: (0, 0))],
        out_specs=pl.BlockSpec((2, 1, r_len, LANE), lambda b, j: (0, b, 0, j)),
        out_shape=jax.ShapeDtypeStruct((2, n_batch, r_len, DN_W), F32),
        compiler_params=_cp(("parallel", "parallel")),
        name="deltanet_prep",
    )(p3, conv_w, _exchange_matrix())


def _tri_mats(tr):
    idx = np.arange(tr)
    same = (idx[:, None] // DN_CHUNK) == (idx[None, :] // DN_CHUNK)
    low = same & (idx[:, None] >= idx[None, :])
    return jnp.asarray(low, BF16), jnp.asarray(low.T, BF16), jnp.asarray(same, BF16)


def _dnintra_kernel(u_ref, la_ref, lb_ref, lat_ref, alr_ref, dtr_ref, alc_ref, dtc_ref, low_ref, upp_ref,
                    one_ref, a_ref, qk_ref, be_ref, eg_ref, ek_ref, gt_ref, *, tr):
    g_col = -jnp.exp(alr_ref[0]) * _softplus(la_ref[0, 0] + dtr_ref[0])
    beta = jax.nn.sigmoid(lb_ref[0, 0])
    g1, g2, g3 = _split3(g_col)
    gc_col = _dot(low_ref[...], g1) + (_dot(low_ref[...], g2) + _dot(low_ref[...], g3))
    gt_col = _dot(one_ref[...], g1) + (_dot(one_ref[...], g2) + _dot(one_ref[...], g3))
    be_ref[0, 0] = beta
    eg_ref[0, 0] = jnp.exp(gc_col)
    ek_ref[0, 0] = jnp.exp(gt_col - gc_col)
    gt_ref[0, 0] = jnp.exp(gt_col)
    g_row = -jnp.exp(alc_ref[0]) * _softplus(lat_ref[0, 0] + dtc_ref[0])
    r1, r2, r3 = _split3(g_row)
    gc_row = _dot(r1, upp_ref[...]) + (_dot(r2, upp_ref[...]) + _dot(r3, upp_ref[...]))

    ii = lax.broadcasted_iota(jnp.int32, (DN_CHUNK, DN_CHUNK), 0)
    jj = lax.broadcasted_iota(jnp.int32, (DN_CHUNK, DN_CHUNK), 1)
    scale = DN_HEAD_DIM ** -0.5
    nt = (((1,), (1,)), ((), ()))
    for c in range(tr // DN_CHUNK):
        rows = slice(c * DN_CHUNK, (c + 1) * DN_CHUNK)
        for hq in range(DN_QK_HEADS):
            q = u_ref[0, 0, rows, hq * DN_HEAD_DIM:(hq + 1) * DN_HEAD_DIM].astype(BF16)
            k = u_ref[0, 0, rows, DN_QK_W + hq * DN_HEAD_DIM:DN_QK_W + (hq + 1) * DN_HEAD_DIM].astype(BF16)
            kk = lax.dot_general(k, k, nt, preferred_element_type=F32)
            qk = lax.dot_general(q, k, nt, preferred_element_type=F32) * scale
            for h in range(hq * (DN_V_HEADS // DN_QK_HEADS), (hq + 1) * (DN_V_HEADS // DN_QK_HEADS)):
                diff = gc_col[rows, h:h + 1] - gc_row[h:h + 1, rows]
                dec = jnp.where(ii >= jj, jnp.exp(jnp.minimum(diff, 0.0)), 0.0)
                a_ref[0, 0, h, c] = jnp.where(ii > jj, kk * beta[rows, h:h + 1] * dec, 0.0)
                qk_ref[0, 0, h, c] = qk * dec


def _dnintra_call(u2, la, lb, lat, alog, dtb):
    n_dir, n_batch, r_len, _ = u2.shape
    tr = 4 * DN_CHUNK
    nc = r_len // DN_CHUNK
    low, upp, one = _tri_mats(tr)
    kern = functools.partial(_dnintra_kernel, tr=tr)
    pad = lambda v: jnp.pad(v, ((0, 0), (0, LANE - DN_V_HEADS))).reshape(n_dir, 1, LANE)
    col = lambda v: v.reshape(n_dir, DN_V_HEADS, 1)
    cm = lambda d, b, t: (0, 0)
    gspec = pl.BlockSpec((1, 1, tr, LANE), lambda d, b, t: (d, b, t, 0))
    mspec = pl.BlockSpec((1, 1, DN_V_HEADS, tr // DN_CHUNK, DN_CHUNK, DN_CHUNK), lambda d, b, t: (d, b, 0, t, 0, 0))
    gshape = jax.ShapeDtypeStruct((n_dir, n_batch, r_len, LANE), F32)
    mshape = jax.ShapeDtypeStruct((n_dir, n_batch, DN_V_HEADS, nc, DN_CHUNK, DN_CHUNK), F32)
    return pl.pallas_call(
        kern,
        grid=(n_dir, n_batch, r_len // tr),
        in_specs=[pl.BlockSpec((1, 1, tr, 2 * DN_QK_W), lambda d, b, t: (d, b, t, 0)),
                  gspec, gspec,
                  pl.BlockSpec((1, 1, DN_V_HEADS, tr), lambda d, b, t: (d, b, 0, t)),
                  pl.BlockSpec((1, 1, LANE), lambda d, b, t: (d, 0, 0)),
                  pl.BlockSpec((1, 1, LANE), lambda d, b, t: (d, 0, 0)),
                  pl.BlockSpec((1, DN_V_HEADS, 1), lambda d, b, t: (d, 0, 0)),
                  pl.BlockSpec((1, DN_V_HEADS, 1), lambda d, b, t: (d, 0, 0)),
                  pl.BlockSpec((tr, tr), cm), pl.BlockSpec((tr, tr), cm), pl.BlockSpec((tr, tr), cm)],
        out_specs=[mspec, mspec, gspec, gspec, gspec, gspec],
        out_shape=[mshape, mshape, gshape, gshape, gshape, gshape],
        compiler_params=_cp(("parallel", "parallel", "parallel")),
        name="deltanet_intra",
    )(u2, la, lb, lat, pad(alog), pad(dtb), col(alog), col(dtb), low, upp, one)


def _dnsolve_kernel(a_ref, o_ref, at_ref, tt_ref):
    c = DN_CHUNK
    for blk in range(c * c // LANE):
        at_ref[blk * LANE:(blk + 1) * LANE, :] = a_ref[:, blk * LANE:(blk + 1) * LANE].T
    tt_ref[...] = jnp.zeros_like(tt_ref)
    for i in range(c):
        nr = 8 * (i // 8 + 1)
        rr = lax.broadcasted_iota(jnp.int32, (nr, LANE), 0)
        acc = jnp.where(rr == i, 1.0, 0.0)

        def body(j, acc, i=i, nr=nr):
            a = at_ref[pl.ds(i * c + j, 1), :]
            return acc - a * tt_ref[pl.ds(pl.multiple_of(j * c, c), nr), :]
        if i > 0:
            acc = lax.fori_loop(0, i, body, acc, unroll=min(i, 8))
        tt_ref[i * c:i * c + nr, :] = acc
    for blk in range(c * c // LANE):
        o_ref[:, blk * LANE:(blk + 1) * LANE] = tt_ref[blk * LANE:(blk + 1) * LANE, :].T


def _dnsolve_call(a2):
    ni, cc = a2.shape
    assert ni % LANE == 0
    return pl.pallas_call(
        _dnsolve_kernel,
        grid=(ni // LANE,),
        in_specs=[pl.BlockSpec((LANE, cc), lambda i: (i, 0))],
        out_specs=pl.BlockSpec((LANE, cc), lambda i: (i, 0)),
        out_shape=jax.ShapeDtypeStruct((ni, cc), F32),
        scratch_shapes=[pltpu.VMEM((cc, LANE), F32), pltpu.VMEM((cc, LANE), F32)],
        compiler_params=_cp(("parallel",)),
        name="deltanet_solve",
    )(a2)


def _dnscan_kernel(u_ref, t_ref, qk_ref, be_ref, eg_ref, ek_ref, gt_ref, o_ref, *s_refs, n_dir, n_batch):
    @pl.when(pl.program_id(0) == 0)
    def _():
        for s_ref in s_refs:
            s_ref[...] = jnp.zeros_like(s_ref)

    scale = DN_HEAD_DIM ** -0.5
    rep = DN_V_HEADS // DN_QK_HEADS
    streams = [(d, b, h) for d in range(n_dir) for b in range(n_batch) for h in range(DN_V_HEADS)]

    def qkv(d, b, h):
        hq = h // rep
        q = u_ref[d, b, :, hq * DN_HEAD_DIM:(hq + 1) * DN_HEAD_DIM]
        k = u_ref[d, b, :, DN_QK_W + hq * DN_HEAD_DIM:DN_QK_W + (hq + 1) * DN_HEAD_DIM]
        v = u_ref[d, b, :, 2 * DN_QK_W + h * DN_HEAD_DIM:2 * DN_QK_W + (h + 1) * DN_HEAD_DIM]
        return q, k, v

    uws = []
    for d, b, h in streams:
        _, k, v = qkv(d, b, h)
        be = be_ref[d, b, :, h:h + 1]
        rhs = jnp.concatenate([v * be, (k * be) * eg_ref[d, b, :, h:h + 1]], axis=1).astype(BF16)
        uws.append(_dot(t_ref[d, b, h, 0].astype(BF16), rhs))
    wqs = []
    for i, (d, b, h) in enumerate(streams):
        q, _, _ = qkv(d, b, h)
        lhs = jnp.concatenate([uws[i][:, DN_HEAD_DIM:], q * (scale * eg_ref[d, b, :, h:h + 1])], axis=0)
        wqs.append(_dot(lhs.astype(BF16), s_refs[i][...].astype(BF16)))
    for i, (d, b, h) in enumerate(streams):
        _, k, _ = qkv(d, b, h)
        vnb = (uws[i][:, :DN_HEAD_DIM] - wqs[i][:DN_CHUNK]).astype(BF16)
        o_ref[d, b, :, h * DN_HEAD_DIM:(h + 1) * DN_HEAD_DIM] = (
            wqs[i][DN_CHUNK:] + _dot(qk_ref[d, b, h, 0].astype(BF16), vnb))
        kd = (k * ek_ref[d, b, :, h:h + 1]).astype(BF16)
        s_refs[i][...] = s_refs[i][...] * gt_ref[d, b, 0:1, h:h + 1] + lax.dot_general(
            kd, vnb, (((0,), (0,)), ((), ())), preferred_element_type=F32)


def _dnscan_call(u2, t6, qk6, be, eg, ek, gt, *, s_len):
    n_dir, n_batch, r_len, _ = u2.shape
    nc = r_len // DN_CHUNK
    ncl = s_len // DN_CHUNK

    def cidx(t):
        return jnp.where(t < nc - ncl, ncl + t, t - (nc - ncl))

    kern = functools.partial(_dnscan_kernel, n_dir=n_dir, n_batch=n_batch)
    gspec = pl.BlockSpec((n_dir, n_batch, DN_CHUNK, LANE), lambda t: (0, 0, cidx(t), 0))
    mspec = pl.BlockSpec((n_dir, n_batch, DN_V_HEADS, 1, DN_CHUNK, DN_CHUNK), lambda t: (0, 0, 0, cidx(t), 0, 0))
    return pl.pallas_call(
        kern,
        grid=(nc,),
        in_specs=[pl.BlockSpec((n_dir, n_batch, DN_CHUNK, DN_W), lambda t: (0, 0, cidx(t), 0)),
                  mspec, mspec, gspec, gspec, gspec, gspec],
        out_specs=pl.BlockSpec((n_dir, n_batch, DN_CHUNK, DN_V_W), lambda t: (0, 0, cidx(t), 0)),
        out_shape=jax.ShapeDtypeStruct((n_dir, n_batch, r_len, DN_V_W), F32),
        scratch_shapes=[pltpu.VMEM((DN_HEAD_DIM, DN_HEAD_DIM), F32)] * (n_dir * n_batch * DN_V_HEADS),
        compiler_params=_cp(("arbitrary",)),
        name="deltanet_scan",
    )(u2, t6, qk6, be, eg, ek, gt)


DN_OUT_COLS = 512


def _dnout_kernel(of_ref, ob_ref, z_ref, g_ref, jm_ref, o_ref):
    for h in range(DN_OUT_COLS // DN_HEAD_DIM):
        sl = slice(h * DN_HEAD_DIM, (h + 1) * DN_HEAD_DIM)
        o = of_ref[0, 0, :, sl] + _flip_rows(jm_ref[...], ob_ref[0, 0, :, sl])
        y = o * lax.rsqrt(jnp.mean(o * o, axis=-1, keepdims=True) + NORM_EPS) * g_ref[...]
        o_ref[0, :, sl] = (y * _silu(z_ref[0, :, sl])).astype(o_ref.dtype)


def _dnout_call(o2, p3, norm_g, *, s_len):
    _, n_batch, r_len, _ = o2.shape
    tr = FLIP_ROWS
    cw = DN_OUT_COLS
    ns, nc = s_len // tr, (r_len - s_len) // tr
    assert OFF_DZ % cw == 0

    def mirror(i):
        return jnp.where(i < ns, ns - 1 - i, 2 * ns + nc - 1 - i)

    return pl.pallas_call(
        _dnout_kernel,
        grid=(n_batch, r_len // tr, DN_V_W // cw),
        in_specs=[pl.BlockSpec((1, 1, tr, cw), lambda b, i, j: (0, b, i, j)),
                  pl.BlockSpec((1, 1, tr, cw), lambda b, i, j: (1, b, mirror(i), j)),
                  pl.BlockSpec((1, tr, cw), lambda b, i, j: (b, i, OFF_DZ // cw + j)),
                  pl.BlockSpec((1, DN_HEAD_DIM), lambda b, i, j: (0, 0)),
                  pl.BlockSpec((FLIP_ROWS, FLIP_ROWS), lambda b, i, j: (0, 0))],
        out_specs=pl.BlockSpec((1, tr, cw), lambda b, i, j: (b, i, j)),
        out_shape=jax.ShapeDtypeStruct((n_batch, r_len, DN_V_W), BF16),
        compiler_params=_cp(("parallel", "parallel", "parallel")),
        name="deltanet_out",
    )(o2, o2, p3, norm_g.reshape(1, DN_HEAD_DIM), _exchange_matrix())


def _seq_flip(a, s_len, axis):
    lat, ctx = jnp.split(a, [s_len], axis=axis)
    return jnp.concatenate([jnp.flip(lat, axis), jnp.flip(ctx, axis)], axis=axis)


def _deltanet_mixer(p3, conv_w, a_log, dt_bias, norm_g, *, s_len):
    n_batch, r_len, _ = p3.shape
    u2 = _dnprep_call(p3, conv_w, s_len=s_len)
    lg =p3[:, :, OFF_DL:OFF_DL + N_LOGITS].reshape(n_batch, r_len, 2, 2, DN_V_HEADS)

    def dirs(x):
        return jnp.stack([x[:, :, 0], _seq_flip(x[:, :, 1], s_len, 1)])

    la, lb = dirs(lg[:, :, 0]), dirs(lg[:, :, 1])
    padl = lambda x: jnp.pad(x, ((0, 0), (0, 0), (0, 0), (0, LANE - DN_V_HEADS)))
    a6, qk6, be, eg, ek, gt = _dnintra_call(u2, padl(la), padl(lb), jnp.swapaxes(la, 2, 3), a_log, dt_bias)
    t6 = _dnsolve_call(a6.reshape(-1, DN_CHUNK * DN_CHUNK)).reshape(a6.shape)
    o2 = _dnscan_call(u2, t6, qk6, be, eg, ek, gt, s_len=s_len)
    return _dnout_call(o2, p3, norm_g, s_len=s_len)


def _merge_kernel(ya_ref, yb_ref, yc_ref, wa_ref, wb_ref, wc_ref, ga_ref, gb_ref, gc_ref, o_ref,
                  wa_s, wb_s, wc_s):
    @pl.when(pl.program_id(1) == 0)
    def _():
        wa_s[...] = wa_ref[0].astype(BF16)
        wb_s[...] = wb_ref[0].astype(BF16)
        wc_s[...] = wc_ref[0].astype(BF16)

    m = (jax.nn.sigmoid(ga_ref[...]) * _dot(ya_ref[...], wa_s[...])
         + jax.nn.sigmoid(gb_ref[...]) * _dot(yb_ref[...], wb_s[...])
         + jax.nn.sigmoid(gc_ref[...]) * _dot(yc_ref[...], wc_s[...]))
    o_ref[...] = m.astype(o_ref.dtype)


def _merge_call(ya, yb, yc, w_pa, w_pb, w_pc, layer, p2, *, r_len):
    rt = ya.shape[0]
    d = w_pa.shape[2]
    tm = _pick(r_len, 1088, 16)
    tn = _pick(d, MERGE_ALIGN)
    assert OFF_MG % tn == 0
    yspec = lambda w: pl.BlockSpec((tm, w), lambda j, i: (i, 0))
    wspec = lambda w: pl.BlockSpec((1, w, tn), lambda j, i: (layer, 0, j))
    gspec = lambda br: pl.BlockSpec((tm, tn), lambda j, i, br=br: (i, (OFF_MG + br * d) // tn + j))
    return pl.pallas_call(
        _merge_kernel,
        grid=(d // tn, rt // tm),
        in_specs=[yspec(ATT_W), yspec(HY_W), yspec(DN_V_W), wspec(ATT_W), wspec(HY_W), wspec(DN_V_W),
                  gspec(0), gspec(1), gspec(2)],
        out_specs=pl.BlockSpec((tm, tn), lambda j, i: (i, j)),
        out_shape=jax.ShapeDtypeStruct((rt, d), BF16),
        scratch_shapes=[pltpu.VMEM((ATT_W, tn), BF16), pltpu.VMEM((HY_W, tn), BF16),
                        pltpu.VMEM((DN_V_W, tn), BF16)],
        compiler_params=_cp(("parallel", "arbitrary"), 56),
        name="branch_merge",
    )(ya, yb, yc, w_pa, w_pb, w_pc, p2, p2, p2)


def _outproj_kernel(m_ref, w_ref, x_ref, gate_ref, o_ref, w_s, *, tm, tiles_per_batch, s_len, n_batch):
    @pl.when(pl.program_id(1) == 0)
    def _():
        w_s[...] = w_ref[0].astype(BF16)

    is_ctx, gl, gc = _row_mods(gate_ref, pl.program_id(1), 0, tm, tm, tiles_per_batch, s_len, n_batch)
    o_ref[...] = x_ref[...] + jnp.where(is_ctx, gc, gl) * _dot(m_ref[...], w_s[...])


def _outproj_call(m, w_out, layer, xs2, mods, *, n_batch, r_len, s_len):
    rt, d = xs2.shape
    tm = _pick(r_len, 1088, 16)
    tn = _pick(d, 512)
    kern = functools.partial(_outproj_kernel, tm=tm, tiles_per_batch=r_len // tm, s_len=s_len, n_batch=n_batch)
    return pl.pallas_call(
        kern,
        grid=(d // tn, rt // tm),
        in_specs=[pl.BlockSpec((tm, d), lambda j, i: (i, 0)),
                  pl.BlockSpec((1, d, tn), lambda j, i: (layer, 0, j)),
                  pl.BlockSpec((tm, tn), lambda j, i: (i, j)),
                  pl.BlockSpec((8, tn), lambda j, i: (0, 2 * d // tn + j))],
        out_specs=pl.BlockSpec((tm, tn), lambda j, i: (i, j)),
        out_shape=jax.ShapeDtypeStruct((rt, d), F32),
        scratch_shapes=[pltpu.VMEM((d, tn), BF16)],
        compiler_params=_cp(("parallel", "arbitrary")),
        name="out_proj_residual",
    )(m, w_out, xs2, mods)


def _finalnorm_kernel(x_ref, g_ref, o_ref):
    x = x_ref[0]
    o_ref[0] = x * lax.rsqrt(jnp.mean(x * x, axis=-1, keepdims=True) + NORM_EPS) * g_ref[...]


def _finalnorm_call(xs, final_g, *, s_len):
    n_batch, _, d = xs.shape
    tr = _pick(s_len, 512, 8)
    return pl.pallas_call(
        _finalnorm_kernel,
        grid=(n_batch, s_len // tr),
        in_specs=[pl.BlockSpec((1, tr, d), lambda b, i: (b, i, 0)), pl.BlockSpec((1, d), lambda b, i: (0, 0))],
        out_specs=pl.BlockSpec((1, tr, d), lambda b, i: (b, i, 0)),
        out_shape=jax.ShapeDtypeStruct((n_batch, s_len, d), F32),
        compiler_params=_cp(("parallel", "parallel")),
        name="final_norm",
    )(xs, final_g.reshape(1, d))


def kernel(x, c, ctx, c_ctx, norm_g, w_mod, b_mod, w_in, q_norm_g, k_norm_g, hy_conv_w, hy_conv_b, hy_w1, hy_b1, hy_freq1, hy_w2, hy_b2, hy_freq2, hy_w3, hy_d, dn_conv_w, dn_a_log, dn_dt_bias, dn_norm_g, w_pa, w_pb, w_pc, w_out, final_g):
    n_batch, s_len, d = x.shape
    ctx_len = ctx.shape[1]
    r_len = s_len + ctx_len
    depth = w_in.shape[0]
    assert n_batch + 1 <= 8 and w_in.shape[2] == OFF_DL + N_LOGITS + N_BRANCH * d

    xs = jnp.concatenate([x, ctx], axis=1)
    cs = jnp.zeros((8, d), F32).at[:n_batch].set(c).at[n_batch].set(c_ctx)
    mods = _mod_call(cs, w_mod, b_mod)
    cos_t, sin_t = _rope_tables(s_len, ctx_len)
    w1p = jnp.pad(hy_w1, ((0, 0), (0, LANE - HY_EMB), (0, 0)))

    for layer in range(depth):
        need_ctx = layer < depth - 1
        xs2 = xs.reshape(n_batch * r_len, d)
        h = _modnorm_call(xs2, norm_g[layer], mods[layer], n_batch=n_batch, r_len=r_len, s_len=s_len)
        p2 = _inproj_call(h, _pad_w_in(w_in[layer]))
        p3 = p2.reshape(n_batch, r_len, -1)

        ya = _attn_call(p3, cos_t, sin_t, q_norm_g[layer], k_norm_g[layer], s_len=s_len)

        filt = (w1p[layer], hy_b1[layer], hy_freq1[layer], hy_w2[layer], hy_b2[layer], hy_freq2[layer],
                hy_w3[layer])
        yb = _hyena_mixer(p3, 0, s_len, hy_conv_w[layer], hy_conv_b[layer], hy_d[layer], filt)
        if need_ctx:
            yb_c = _hyena_mixer(p3, s_len // ctx_len, ctx_len, hy_conv_w[layer], hy_conv_b[layer],
                                hy_d[layer], filt)
        else:
            yb_c = jnp.zeros((n_batch, ctx_len, HY_W), BF16)
        yb = jnp.concatenate([yb, yb_c], axis=1)

        yc = _deltanet_mixer(p3, dn_conv_w[layer], dn_a_log[layer], dn_dt_bias[layer], dn_norm_g[layer],
                             s_len=s_len)

        rt = n_batch * r_len
        m = _merge_call(ya.reshape(rt, ATT_W), yb.reshape(rt, HY_W), yc.reshape(rt, DN_V_W),
                        w_pa, w_pb, w_pc, layer, p2, r_len=r_len)
        xs = _outproj_call(m, w_out, layer, xs2, mods[layer],
                           n_batch=n_batch, r_len=r_len, s_len=s_len).reshape(n_batch, r_len, d)

    return _finalnorm_call(xs, final_g, s_len=s_len)
```

```python
import functools
import math

import jax
import jax.numpy as jnp
import numpy as np
from jax import lax
from jax.experimental import pallas as pl
from jax.experimental.pallas import tpu as pltpu

F32 = jnp.float32
BF16 = jnp.bfloat16

GRID_W = 64
NORM_EPS = 1e-6
N_BRANCH = 3

ATT_HEADS = 8
ATT_KV_HEADS = 2
HEAD_DIM = 128
ATT_GROUP = ATT_HEADS // ATT_KV_HEADS
ATT_W = ATT_HEADS * HEAD_DIM
ATT_KV_W = ATT_KV_HEADS * HEAD_DIM
AXIS_ROPE_DIM = HEAD_DIM // 2
ROPE_THETA = 10000.0

HY_W = 1024
HY_ORDER = 2
HY_BANDS = 16
HY_EMB = 1 + 2 * HY_BANDS
HY_FILTER_HIDDEN = 64
HY_DECAY_TARGET = 1e-2
HY_DECAY_FAST = 0.3
HY_DECAY_SLOW = 1.5

DN_QK_HEADS = 4
DN_V_HEADS = 8
DN_HEAD_DIM = 128
DN_QK_W = DN_QK_HEADS * DN_HEAD_DIM
DN_V_W = DN_V_HEADS * DN_HEAD_DIM
DN_CHUNK = 64
DN_W = 2 * DN_QK_W + DN_V_W

LANE = 128
MIB = 1024 * 1024

OFF_AQ = 0
OFF_AK = OFF_AQ + ATT_W
OFF_AV = OFF_AK + ATT_KV_W
OFF_AG = OFF_AV + ATT_KV_W
OFF_HV = OFF_AG + ATT_W
OFF_HX1 = OFF_HV + HY_W
OFF_HX2 = OFF_HX1 + HY_W
OFF_HG = OFF_HX2 + HY_W
OFF_DQ = OFF_HG + HY_W
OFF_DV = OFF_DQ + 2 * DN_QK_W
OFF_DZ = OFF_DV + DN_V_W
OFF_DL = OFF_DZ + DN_V_W
N_LOGITS = 4 * DN_V_HEADS
MERGE_ALIGN = 512
OFF_MG = -(-(OFF_DL + N_LOGITS) // MERGE_ALIGN) * MERGE_ALIGN


def _pick(n, cap, mult=LANE):
    best = None
    for t in range(mult, min(n, cap) + 1, mult):
        if n % t == 0:
            best = t
    assert best is not None, (n, cap, mult)
    return best


def _cp(sem, vmem_mib=48):
    return pltpu.CompilerParams(dimension_semantics=sem, vmem_limit_bytes=vmem_mib * MIB)


def _silu(x):
    return x * jax.nn.sigmoid(x)


def _split3(x):
    x1 = x.astype(BF16)
    r = x - x1.astype(F32)
    x2 = r.astype(BF16)
    x3 = (r - x2.astype(F32)).astype(BF16)
    return x1, x2, x3


def _dot(a, b):
    return jnp.dot(a, b, preferred_element_type=F32)


def _dot_hp(a, b):
    a1 = a.astype(BF16)
    a2 = (a - a1.astype(F32)).astype(BF16)
    b1 = b.astype(BF16)
    b2 = (b - b1.astype(F32)).astype(BF16)
    return _dot(a1, b1) + (_dot(a1, b2) + _dot(a2, b1))


def _mod_kernel(cs_ref, w_ref, b_ref, o_ref):
    cs = cs_ref[...]
    o_ref[0] = _dot(_silu(cs).astype(BF16), w_ref[0].astype(BF16)) + b_ref[0]


def _mod_call(cs, w_mod, b_mod):
    depth, d, d3 = w_mod.shape
    tn = _pick(d3, 512)
    return pl.pallas_call(
        _mod_kernel,
        grid=(depth, d3 // tn),
        in_specs=[pl.BlockSpec((8, d), lambda l, j: (0, 0)),
                  pl.BlockSpec((1, d, tn), lambda l, j: (l, 0, j)),
                  pl.BlockSpec((1, 1, tn), lambda l, j: (l, 0, j))],
        out_specs=pl.BlockSpec((1, 8, tn), lambda l, j: (l, 0, j)),
        out_shape=jax.ShapeDtypeStruct((depth, 8, d3), F32),
        compiler_params=_cp(("parallel", "parallel")),
        name="adaln_mod",
    )(cs, w_mod, b_mod.reshape(depth, 1, d3))


def _row_mods(mod_ref, i, r0, rows, tm, tiles_per_batch, s_len, n_batch):
    b = i // tiles_per_batch
    row = (i % tiles_per_batch) * tm + r0 + lax.broadcasted_iota(jnp.int32, (rows, 1), 0)
    return row >= s_len, mod_ref[pl.ds(b, 1), :], mod_ref[n_batch:n_batch + 1, :]


NORM_ROWS = 64


def _modnorm_kernel(x_ref, g_ref, mod_ref, h_ref, *, tm, tiles_per_batch, s_len, n_batch, d):
    i = pl.program_id(0)

    def body(c, carry):
        r0 = pl.multiple_of(c * NORM_ROWS, NORM_ROWS)
        x = x_ref[pl.ds(r0, NORM_ROWS), :]
        y = x * lax.rsqrt(jnp.mean(x * x, axis=-1, keepdims=True) + NORM_EPS) * g_ref[...]
        is_ctx, ml, mc = _row_mods(mod_ref, i, r0, NORM_ROWS, tm, tiles_per_batch, s_len, n_batch)
        shift = jnp.where(is_ctx, mc[:, :d], ml[:, :d])
        scale = jnp.where(is_ctx, mc[:, d:2 * d], ml[:, d:2 * d])
        h_ref[pl.ds(r0, NORM_ROWS), :] = (y * (1.0 + scale) + shift).astype(BF16)
        return carry
    lax.fori_loop(0, tm // NORM_ROWS, body, 0)


def _modnorm_call(xs2, norm_g, mods, *, n_batch, r_len, s_len):
    rt, d = xs2.shape
    tm = _pick(r_len, 576, NORM_ROWS)
    kern = functools.partial(_modnorm_kernel, tm=tm, tiles_per_batch=r_len // tm, s_len=s_len,
                             n_batch=n_batch, d=d)
    return pl.pallas_call(
        kern,
        grid=(rt // tm,),
        in_specs=[pl.BlockSpec((tm, d), lambda i: (i, 0)),
                  pl.BlockSpec((1, d), lambda i: (0, 0)),
                  pl.BlockSpec((8, 3 * d), lambda i: (0, 0))],
        out_specs=pl.BlockSpec((tm, d), lambda i: (i, 0)),
        out_shape=jax.ShapeDtypeStruct((rt, d), BF16),
        compiler_params=_cp(("parallel",)),
        name="mod_norm",
    )(xs2, norm_g.reshape(1, d), mods)


def _inproj_kernel(h_ref, w_ref, o_ref):
    o_ref[...] = _dot(h_ref[...], w_ref[0])


def _pad_w_in(w_in):
    cut = OFF_DL + N_LOGITS
    pad = jnp.zeros(w_in.shape[:2] + (OFF_MG - cut,), w_in.dtype)
    return jnp.concatenate([w_in[..., :cut], pad, w_in[..., cut:]], axis=-1).astype(BF16)


def _inproj_call(h, w_bf, layer):
    rt, d = h.shape
    nw = w_bf.shape[2]
    tm = _pick(rt, 2176, 16)
    tn = _pick(nw, 1024)
    return pl.pallas_call(
        _inproj_kernel,
        grid=(rt // tm, nw // tn),
        in_specs=[pl.BlockSpec((tm, d), lambda i, j: (i, 0)),
                  pl.BlockSpec((1, d, tn), lambda i, j: (layer, 0, j))],
        out_specs=pl.BlockSpec((tm, tn), lambda i, j: (i, j)),
        out_shape=jax.ShapeDtypeStruct((rt, nw), F32),
        compiler_params=_cp(("parallel", "arbitrary"), 56),
        name="in_proj",
    )(h, w_bf)


def _rope_tables(s_len, ctx_len):
    m = AXIS_ROPE_DIM // 2
    inv_freq = ROPE_THETA ** (-np.arange(0, AXIS_ROPE_DIM, 2, dtype=np.float64) / AXIS_ROPE_DIM)
    t = np.arange(s_len)
    ang_r = (t // GRID_W)[:, None] * inv_freq
    ang_c = (t % GRID_W)[:, None] * inv_freq
    cos = np.concatenate([np.cos(ang_r)] * 2 + [np.cos(ang_c)] * 2, axis=-1)
    sin = np.concatenate([-np.sin(ang_r), np.sin(ang_r), -np.sin(ang_c), np.sin(ang_c)], axis=-1)
    assert cos.shape[1] == 4 * m == HEAD_DIM
    cos = np.concatenate([cos, np.ones((ctx_len, HEAD_DIM))], axis=0)
    sin = np.concatenate([sin, np.zeros((ctx_len, HEAD_DIM))], axis=0)
    return jnp.asarray(cos, F32), jnp.asarray(sin, F32)


def _norm_rope(x, g, cs, sn):
    y = x * lax.rsqrt(jnp.mean(x * x, axis=-1, keepdims=True) + NORM_EPS) * g
    lane = lax.broadcasted_iota(jnp.int32, (1, HEAD_DIM), 1)
    first = (lane % AXIS_ROPE_DIM) < (AXIS_ROPE_DIM // 2)
    q = AXIS_ROPE_DIM // 2
    partner = jnp.where(first, pltpu.roll(y, HEAD_DIM - q, 1), pltpu.roll(y, q, 1))
    return y * cs + partner * sn


def _attn_kernel(q_ref, k_ref, v_ref, gt_ref, cos_ref, sin_ref, qg_ref, kg_ref, o_ref, ks_ref, vs_ref,
                 *, tq, r_len, s_len, kv_chunk):
    i = pl.program_id(2)

    @pl.when(i == 0)
    def _():
        def body(c, carry):
            r0 = pl.multiple_of(c * kv_chunk, kv_chunk)
            kk = _norm_rope(k_ref[0, pl.ds(r0, kv_chunk), :], kg_ref[...],
                            cos_ref[pl.ds(r0, kv_chunk), :], sin_ref[pl.ds(r0, kv_chunk), :])
            ks_ref[pl.ds(r0, kv_chunk), :] = kk.astype(BF16)
            vs_ref[pl.ds(r0, kv_chunk), :] = v_ref[0, pl.ds(r0, kv_chunk), :].astype(BF16)
            return carry
        lax.fori_loop(0, r_len // kv_chunk, body, 0)

    r0 = pl.multiple_of(i * tq, tq)
    cs = cos_ref[pl.ds(r0, tq), :]
    sn = sin_ref[pl.ds(r0, tq), :]
    scale = HEAD_DIM ** -0.5

    def heads(k_lo, k_hi):
        for g in range(ATT_GROUP):
            sl = slice(g * HEAD_DIM, (g + 1) * HEAD_DIM)
            qh = (_norm_rope(q_ref[0, :, sl], qg_ref[...], cs, sn) * scale).astype(BF16)
            s = lax.dot_general(qh, ks_ref[k_lo:k_hi, :], (((1,), (1,)), ((), ())),
                                preferred_element_type=F32)
            e = jnp.exp(s - jnp.max(s, axis=-1, keepdims=True))
            den = jnp.sum(e, axis=-1, keepdims=True)
            o = _dot(e.astype(BF16), vs_ref[k_lo:k_hi, :]) / den
            o_ref[0, :, sl] = (o * _silu(gt_ref[0, :, sl])).astype(o_ref.dtype)

    @pl.when(i < s_len // tq)
    def _():
        heads(0, r_len)

    @pl.when(i >= s_len // tq)
    def _():
        heads(s_len, r_len)


def _attn_call(p3, cos_t, sin_t, q_g, k_g, *, s_len):
    n_batch, r_len, _ = p3.shape
    ctx_len = r_len - s_len
    tq = 256 if (ctx_len % 256 == 0 and s_len % 256 == 0) else 128
    assert ctx_len % tq == 0 and s_len % tq == 0
    gw = ATT_GROUP * HEAD_DIM
    kern = functools.partial(_attn_kernel, tq=tq, r_len=r_len, s_len=s_len, kv_chunk=tq)
    return pl.pallas_call(
        kern,
        grid=(n_batch, ATT_KV_HEADS, r_len // tq),
        in_specs=[pl.BlockSpec((1, tq, gw), lambda b, h, i: (b, i, OFF_AQ // gw + h)),
                  pl.BlockSpec((1, r_len, HEAD_DIM), lambda b, h, i: (b, 0, OFF_AK // HEAD_DIM + h)),
                  pl.BlockSpec((1, r_len, HEAD_DIM), lambda b, h, i: (b, 0, OFF_AV // HEAD_DIM + h)),
                  pl.BlockSpec((1, tq, gw), lambda b, h, i: (b, i, OFF_AG // gw + h)),
                  pl.BlockSpec((r_len, HEAD_DIM), lambda b, h, i: (0, 0)),
                  pl.BlockSpec((r_len, HEAD_DIM), lambda b, h, i: (0, 0)),
                  pl.BlockSpec((1, HEAD_DIM), lambda b, h, i: (0, 0)),
                  pl.BlockSpec((1, HEAD_DIM), lambda b, h, i: (0, 0))],
        out_specs=pl.BlockSpec((1, tq, gw), lambda b, h, i: (b, i, h)),
        out_shape=jax.ShapeDtypeStruct((n_batch, r_len, ATT_W), BF16),
        scratch_shapes=[pltpu.VMEM((r_len, HEAD_DIM), BF16), pltpu.VMEM((r_len, HEAD_DIM), BF16)],
        compiler_params=_cp(("parallel", "parallel", "arbitrary")),
        name="gqa_attention",
    )(p3, p3, p3, p3, cos_t, sin_t, q_g.reshape(1, HEAD_DIM), k_g.reshape(1, HEAD_DIM))


def _conv3(x, w, first, last):
    n = x.shape[0]
    prev = jnp.where(first, 0.0, pltpu.roll(x, 1, 0))
    nxt = jnp.where(last, 0.0, pltpu.roll(x, n - 1, 0))
    return prev * w[0:1, :] + x * w[1:2, :] + nxt * w[2:3, :]


FFT_UNROLL = 16


def _fft_split(n):
    if n <= 512:
        return 1, n
    n2 = 64
    return n // n2, n2


@functools.lru_cache(maxsize=None)
def _fft_mats(n, n1, n2):
    half = n2 // 2
    a = np.arange(n1)[:, None, None]
    k2 = np.arange(n2)[None, :, None]

    def g(bs):
        ph = (k2 * (a + n1 * bs[None, None, :])) % n
        ang = -2.0 * np.pi * ph / n
        return np.cos(ang), np.sin(ang)

    gre, gim = g(np.arange(half))
    gal = np.concatenate([gre, gim], axis=1)
    gar = np.concatenate([-gim, gre], axis=1)
    hre = np.swapaxes(gre, 1, 2) / n
    him = -np.swapaxes(gim, 1, 2) / n
    hal = np.concatenate([hre, him], axis=1)
    har = np.concatenate([-him, hre], axis=1)
    fre_f, fim_f = g(np.arange(n2))
    gf = np.concatenate([fre_f, fim_f], axis=1)
    k1 = np.arange(n1)
    ang1 = -2.0 * np.pi * ((k1[:, None] * k1[None, :]) % n1) / n1
    f1re, f1im = np.cos(ang1), np.sin(ang1)
    fbl = np.concatenate([f1re, f1im], axis=0)
    fbr = np.concatenate([-f1im, f1re], axis=0)
    fbil = np.concatenate([f1re, -f1im], axis=0)
    fbir = np.concatenate([f1im, f1re], axis=0)
    cast = lambda m: jnp.asarray(m, BF16)
    return dict(ga=cast(np.concatenate([gal, gar], axis=2)), ha=cast(np.concatenate([hal, har], axis=2)),
                gf=cast(gf), fb=cast(np.concatenate([fbl, fbr], axis=1)),
                fbi=cast(np.concatenate([fbil, fbir], axis=1)))


@functools.lru_cache(maxsize=None)
def _hy_tables(n_tok):
    pos = np.arange(n_tok, dtype=np.float64)
    t = pos / max(n_tok - 1, 1)
    bands = np.linspace(1e-4, HY_BANDS - 1, HY_BANDS)
    ang = (2.0 * np.pi / n_tok) * pos[:, None] * bands
    z = np.concatenate([t[:, None], np.cos(ang), np.sin(ang)], axis=-1)
    zrev = np.zeros_like(z)
    zrev[1:] = z[:0:-1]
    ztab = np.zeros((2 * n_tok, LANE))
    ztab[:, :HY_EMB] = np.concatenate([z, zrev], axis=0)
    deltas = np.abs(np.linspace(math.log(HY_DECAY_TARGET) / HY_DECAY_SLOW,
                                math.log(HY_DECAY_TARGET) / HY_DECAY_FAST, HY_W))
    return jnp.asarray(ztab, F32), jnp.asarray(np.tile(deltas, HY_ORDER)[None, :], F32)


SLAB_PAD = 8


def _hyfilt_kernel(z_ref, w1_ref, b1_ref, f1_ref, w2_ref, b2_ref, f2_ref, w3_ref, dl_ref, k_ref, s_ref,
                   *, n_tok, tr, slab_rows):
    i = pl.program_id(0)
    z = z_ref[...]
    h = jnp.sin(f1_ref[...] * (_dot_hp(z, w1_ref[...]) + b1_ref[...]))
    h = jnp.sin(f2_ref[...] * (_dot_hp(h, w2_ref[...]) + b2_ref[...]))
    h = _dot(h.astype(BF16), w3_ref[...].astype(BF16)) * jnp.exp(-z[:, 0:1] * dl_ref[...])
    row = i * tr + lax.broadcasted_iota(jnp.int32, (tr, 1), 0)
    h = jnp.where(row == n_tok, 0.0, h)
    if slab_rows is None:
        k_ref[...] = h
    else:
        step = slab_rows + SLAB_PAD
        for s in range(tr // slab_rows):
            k_ref[s * step:s * step + slab_rows, :] = h[s * slab_rows:(s + 1) * slab_rows]
            k_ref[s * step + slab_rows:(s + 1) * step, :] = jnp.zeros((SLAB_PAD, h.shape[1]), F32)

    @pl.when(i == 0)
    def _():
        s_ref[...] = jnp.zeros_like(s_ref)

    s_ref[...] += jnp.sum(jnp.abs(h), axis=0, keepdims=True)


def _hyfilt_call(n_tok, w1p, b1, f1, w2, b2, f2, w3):
    ztab, dl = _hy_tables(n_tok)
    n = 2 * n_tok
    tr = _pick(n_tok, 512, 8)
    ow = HY_ORDER * HY_W
    hid = HY_FILTER_HIDDEN
    n1, _ = _fft_split(n)
    slab_rows = n1 if n1 > 1 else None
    assert slab_rows is None or tr % slab_rows == 0
    out_tr = tr if slab_rows is None else (tr // slab_rows) * (slab_rows + SLAB_PAD)
    kern = functools.partial(_hyfilt_kernel, n_tok=n_tok, tr=tr, slab_rows=slab_rows)
    c2 = lambda i: (0, 0)
    return pl.pallas_call(
        kern,
        grid=(n // tr,),
        in_specs=[pl.BlockSpec((tr, LANE), lambda i: (i, 0)),
                  pl.BlockSpec((LANE, hid), c2), pl.BlockSpec((1, hid), c2), pl.BlockSpec((1, hid), c2),
                  pl.BlockSpec((hid, hid), c2), pl.BlockSpec((1, hid), c2), pl.BlockSpec((1, hid), c2),
                  pl.BlockSpec((hid, ow), lambda i: (0, i // (n_tok // tr))),
                  pl.BlockSpec((1, ow), c2)],
        out_specs=[pl.BlockSpec((out_tr, ow), lambda i: (i, 0)), pl.BlockSpec((1, ow), c2)],
        out_shape=[jax.ShapeDtypeStruct((n // tr * out_tr, ow), F32), jax.ShapeDtypeStruct((1, ow), F32)],
        compiler_params=_cp(("arbitrary",)),
        name="hyena_filter",
    )(ztab, w1p, b1.reshape(1, hid), f1.reshape(1, hid), w2, b2.reshape(1, hid), f2.reshape(1, hid), w3, dl)


FFT_GROUP = 8


def _filtfft_kernel(k_ref, s_ref, gf_ref, fb_ref, o_ref, z_ref, *, n, n1, n2):
    inv = 1.0 / (s_ref[...] + 1e-6)
    if n1 == 1:
        o_ref[...] = _dot(gf_ref[0], k_ref[...].astype(BF16)) * inv
        return
    slab = 2 * n1
    zs, ks = slab + SLAB_PAD, n1 + SLAB_PAD

    def stage_a(a, carry):
        out = _dot(gf_ref[a], k_ref[pl.ds(a, n2, stride=ks), :].astype(BF16))
        z_ref[pl.ds(a, n2, stride=zs), :] = out[:n2]
        z_ref[pl.ds(n1 + a, n2, stride=zs), :] = out[n2:]
        return carry
    lax.fori_loop(0, n1, stage_a, 0, unroll=FFT_UNROLL)

    def stage_b(g, carry):
        for u in range(FFT_GROUP):
            k2 = g * FFT_GROUP + u
            z = z_ref[pl.ds(pl.multiple_of(k2 * zs, 8), slab), :].astype(BF16)
            o_ref[pl.ds(pl.multiple_of(k2 * slab, slab), slab), :] = _dot(fb_ref[...], z) * inv
        return carry
    lax.fori_loop(0, n2 // FFT_GROUP, stage_b, 0)


def _filtfft_call(k_un, asum, n_tok):
    k_rows, ow = k_un.shape
    n = 2 * n_tok
    n1, n2 = _fft_split(n)
    assert k_rows == (n if n1 == 1 else n2 * (n1 + SLAB_PAD))
    m = _fft_mats(n, n1, n2)
    cb = LANE
    kern = functools.partial(_filtfft_kernel, n=n, n1=n1, n2=n2)
    return pl.pallas_call(
        kern,
        grid=(ow // cb,),
        in_specs=[pl.BlockSpec((k_rows, cb), lambda j: (0, j)),
                  pl.BlockSpec((1, cb), lambda j: (0, j)),
                  pl.BlockSpec(m["gf"].shape, lambda j: (0, 0, 0)),
                  pl.BlockSpec(m["fb"].shape, lambda j: (0, 0))],
        out_specs=pl.BlockSpec((2 * n, cb), lambda j: (0, j)),
        out_shape=jax.ShapeDtypeStruct((2 * n, ow), F32),
        scratch_shapes=[pltpu.VMEM((n2 * (2 * n1 + SLAB_PAD) if n1 > 1 else 8, cb), F32)],
        compiler_params=_cp(("parallel",)),
        name="hyena_filter_fft",
    )(k_un, asum, m["gf"], m["fb"])


def _hyena_kernel(*refs, n_tok, n1, n2, conv_a, has_gate):
    it = iter(refs)
    a_ref, m_ref = next(it), next(it)
    g_ref = next(it) if has_gate else None
    if conv_a:
        cwa_ref, cba_ref = next(it), next(it)
    cwm_ref, cbm_ref, d_ref, ks_ref = next(it), next(it), next(it), next(it)
    ga_ref, ha_ref = next(it), next(it)
    if n1 > 1:
        fb_ref, fbi_ref = next(it), next(it)
    o_ref, ac_ref, z_ref = next(it), next(it), next(it)

    n = 2 * n_tok
    half = n2 // 2
    row = lax.broadcasted_iota(jnp.int32, (n_tok, 1), 0)
    first, last = row == 0, row == n_tok - 1
    sr = n1 if n1 > 1 else n_tok
    astep = sr + SLAB_PAD if n1 > 1 else sr
    for b in range(2):
        a = a_ref[b]
        if conv_a:
            a = _conv3(a, cwa_ref[...], first, last) + cba_ref[...]
        for s in range(n_tok // sr):
            ac_ref[b, s * astep:s * astep + sr, :] = a[s * sr:(s + 1) * sr]

    def spectrum_mul(x, k, h):
        xre, xim, kre, kim = x[:h], x[h:], k[:h], k[h:]
        return jnp.concatenate([xre * kre - xim * kim, xre * kim + xim * kre], axis=0).astype(BF16)

    if n1 == 1:
        x = _dot(ga_ref[0], jnp.concatenate([ac_ref[0], ac_ref[1]], axis=0).astype(BF16))
        y = _dot(ha_ref[0], spectrum_mul(x, ks_ref[...], n))
        for b in range(2):
            ac_ref[b] = y[b * n_tok:(b + 1) * n_tok] + ac_ref[b] * d_ref[0]
    else:
        slab = 2 * n1
        zs = slab + SLAB_PAD

        def stage_a(a, carry):
            rows = jnp.concatenate([ac_ref[0, pl.ds(a, half, stride=astep), :],
                                    ac_ref[1, pl.ds(a, half, stride=astep), :]], axis=0).astype(BF16)
            out = _dot(ga_ref[a], rows)
            z_ref[pl.ds(a, n2, stride=zs), :] = out[:n2]
            z_ref[pl.ds(n1 + a, n2, stride=zs), :] = out[n2:]
            return carry
        lax.fori_loop(0, n1, stage_a, 0, unroll=FFT_UNROLL)

        def stage_b(g, carry):
            k2s = [g * FFT_GROUP + u for u in range(FFT_GROUP)]
            zrows = [pl.ds(pl.multiple_of(k2 * zs, 8), slab) for k2 in k2s]
            xs = [_dot(fb_ref[...], z_ref[zr, :].astype(BF16)) for zr in zrows]
            ys = [spectrum_mul(x, ks_ref[pl.ds(pl.multiple_of(k2 * slab, slab), slab), :], n1)
                  for x, k2 in zip(xs, k2s)]
            for y, zr in zip(ys, zrows):
                z_ref[zr, :] = _dot(fbi_ref[...], y)
            return carry
        lax.fori_loop(0, n2 // FFT_GROUP, stage_b, 0)

        def stage_a_inv(a, carry):
            rows = jnp.concatenate([z_ref[pl.ds(a, n2, stride=zs), :],
                                    z_ref[pl.ds(n1 + a, n2, stride=zs), :]], axis=0).astype(BF16)
            y = _dot(ha_ref[a], rows)
            for b in range(2):
                cur = ac_ref[b, pl.ds(a, half, stride=astep), :]
                ac_ref[b, pl.ds(a, half, stride=astep), :] = y[b * half:(b + 1) * half] + cur * d_ref[0]
            return carry
        lax.fori_loop(0, n1, stage_a_inv, 0, unroll=FFT_UNROLL)

    for b in range(2):
        mc = _conv3(m_ref[b], cwm_ref[...], first, last) + cbm_ref[...]
        for s in range(n_tok // sr):
            rows = slice(s * sr, (s + 1) * sr)
            out = mc[rows] * ac_ref[b, s * astep:s * astep + sr, :]
            if has_gate:
                out = out * _silu(g_ref[b, rows, :])
            o_ref[b, rows, :] = out.astype(o_ref.dtype)


def _hyena_call(a_arr, a_blk, m_arr, m_blk, gate, conv_a, conv_m, d_row, spec, spec_blk, *, n_tok, out_dtype):
    n_batch = a_arr.shape[0]
    assert n_batch % 2 == 0
    n = 2 * n_tok
    n1, n2 = _fft_split(n)
    m = _fft_mats(n, n1, n2)
    cb = LANE
    one = pl.Buffered(1)
    tile_mode, spec_mode = (one, pl.Buffered(2)) if gate is not None else (pl.Buffered(2), one)

    def tile(blk):
        return pl.BlockSpec((2, n_tok, cb), lambda p, j, blk=blk: (p, blk[0], blk[1] + j), pipeline_mode=tile_mode)

    def convspec(c0):
        return [pl.BlockSpec((3, cb), lambda p, j, c0=c0: (0, c0 + j)),
                pl.BlockSpec((1, cb), lambda p, j, c0=c0: (0, c0 + j))]

    ins, specs = [a_arr, m_arr], [tile(a_blk), tile(m_blk)]
    if gate is not None:
        ins.append(gate[0])
        specs.append(tile(gate[1]))
    if conv_a is not None:
        ins += [conv_a[0], conv_a[1]]
        specs += convspec(conv_a[2])
    ins += [conv_m[0], conv_m[1], d_row, spec]
    specs += convspec(conv_m[2])
    specs += [pl.BlockSpec((1, 1, cb), lambda p, j: (0, 0, j)),
              pl.BlockSpec((2 * n, cb), lambda p, j, s0=spec_blk: (0, s0 + j), pipeline_mode=spec_mode)]
    names = ["ga", "ha"] + (["fb", "fbi"] if n1 > 1 else [])
    for nm in names:
        ins.append(m[nm])
        specs.append(pl.BlockSpec(m[nm].shape, lambda p, j, nd=m[nm].ndim: (0,) * nd, pipeline_mode=one))
    kern = functools.partial(_hyena_kernel, n_tok=n_tok, n1=n1, n2=n2, conv_a=conv_a is not None,
                             has_gate=gate is not None)
    return pl.pallas_call(
        kern,
        grid=(n_batch // 2, HY_W // cb),
        in_specs=specs,
        out_specs=pl.BlockSpec((2, n_tok, cb), lambda p, j: (p, 0, j)),
        out_shape=jax.ShapeDtypeStruct((n_batch, n_tok, HY_W), out_dtype),
        scratch_shapes=[pltpu.VMEM((2, n_tok // n1 * (n1 + SLAB_PAD) if n1 > 1 else n_tok, cb), F32),
                        pltpu.VMEM((n2 * (2 * n1 + SLAB_PAD) if n1 > 1 else 8, cb), F32)],
        compiler_params=_cp(("parallel", "arbitrary"), 56),
        name="hyena_conv",
    )(*ins)


def _hyena_mixer(p3, row_blk, n_tok, conv_w, conv_b, hy_d, filt):
    k_un, asum = _hyfilt_call(n_tok, *filt)
    spec = _filtfft_call(k_un, asum, n_tok)
    cbias = conv_b.reshape(1, -1)
    cblk = HY_W // LANE
    d3 = hy_d.reshape(HY_ORDER, 1, HY_W)
    z = _hyena_call(p3, (row_blk, OFF_HV // LANE), p3, (row_blk, OFF_HX1 // LANE), None,
                    (conv_w, cbias, 0), (conv_w, cbias, cblk), d3[0:1], spec, 0,
                    n_tok=n_tok, out_dtype=F32)
    return _hyena_call(z, (0, 0), p3, (row_blk, OFF_HX2 // LANE), (p3, (row_blk, OFF_HG // LANE)),
                       None, (conv_w, cbias, 2 * cblk), d3[1:2], spec, cblk,
                       n_tok=n_tok, out_dtype=BF16)


def _softplus(x):
    return jnp.maximum(x, 0.0) + jnp.log1p(jnp.exp(-jnp.abs(x)))


FLIP_ROWS = 256


def _exchange_matrix():
    return jnp.asarray(np.eye(FLIP_ROWS)[::-1], BF16)


def _flip_rows(jm, x):
    x1, x2, x3 = _split3(x)
    return _dot(jm, x1) + (_dot(jm, x2) + _dot(jm, x3))


def _dnprep_kernel(x_ref, w_ref, jm_ref, o_ref, *, r_len, s_len):
    j = pl.program_id(1)
    row = lax.broadcasted_iota(jnp.int32, (r_len, 1), 0)
    first = (row == 0) | (row == s_len)
    last = (row == s_len - 1) | (row == r_len - 1)
    u = _silu(_conv3(x_ref[0], w_ref[...], first, last))
    nrm = u * lax.rsqrt(jnp.sum(u * u, axis=-1, keepdims=True) + 1e-6)
    o_ref[0, 0] = jnp.where(j < 2 * DN_QK_HEADS, nrm, u)
    for seg0, seg_len in ((0, s_len), (s_len, r_len - s_len)):
        nt = seg_len // FLIP_ROWS
        for t in range(nt):
            src = seg0 + t * FLIP_ROWS
            dst = seg0 + (nt - 1 - t) * FLIP_ROWS
            o_ref[1, 0, dst:dst + FLIP_ROWS, :] = _flip_rows(jm_ref[...], o_ref[0, 0, src:src + FLIP_ROWS, :])


def _dnprep_call(p3, conv_w, *, s_len):
    n_batch, r_len, _ = p3.shape
    assert s_len % FLIP_ROWS == 0 and (r_len - s_len) % FLIP_ROWS == 0
    kern = functools.partial(_dnprep_kernel, r_len=r_len, s_len=s_len)
    return pl.pallas_call(
        kern,
        grid=(n_batch, DN_W // LANE),
        in_specs=[pl.BlockSpec((1, r_len, LANE), lambda b, j: (b, 0, OFF_DQ // LANE + j)),
                  pl.BlockSpec((3, LANE), lambda b, j: (0, j)),
                  pl.BlockSpec((FLIP_ROWS, FLIP_ROWS), lambda b, j: (0, 0))],
        out_specs=pl.BlockSpec((2, 1, r_len, LANE), lambda b, j: (0, b, 0, j)),
        out_shape=jax.ShapeDtypeStruct((2, n_batch, r_len, DN_W), F32),
        compiler_params=_cp(("parallel", "parallel")),
        name="deltanet_prep",
    )(p3, conv_w, _exchange_matrix())


def _tri_mats(tr):
    idx = np.arange(tr)
    same = (idx[:, None] // DN_CHUNK) == (idx[None, :] // DN_CHUNK)
    low = same & (idx[:, None] >= idx[None, :])
    return jnp.asarray(low, BF16), jnp.asarray(low.T, BF16), jnp.asarray(same, BF16)


def _dnintra_kernel(u_ref, la_ref, lb_ref, lat_ref, alr_ref, dtr_ref, alc_ref, dtc_ref, low_ref, upp_ref,
                    one_ref, a_ref, qk_ref, be_ref, eg_ref, ek_ref, gt_ref, *, tr):
    g_col = -jnp.exp(alr_ref[0]) * _softplus(la_ref[0, 0] + dtr_ref[0])
    beta = jax.nn.sigmoid(lb_ref[0, 0])
    g1, g2, g3 = _split3(g_col)
    gc_col = _dot(low_ref[...], g1) + (_dot(low_ref[...], g2) + _dot(low_ref[...], g3))
    gt_col = _dot(one_ref[...], g1) + (_dot(one_ref[...], g2) + _dot(one_ref[...], g3))
    be_ref[0, 0] = beta
    eg_ref[0, 0] = jnp.exp(gc_col)
    ek_ref[0, 0] = jnp.exp(gt_col - gc_col)
    gt_ref[0, 0] = jnp.exp(gt_col)
    g_row = -jnp.exp(alc_ref[0]) * _softplus(lat_ref[0, 0] + dtc_ref[0])
    r1, r2, r3 = _split3(g_row)
    gc_row = _dot(r1, upp_ref[...]) + (_dot(r2, upp_ref[...]) + _dot(r3, upp_ref[...]))

    ii = lax.broadcasted_iota(jnp.int32, (DN_CHUNK, DN_CHUNK), 0)
    jj = lax.broadcasted_iota(jnp.int32, (DN_CHUNK, DN_CHUNK), 1)
    scale = DN_HEAD_DIM ** -0.5
    nt = (((1,), (1,)), ((), ()))
    for c in range(tr // DN_CHUNK):
        rows = slice(c * DN_CHUNK, (c + 1) * DN_CHUNK)
        for hq in range(DN_QK_HEADS):
            q = u_ref[0, 0, rows, hq * DN_HEAD_DIM:(hq + 1) * DN_HEAD_DIM].astype(BF16)
            k = u_ref[0, 0, rows, DN_QK_W + hq * DN_HEAD_DIM:DN_QK_W + (hq + 1) * DN_HEAD_DIM].astype(BF16)
            kk = lax.dot_general(k, k, nt, preferred_element_type=F32)
            qk = lax.dot_general(q, k, nt, preferred_element_type=F32) * scale
            for h in range(hq * (DN_V_HEADS // DN_QK_HEADS), (hq + 1) * (DN_V_HEADS // DN_QK_HEADS)):
                diff = gc_col[rows, h:h + 1] - gc_row[h:h + 1, rows]
                dec = jnp.where(ii >= jj, jnp.exp(jnp.minimum(diff, 0.0)), 0.0)
                a_ref[0, 0, h, c] = jnp.where(ii > jj, kk * beta[rows, h:h + 1] * dec, 0.0)
                qk_ref[0, 0, h, c] = (qk * dec).astype(qk_ref.dtype)


def _dnintra_call(u2, la, lb, lat, alog, dtb):
    n_dir, n_batch, r_len, _ = u2.shape
    tr = 4 * DN_CHUNK
    nc = r_len // DN_CHUNK
    low, upp, one = _tri_mats(tr)
    kern = functools.partial(_dnintra_kernel, tr=tr)
    pad = lambda v: jnp.pad(v, ((0, 0), (0, LANE - DN_V_HEADS))).reshape(n_dir, 1, LANE)
    col = lambda v: v.reshape(n_dir, DN_V_HEADS, 1)
    cm = lambda d, b, t: (0, 0)
    gspec = pl.BlockSpec((1, 1, tr, LANE), lambda d, b, t: (d, b, t, 0))
    mspec = pl.BlockSpec((1, 1, DN_V_HEADS, tr // DN_CHUNK, DN_CHUNK, DN_CHUNK), lambda d, b, t: (d, b, 0, t, 0, 0))
    gshape = jax.ShapeDtypeStruct((n_dir, n_batch, r_len, LANE), F32)
    mshape = jax.ShapeDtypeStruct((n_dir, n_batch, DN_V_HEADS, nc, DN_CHUNK, DN_CHUNK), F32)
    return pl.pallas_call(
        kern,
        grid=(n_dir, n_batch, r_len // tr),
        in_specs=[pl.BlockSpec((1, 1, tr, 2 * DN_QK_W), lambda d, b, t: (d, b, t, 0)),
                  gspec, gspec,
                  pl.BlockSpec((1, 1, DN_V_HEADS, tr), lambda d, b, t: (d, b, 0, t)),
                  pl.BlockSpec((1, 1, LANE), lambda d, b, t: (d, 0, 0)),
                  pl.BlockSpec((1, 1, LANE), lambda d, b, t: (d, 0, 0)),
                  pl.BlockSpec((1, DN_V_HEADS, 1), lambda d, b, t: (d, 0, 0)),
                  pl.BlockSpec((1, DN_V_HEADS, 1), lambda d, b, t: (d, 0, 0)),
                  pl.BlockSpec((tr, tr), cm), pl.BlockSpec((tr, tr), cm), pl.BlockSpec((tr, tr), cm)],
        out_specs=[mspec, mspec, gspec, gspec, gspec, gspec],
        out_shape=[mshape, jax.ShapeDtypeStruct(mshape.shape, BF16), gshape, gshape, gshape, gshape],
        compiler_params=_cp(("parallel", "parallel", "parallel")),
        name="deltanet_intra",
    )(u2, la, lb, lat, pad(alog), pad(dtb), col(alog), col(dtb), low, upp, one)


def _dnsolve_kernel(a_ref, o_ref, at_ref, tt_ref):
    c = DN_CHUNK
    for blk in range(c * c // LANE):
        at_ref[blk * LANE:(blk + 1) * LANE, :] = a_ref[:, blk * LANE:(blk + 1) * LANE].T
    tt_ref[...] = jnp.zeros_like(tt_ref)
    for i in range(c):
        nr = 8 * (i // 8 + 1)
        rr = lax.broadcasted_iota(jnp.int32, (nr, LANE), 0)
        acc = jnp.where(rr == i, 1.0, 0.0)

        def body(j, acc, i=i, nr=nr):
            a = at_ref[pl.ds(i * c + j, 1), :]
            return acc - a * tt_ref[pl.ds(pl.multiple_of(j * c, c), nr), :]
        if i > 0:
            acc = lax.fori_loop(0, i, body, acc, unroll=min(i, 8))
        tt_ref[i * c:i * c + nr, :] = acc
    for blk in range(c * c // LANE):
        o_ref[:, blk * LANE:(blk + 1) * LANE] = tt_ref[blk * LANE:(blk + 1) * LANE, :].T.astype(o_ref.dtype)


def _dnsolve_call(a2):
    ni, cc = a2.shape
    assert ni % LANE == 0
    return pl.pallas_call(
        _dnsolve_kernel,
        grid=(ni // LANE,),
        in_specs=[pl.BlockSpec((LANE, cc), lambda i: (i, 0))],
        out_specs=pl.BlockSpec((LANE, cc), lambda i: (i, 0)),
        out_shape=jax.ShapeDtypeStruct((ni, cc), BF16),
        scratch_shapes=[pltpu.VMEM((cc, LANE), F32), pltpu.VMEM((cc, LANE), F32)],
        compiler_params=_cp(("parallel",)),
        name="deltanet_solve",
    )(a2)


def _dnscan_kernel(u_ref, t_ref, qk_ref, be_ref, eg_ref, ek_ref, gt_ref, o_ref, *s_refs, n_dir, n_batch):
    @pl.when(pl.program_id(0) == 0)
    def _():
        for s_ref in s_refs:
            s_ref[...] = jnp.zeros_like(s_ref)

    scale = DN_HEAD_DIM ** -0.5
    rep = DN_V_HEADS // DN_QK_HEADS
    streams = [(d, b, h) for d in range(n_dir) for b in range(n_batch) for h in range(DN_V_HEADS)]

    def qkv(d, b, h):
        hq = h // rep
        q = u_ref[d, b, :, hq * DN_HEAD_DIM:(hq + 1) * DN_HEAD_DIM]
        k = u_ref[d, b, :, DN_QK_W + hq * DN_HEAD_DIM:DN_QK_W + (hq + 1) * DN_HEAD_DIM]
        v = u_ref[d, b, :, 2 * DN_QK_W + h * DN_HEAD_DIM:2 * DN_QK_W + (h + 1) * DN_HEAD_DIM]
        return q, k, v

    uws, egs = [], []
    for d, b, h in streams:
        _, k, v = qkv(d, b, h)
        be = be_ref[d, b, :, h:h + 1]
        egs.append(jnp.broadcast_to(eg_ref[d, b, :, h:h + 1], (DN_CHUNK, DN_HEAD_DIM)))
        rhs = jnp.concatenate([v * be, (k * be) * egs[-1]], axis=1).astype(BF16)
        uws.append(_dot(t_ref[d, b, h, 0], rhs))
    wqs = []
    for i, (d, b, h) in enumerate(streams):
        q, _, _ = qkv(d, b, h)
        lhs = jnp.concatenate([uws[i][:, DN_HEAD_DIM:], q * (scale * egs[i])], axis=0)
        wqs.append(_dot(lhs.astype(BF16), s_refs[i][...].astype(BF16)))
    for i, (d, b, h) in enumerate(streams):
        _, k, _ = qkv(d, b, h)
        vnb = (uws[i][:, :DN_HEAD_DIM] - wqs[i][:DN_CHUNK]).astype(BF16)
        o_ref[d, b, :, h * DN_HEAD_DIM:(h + 1) * DN_HEAD_DIM] = (
            wqs[i][DN_CHUNK:] + _dot(qk_ref[d, b, h, 0], vnb))
        kd = (k * ek_ref[d, b, :, h:h + 1]).astype(BF16)
        s_refs[i][...] = s_refs[i][...] * gt_ref[d, b, 0:1, h:h + 1] + lax.dot_general(
            kd, vnb, (((0,), (0,)), ((), ())), preferred_element_type=F32)


def _dnscan_call(u2, t6, qk6, be, eg, ek, gt, *, s_len):
    n_dir, n_batch, r_len, _ = u2.shape
    nc = r_len // DN_CHUNK
    ncl = s_len // DN_CHUNK

    def cidx(t):
        return jnp.where(t < nc - ncl, ncl + t, t - (nc - ncl))

    kern = functools.partial(_dnscan_kernel, n_dir=n_dir, n_batch=n_batch)
    gspec = pl.BlockSpec((n_dir, n_batch, DN_CHUNK, LANE), lambda t: (0, 0, cidx(t), 0))
    mspec = pl.BlockSpec((n_dir, n_batch, DN_V_HEADS, 1, DN_CHUNK, DN_CHUNK), lambda t: (0, 0, 0, cidx(t), 0, 0))
    return pl.pallas_call(
        kern,
        grid=(nc,),
        in_specs=[pl.BlockSpec((n_dir, n_batch, DN_CHUNK, DN_W), lambda t: (0, 0, cidx(t), 0)),
                  mspec, mspec, gspec, gspec, gspec, gspec],
        out_specs=pl.BlockSpec((n_dir, n_batch, DN_CHUNK, DN_V_W), lambda t: (0, 0, cidx(t), 0)),
        out_shape=jax.ShapeDtypeStruct((n_dir, n_batch, r_len, DN_V_W), F32),
        scratch_shapes=[pltpu.VMEM((DN_HEAD_DIM, DN_HEAD_DIM), F32)] * (n_dir * n_batch * DN_V_HEADS),
        compiler_params=_cp(("arbitrary",)),
        name="deltanet_scan",
    )(u2, t6, qk6, be, eg, ek, gt)


DN_OUT_COLS = 512


def _dnout_kernel(of_ref, ob_ref, z_ref, g_ref, jm_ref, o_ref):
    for h in range(DN_OUT_COLS // DN_HEAD_DIM):
        sl = slice(h * DN_HEAD_DIM, (h + 1) * DN_HEAD_DIM)
        o = of_ref[0, 0, :, sl] + _flip_rows(jm_ref[...], ob_ref[0, 0, :, sl])
        y = o * lax.rsqrt(jnp.mean(o * o, axis=-1, keepdims=True) + NORM_EPS) * g_ref[...]
        o_ref[0, :, sl] = (y * _silu(z_ref[0, :, sl])).astype(o_ref.dtype)


def _dnout_call(o2, p3, norm_g, *, s_len):
    _, n_batch, r_len, _ = o2.shape
    tr = FLIP_ROWS
    cw = DN_OUT_COLS
    ns, nc = s_len // tr, (r_len - s_len) // tr
    assert OFF_DZ % cw == 0

    def mirror(i):
        return jnp.where(i < ns, ns - 1 - i, 2 * ns + nc - 1 - i)

    return pl.pallas_call(
        _dnout_kernel,
        grid=(n_batch, r_len // tr, DN_V_W // cw),
        in_specs=[pl.BlockSpec((1, 1, tr, cw), lambda b, i, j: (0, b, i, j)),
                  pl.BlockSpec((1, 1, tr, cw), lambda b, i, j: (1, b, mirror(i), j)),
                  pl.BlockSpec((1, tr, cw), lambda b, i, j: (b, i, OFF_DZ // cw + j)),
                  pl.BlockSpec((1, DN_HEAD_DIM), lambda b, i, j: (0, 0)),
                  pl.BlockSpec((FLIP_ROWS, FLIP_ROWS), lambda b, i, j: (0, 0))],
        out_specs=pl.BlockSpec((1, tr, cw), lambda b, i, j: (b, i, j)),
        out_shape=jax.ShapeDtypeStruct((n_batch, r_len, DN_V_W), BF16),
        compiler_params=_cp(("parallel", "parallel", "parallel")),
        name="deltanet_out",
    )(o2, o2, p3, norm_g.reshape(1, DN_HEAD_DIM), _exchange_matrix())


def _seq_flip(a, s_len, axis):
    lat, ctx = jnp.split(a, [s_len], axis=axis)
    return jnp.concatenate([jnp.flip(lat, axis), jnp.flip(ctx, axis)], axis=axis)


def _deltanet_mixer(p3, conv_w, a_log, dt_bias, norm_g, *, s_len):
    n_batch, r_len, _ = p3.shape
    u2 = _dnprep_call(p3, conv_w, s_len=s_len)
    lg =p3[:, :, OFF_DL:OFF_DL + N_LOGITS].reshape(n_batch, r_len, 2, 2, DN_V_HEADS)

    def dirs(x):
        return jnp.stack([x[:, :, 0], _seq_flip(x[:, :, 1], s_len, 1)])

    la, lb = dirs(lg[:, :, 0]), dirs(lg[:, :, 1])
    padl = lambda x: jnp.pad(x, ((0, 0), (0, 0), (0, 0), (0, LANE - DN_V_HEADS)))
    a6, qk6, be, eg, ek, gt = _dnintra_call(u2, padl(la), padl(lb), jnp.swapaxes(la, 2, 3), a_log, dt_bias)
    t6 = _dnsolve_call(a6.reshape(-1, DN_CHUNK * DN_CHUNK)).reshape(a6.shape)
    o2 = _dnscan_call(u2, t6, qk6, be, eg, ek, gt, s_len=s_len)
    return _dnout_call(o2, p3, norm_g, s_len=s_len)


def _merge_kernel(ya_ref, yb_ref, yc_ref, wa_ref, wb_ref, wc_ref, ga_ref, gb_ref, gc_ref, o_ref,
                  wa_s, wb_s, wc_s):
    @pl.when(pl.program_id(1) == 0)
    def _():
        wa_s[...] = wa_ref[0].astype(BF16)
        wb_s[...] = wb_ref[0].astype(BF16)
        wc_s[...] = wc_ref[0].astype(BF16)

    m = (jax.nn.sigmoid(ga_ref[...]) * _dot(ya_ref[...], wa_s[...])
         + jax.nn.sigmoid(gb_ref[...]) * _dot(yb_ref[...], wb_s[...])
         + jax.nn.sigmoid(gc_ref[...]) * _dot(yc_ref[...], wc_s[...]))
    o_ref[...] = m.astype(o_ref.dtype)


def _merge_call(ya, yb, yc, w_pa, w_pb, w_pc, layer, p2, *, r_len):
    rt = ya.shape[0]
    d = w_pa.shape[2]
    tm = _pick(r_len, 1088, 16)
    tn = _pick(d, MERGE_ALIGN)
    assert OFF_MG % tn == 0
    yspec = lambda w: pl.BlockSpec((tm, w), lambda j, i: (i, 0))
    wspec = lambda w: pl.BlockSpec((1, w, tn), lambda j, i: (layer, 0, j))
    gspec = lambda br: pl.BlockSpec((tm, tn), lambda j, i, br=br: (i, (OFF_MG + br * d) // tn + j))
    return pl.pallas_call(
        _merge_kernel,
        grid=(d // tn, rt // tm),
        in_specs=[yspec(ATT_W), yspec(HY_W), yspec(DN_V_W), wspec(ATT_W), wspec(HY_W), wspec(DN_V_W),
                  gspec(0), gspec(1), gspec(2)],
        out_specs=pl.BlockSpec((tm, tn), lambda j, i: (i, j)),
        out_shape=jax.ShapeDtypeStruct((rt, d), BF16),
        scratch_shapes=[pltpu.VMEM((ATT_W, tn), BF16), pltpu.VMEM((HY_W, tn), BF16),
                        pltpu.VMEM((DN_V_W, tn), BF16)],
        compiler_params=_cp(("parallel", "arbitrary"), 56),
        name="branch_merge",
    )(ya, yb, yc, w_pa, w_pb, w_pc, p2, p2, p2)


def _outproj_kernel(m_ref, w_ref, x_ref, gate_ref, o_ref, w_s, *, tm, tiles_per_batch, s_len, n_batch):
    @pl.when(pl.program_id(1) == 0)
    def _():
        w_s[...] = w_ref[0].astype(BF16)

    is_ctx, gl, gc = _row_mods(gate_ref, pl.program_id(1), 0, tm, tm, tiles_per_batch, s_len, n_batch)
    o_ref[...] = x_ref[...] + jnp.where(is_ctx, gc, gl) * _dot(m_ref[...], w_s[...])


def _outproj_call(m, w_out, layer, xs2, mods, *, n_batch, r_len, s_len):
    rt, d = xs2.shape
    tm = _pick(r_len, 1088, 16)
    tn = _pick(d, 512)
    kern = functools.partial(_outproj_kernel, tm=tm, tiles_per_batch=r_len // tm, s_len=s_len, n_batch=n_batch)
    return pl.pallas_call(
        kern,
        grid=(d // tn, rt // tm),
        in_specs=[pl.BlockSpec((tm, d), lambda j, i: (i, 0)),
                  pl.BlockSpec((1, d, tn), lambda j, i: (layer, 0, j)),
                  pl.BlockSpec((tm, tn), lambda j, i: (i, j)),
                  pl.BlockSpec((8, tn), lambda j, i: (0, 2 * d // tn + j))],
        out_specs=pl.BlockSpec((tm, tn), lambda j, i: (i, j)),
        out_shape=jax.ShapeDtypeStruct((rt, d), F32),
        scratch_shapes=[pltpu.VMEM((d, tn), BF16)],
        compiler_params=_cp(("parallel", "arbitrary")),
        name="out_proj_residual",
    )(m, w_out, xs2, mods)


def _finalnorm_kernel(x_ref, g_ref, o_ref):
    x = x_ref[0]
    o_ref[0] = x * lax.rsqrt(jnp.mean(x * x, axis=-1, keepdims=True) + NORM_EPS) * g_ref[...]


def _finalnorm_call(xs, final_g, *, s_len):
    n_batch, _, d = xs.shape
    tr = _pick(s_len, 512, 8)
    return pl.pallas_call(
        _finalnorm_kernel,
        grid=(n_batch, s_len // tr),
        in_specs=[pl.BlockSpec((1, tr, d), lambda b, i: (b, i, 0)), pl.BlockSpec((1, d), lambda b, i: (0, 0))],
        out_specs=pl.BlockSpec((1, tr, d), lambda b, i: (b, i, 0)),
        out_shape=jax.ShapeDtypeStruct((n_batch, s_len, d), F32),
        compiler_params=_cp(("parallel", "parallel")),
        name="final_norm",
    )(xs, final_g.reshape(1, d))


def kernel(x, c, ctx, c_ctx, norm_g, w_mod, b_mod, w_in, q_norm_g, k_norm_g, hy_conv_w, hy_conv_b, hy_w1, hy_b1, hy_freq1, hy_w2, hy_b2, hy_freq2, hy_w3, hy_d, dn_conv_w, dn_a_log, dn_dt_bias, dn_norm_g, w_pa, w_pb, w_pc, w_out, final_g):
    n_batch, s_len, d = x.shape
    ctx_len = ctx.shape[1]
    r_len = s_len + ctx_len
    depth = w_in.shape[0]
    assert n_batch + 1 <= 8 and w_in.shape[2] == OFF_DL + N_LOGITS + N_BRANCH * d

    xs = jnp.concatenate([x, ctx], axis=1)
    cs = jnp.zeros((8, d), F32).at[:n_batch].set(c).at[n_batch].set(c_ctx)
    mods = _mod_call(cs, w_mod, b_mod)
    cos_t, sin_t = _rope_tables(s_len, ctx_len)
    w1p = jnp.pad(hy_w1, ((0, 0), (0, LANE - HY_EMB), (0, 0)))
    w_in_bf = _pad_w_in(w_in)

    for layer in range(depth):
        need_ctx = layer < depth - 1
        xs2 = xs.reshape(n_batch * r_len, d)
        h = _modnorm_call(xs2, norm_g[layer], mods[layer], n_batch=n_batch, r_len=r_len, s_len=s_len)
        p2 = _inproj_call(h, w_in_bf, layer)
        p3 = p2.reshape(n_batch, r_len, -1)

        ya = _attn_call(p3, cos_t, sin_t, q_norm_g[layer], k_norm_g[layer], s_len=s_len)

        filt = (w1p[layer], hy_b1[layer], hy_freq1[layer], hy_w2[layer], hy_b2[layer], hy_freq2[layer],
                hy_w3[layer])
        yb = _hyena_mixer(p3, 0, s_len, hy_conv_w[layer], hy_conv_b[layer], hy_d[layer], filt)
        if need_ctx:
            yb_c = _hyena_mixer(p3, s_len // ctx_len, ctx_len, hy_conv_w[layer], hy_conv_b[layer],
                                hy_d[layer], filt)
        else:
            yb_c = jnp.zeros((n_batch, ctx_len, HY_W), BF16)
        yb = jnp.concatenate([yb, yb_c], axis=1)

        yc = _deltanet_mixer(p3, dn_conv_w[layer], dn_a_log[layer], dn_dt_bias[layer], dn_norm_g[layer],
                             s_len=s_len)

        rt = n_batch * r_len
        m = _merge_call(ya.reshape(rt, ATT_W), yb.reshape(rt, HY_W), yc.reshape(rt, DN_V_W),
                        w_pa, w_pb, w_pc, layer, p2, r_len=r_len)
        xs = _outproj_call(m, w_out, layer, xs2, mods[layer],
                           n_batch=n_batch, r_len=r_len, s_len=s_len).reshape(n_batch, r_len, d)

    return _finalnorm_call(xs, final_g, s_len=s_len)
```

```python
import functools
import math

import jax
import jax.numpy as jnp
import numpy as np
from jax import lax
from jax.experimental import pallas as pl
from jax.experimental.pallas import tpu as pltpu

F32 = jnp.float32
BF16 = jnp.bfloat16

GRID_W = 64
NORM_EPS = 1e-6
N_BRANCH = 3

ATT_HEADS = 8
ATT_KV_HEADS = 2
HEAD_DIM = 128
ATT_GROUP = ATT_HEADS // ATT_KV_HEADS
ATT_W = ATT_HEADS * HEAD_DIM
ATT_KV_W = ATT_KV_HEADS * HEAD_DIM
AXIS_ROPE_DIM = HEAD_DIM // 2
ROPE_THETA = 10000.0

HY_W = 1024
HY_ORDER = 2
HY_BANDS = 16
HY_EMB = 1 + 2 * HY_BANDS
HY_FILTER_HIDDEN = 64
HY_DECAY_TARGET = 1e-2
HY_DECAY_FAST = 0.3
HY_DECAY_SLOW = 1.5

DN_QK_HEADS = 4
DN_V_HEADS = 8
DN_HEAD_DIM = 128
DN_QK_W = DN_QK_HEADS * DN_HEAD_DIM
DN_V_W = DN_V_HEADS * DN_HEAD_DIM
DN_CHUNK = 64
DN_W = 2 * DN_QK_W + DN_V_W

LANE = 128
MIB = 1024 * 1024

OFF_AQ = 0
OFF_AK = OFF_AQ + ATT_W
OFF_AV = OFF_AK + ATT_KV_W
OFF_AG = OFF_AV + ATT_KV_W
OFF_HV = OFF_AG + ATT_W
OFF_HX1 = OFF_HV + HY_W
OFF_HX2 = OFF_HX1 + HY_W
OFF_HG = OFF_HX2 + HY_W
OFF_DQ = OFF_HG + HY_W
OFF_DV = OFF_DQ + 2 * DN_QK_W
OFF_DZ = OFF_DV + DN_V_W
OFF_DL = OFF_DZ + DN_V_W
N_LOGITS = 4 * DN_V_HEADS
MERGE_ALIGN = 512
OFF_MG = -(-(OFF_DL + N_LOGITS) // MERGE_ALIGN) * MERGE_ALIGN


def _pick(n, cap, mult=LANE):
    best = None
    for t in range(mult, min(n, cap) + 1, mult):
        if n % t == 0:
            best = t
    assert best is not None, (n, cap, mult)
    return best


def _cp(sem, vmem_mib=48):
    return pltpu.CompilerParams(dimension_semantics=sem, vmem_limit_bytes=vmem_mib * MIB)


def _silu(x):
    return x * jax.nn.sigmoid(x)


def _split3(x):
    x1 = x.astype(BF16)
    r = x - x1.astype(F32)
    x2 = r.astype(BF16)
    x3 = (r - x2.astype(F32)).astype(BF16)
    return x1, x2, x3


def _dot(a, b):
    return jnp.dot(a, b, preferred_element_type=F32)


def _dot_hp(a, b):
    a1 = a.astype(BF16)
    a2 = (a - a1.astype(F32)).astype(BF16)
    b1 = b.astype(BF16)
    b2 = (b - b1.astype(F32)).astype(BF16)
    return _dot(a1, b1) + (_dot(a1, b2) + _dot(a2, b1))


def _mod_kernel(cs_ref, w_ref, b_ref, o_ref):
    cs = cs_ref[...]
    o_ref[0] = _dot(_silu(cs).astype(BF16), w_ref[0].astype(BF16)) + b_ref[0]


def _mod_call(cs, w_mod, b_mod):
    depth, d, d3 = w_mod.shape
    tn = _pick(d3, 512)
    return pl.pallas_call(
        _mod_kernel,
        grid=(depth, d3 // tn),
        in_specs=[pl.BlockSpec((8, d), lambda l, j: (0, 0)),
                  pl.BlockSpec((1, d, tn), lambda l, j: (l, 0, j)),
                  pl.BlockSpec((1, 1, tn), lambda l, j: (l, 0, j))],
        out_specs=pl.BlockSpec((1, 8, tn), lambda l, j: (l, 0, j)),
        out_shape=jax.ShapeDtypeStruct((depth, 8, d3), F32),
        compiler_params=_cp(("parallel", "parallel")),
        name="adaln_mod",
    )(cs, w_mod, b_mod.reshape(depth, 1, d3))


def _row_mods(mod_ref, i, r0, rows, tm, tiles_per_batch, s_len, n_batch):
    b = i // tiles_per_batch
    row = (i % tiles_per_batch) * tm + r0 + lax.broadcasted_iota(jnp.int32, (rows, 1), 0)
    return row >= s_len, mod_ref[pl.ds(b, 1), :], mod_ref[n_batch:n_batch + 1, :]


NORM_ROWS = 64


def _modnorm_kernel(x_ref, g_ref, mod_ref, h_ref, *, tm, tiles_per_batch, s_len, n_batch, d):
    i = pl.program_id(0)

    def body(c, carry):
        r0 = pl.multiple_of(c * NORM_ROWS, NORM_ROWS)
        x = x_ref[pl.ds(r0, NORM_ROWS), :]
        y = x * lax.rsqrt(jnp.mean(x * x, axis=-1, keepdims=True) + NORM_EPS) * g_ref[...]
        is_ctx, ml, mc = _row_mods(mod_ref, i, r0, NORM_ROWS, tm, tiles_per_batch, s_len, n_batch)
        shift = jnp.where(is_ctx, mc[:, :d], ml[:, :d])
        scale = jnp.where(is_ctx, mc[:, d:2 * d], ml[:, d:2 * d])
        h_ref[pl.ds(r0, NORM_ROWS), :] = (y * (1.0 + scale) + shift).astype(BF16)
        return carry
    lax.fori_loop(0, tm // NORM_ROWS, body, 0)


def _modnorm_call(xs2, norm_g, mods, *, n_batch, r_len, s_len):
    rt, d = xs2.shape
    tm = _pick(r_len, 576, NORM_ROWS)
    kern = functools.partial(_modnorm_kernel, tm=tm, tiles_per_batch=r_len // tm, s_len=s_len,
                             n_batch=n_batch, d=d)
    return pl.pallas_call(
        kern,
        grid=(rt // tm,),
        in_specs=[pl.BlockSpec((tm, d), lambda i: (i, 0)),
                  pl.BlockSpec((1, d), lambda i: (0, 0)),
                  pl.BlockSpec((8, 3 * d), lambda i: (0, 0))],
        out_specs=pl.BlockSpec((tm, d), lambda i: (i, 0)),
        out_shape=jax.ShapeDtypeStruct((rt, d), BF16),
        compiler_params=_cp(("parallel",)),
        name="mod_norm",
    )(xs2, norm_g.reshape(1, d), mods)


def _inproj_kernel(h_ref, wa_ref, wb_ref, o_ref, *, n_a):
    j = pl.program_id(1)

    @pl.when(j < n_a)
    def _():
        o_ref[...] = _dot(h_ref[...], wa_ref[0])

    @pl.when(j >= n_a)
    def _():
        o_ref[...] = _dot(h_ref[...], wb_ref[0])


def _split_w_in(w_in):
    cut = OFF_DL + N_LOGITS
    head = jnp.pad(w_in[..., :cut].astype(BF16), ((0, 0), (0, 0), (0, OFF_MG - cut)))
    return head, w_in[..., cut:].astype(BF16)


def _inproj_call(h, w_head, w_gate, layer):
    rt, d = h.shape
    na, nb = w_head.shape[2], w_gate.shape[2]
    tm = _pick(rt, 2176, 16)
    tn = math.gcd(_pick(na, 512), _pick(nb, 512))
    n_a = na // tn
    return pl.pallas_call(
        functools.partial(_inproj_kernel, n_a=n_a),
        grid=(rt // tm, (na + nb) // tn),
        in_specs=[pl.BlockSpec((tm, d), lambda i, j: (i, 0)),
                  pl.BlockSpec((1, d, tn), lambda i, j: (layer, 0, jnp.minimum(j, n_a - 1))),
                  pl.BlockSpec((1, d, tn), lambda i, j: (layer, 0, jnp.maximum(j - n_a, 0)))],
        out_specs=pl.BlockSpec((tm, tn), lambda i, j: (i, j)),
        out_shape=jax.ShapeDtypeStruct((rt, na + nb), F32),
        compiler_params=_cp(("parallel", "arbitrary"), 56),
        name="in_proj",
    )(h, w_head, w_gate)


def _rope_tables(s_len, ctx_len):
    m = AXIS_ROPE_DIM // 2
    inv_freq = ROPE_THETA ** (-np.arange(0, AXIS_ROPE_DIM, 2, dtype=np.float64) / AXIS_ROPE_DIM)
    t = np.arange(s_len)
    ang_r = (t // GRID_W)[:, None] * inv_freq
    ang_c = (t % GRID_W)[:, None] * inv_freq
    cos = np.concatenate([np.cos(ang_r)] * 2 + [np.cos(ang_c)] * 2, axis=-1)
    sin = np.concatenate([-np.sin(ang_r), np.sin(ang_r), -np.sin(ang_c), np.sin(ang_c)], axis=-1)
    assert cos.shape[1] == 4 * m == HEAD_DIM
    cos = np.concatenate([cos, np.ones((ctx_len, HEAD_DIM))], axis=0)
    sin = np.concatenate([sin, np.zeros((ctx_len, HEAD_DIM))], axis=0)
    return jnp.asarray(cos, F32), jnp.asarray(sin, F32)


def _norm_rope(x, g, cs, sn):
    y = x * lax.rsqrt(jnp.mean(x * x, axis=-1, keepdims=True) + NORM_EPS) * g
    lane = lax.broadcasted_iota(jnp.int32, (1, HEAD_DIM), 1)
    first = (lane % AXIS_ROPE_DIM) < (AXIS_ROPE_DIM // 2)
    q = AXIS_ROPE_DIM // 2
    partner = jnp.where(first, pltpu.roll(y, HEAD_DIM - q, 1), pltpu.roll(y, q, 1))
    return y * cs + partner * sn


def _attn_kernel(q_ref, k_ref, v_ref, gt_ref, cos_ref, sin_ref, qg_ref, kg_ref, o_ref, ks_ref, vs_ref,
                 *, tq, r_len, s_len, kv_chunk):
    i = pl.program_id(2)

    @pl.when(i == 0)
    def _():
        def body(c, carry):
            r0 = pl.multiple_of(c * kv_chunk, kv_chunk)
            kk = _norm_rope(k_ref[0, pl.ds(r0, kv_chunk), :], kg_ref[...],
                            cos_ref[pl.ds(r0, kv_chunk), :], sin_ref[pl.ds(r0, kv_chunk), :])
            ks_ref[pl.ds(r0, kv_chunk), :] = kk.astype(BF16)
            vs_ref[pl.ds(r0, kv_chunk), :] = v_ref[0, pl.ds(r0, kv_chunk), :].astype(BF16)
            return carry
        lax.fori_loop(0, r_len // kv_chunk, body, 0)

    r0 = pl.multiple_of(i * tq, tq)
    cs = cos_ref[pl.ds(r0, tq), :]
    sn = sin_ref[pl.ds(r0, tq), :]
    scale = HEAD_DIM ** -0.5

    def heads(k_lo, k_hi):
        for g in range(ATT_GROUP):
            sl = slice(g * HEAD_DIM, (g + 1) * HEAD_DIM)
            qh = (_norm_rope(q_ref[0, :, sl], qg_ref[...], cs, sn) * scale).astype(BF16)
            s = lax.dot_general(qh, ks_ref[k_lo:k_hi, :], (((1,), (1,)), ((), ())),
                                preferred_element_type=F32)
            e = jnp.exp(s - jnp.max(s, axis=-1, keepdims=True))
            den = jnp.sum(e, axis=-1, keepdims=True)
            o = _dot(e.astype(BF16), vs_ref[k_lo:k_hi, :]) / den
            o_ref[0, :, sl] = (o * _silu(gt_ref[0, :, sl])).astype(o_ref.dtype)

    @pl.when(i < s_len // tq)
    def _():
        heads(0, r_len)

    @pl.when(i >= s_len // tq)
    def _():
        heads(s_len, r_len)


def _attn_call(p3, cos_t, sin_t, q_g, k_g, *, s_len):
    n_batch, r_len, _ = p3.shape
    ctx_len = r_len - s_len
    tq = 256 if (ctx_len % 256 == 0 and s_len % 256 == 0) else 128
    assert ctx_len % tq == 0 and s_len % tq == 0
    gw = ATT_GROUP * HEAD_DIM
    kern = functools.partial(_attn_kernel, tq=tq, r_len=r_len, s_len=s_len, kv_chunk=tq)
    return pl.pallas_call(
        kern,
        grid=(n_batch, ATT_KV_HEADS, r_len // tq),
        in_specs=[pl.BlockSpec((1, tq, gw), lambda b, h, i: (b, i, OFF_AQ // gw + h)),
                  pl.BlockSpec((1, r_len, HEAD_DIM), lambda b, h, i: (b, 0, OFF_AK // HEAD_DIM + h)),
                  pl.BlockSpec((1, r_len, HEAD_DIM), lambda b, h, i: (b, 0, OFF_AV // HEAD_DIM + h)),
                  pl.BlockSpec((1, tq, gw), lambda b, h, i: (b, i, OFF_AG // gw + h)),
                  pl.BlockSpec((r_len, HEAD_DIM), lambda b, h, i: (0, 0)),
                  pl.BlockSpec((r_len, HEAD_DIM), lambda b, h, i: (0, 0)),
                  pl.BlockSpec((1, HEAD_DIM), lambda b, h, i: (0, 0)),
                  pl.BlockSpec((1, HEAD_DIM), lambda b, h, i: (0, 0))],
        out_specs=pl.BlockSpec((1, tq, gw), lambda b, h, i: (b, i, h)),
        out_shape=jax.ShapeDtypeStruct((n_batch, r_len, ATT_W), BF16),
        scratch_shapes=[pltpu.VMEM((r_len, HEAD_DIM), BF16), pltpu.VMEM((r_len, HEAD_DIM), BF16)],
        compiler_params=_cp(("parallel", "parallel", "arbitrary")),
        name="gqa_attention",
    )(p3, p3, p3, p3, cos_t, sin_t, q_g.reshape(1, HEAD_DIM), k_g.reshape(1, HEAD_DIM))


def _conv3(x, w, first, last):
    n = x.shape[0]
    prev = jnp.where(first, 0.0, pltpu.roll(x, 1, 0))
    nxt = jnp.where(last, 0.0, pltpu.roll(x, n - 1, 0))
    return prev * w[0:1, :] + x * w[1:2, :] + nxt * w[2:3, :]


FFT_UNROLL = 16


def _fft_split(n):
    if n <= 512:
        return 1, n
    n2 = 64
    return n // n2, n2


@functools.lru_cache(maxsize=None)
def _fft_mats(n, n1, n2):
    half = n2 // 2
    a = np.arange(n1)[:, None, None]
    k2 = np.arange(n2)[None, :, None]

    def g(bs):
        ph = (k2 * (a + n1 * bs[None, None, :])) % n
        ang = -2.0 * np.pi * ph / n
        return np.cos(ang), np.sin(ang)

    gre, gim = g(np.arange(half))
    gal = np.concatenate([gre, gim], axis=1)
    gar = np.concatenate([-gim, gre], axis=1)
    hre = np.swapaxes(gre, 1, 2) / n
    him = -np.swapaxes(gim, 1, 2) / n
    hal = np.concatenate([hre, him], axis=1)
    har = np.concatenate([-him, hre], axis=1)
    fre_f, fim_f = g(np.arange(n2))
    gf = np.concatenate([fre_f, fim_f], axis=1)
    k1 = np.arange(n1)
    ang1 = -2.0 * np.pi * ((k1[:, None] * k1[None, :]) % n1) / n1
    f1re, f1im = np.cos(ang1), np.sin(ang1)
    fbl = np.concatenate([f1re, f1im], axis=0)
    fbr = np.concatenate([-f1im, f1re], axis=0)
    fbil = np.concatenate([f1re, -f1im], axis=0)
    fbir = np.concatenate([f1im, f1re], axis=0)
    cast = lambda m: jnp.asarray(m, BF16)
    return dict(ga=cast(np.concatenate([gal, gar], axis=2)), ha=cast(np.concatenate([hal, har], axis=2)),
                gf=cast(gf), fb=cast(np.concatenate([fbl, fbr], axis=1)),
                fbi=cast(np.concatenate([fbil, fbir], axis=1)))


@functools.lru_cache(maxsize=None)
def _hy_tables(n_tok):
    pos = np.arange(n_tok, dtype=np.float64)
    t = pos / max(n_tok - 1, 1)
    bands = np.linspace(1e-4, HY_BANDS - 1, HY_BANDS)
    ang = (2.0 * np.pi / n_tok) * pos[:, None] * bands
    z = np.concatenate([t[:, None], np.cos(ang), np.sin(ang)], axis=-1)
    zrev = np.zeros_like(z)
    zrev[1:] = z[:0:-1]
    ztab = np.zeros((2 * n_tok, LANE))
    ztab[:, :HY_EMB] = np.concatenate([z, zrev], axis=0)
    deltas = np.abs(np.linspace(math.log(HY_DECAY_TARGET) / HY_DECAY_SLOW,
                                math.log(HY_DECAY_TARGET) / HY_DECAY_FAST, HY_W))
    return jnp.asarray(ztab, F32), jnp.asarray(np.tile(deltas, HY_ORDER)[None, :], F32)


SLAB_PAD = 8


def _hyfilt_kernel(z_ref, w1_ref, b1_ref, f1_ref, w2_ref, b2_ref, f2_ref, w3_ref, dl_ref, k_ref, s_ref,
                   *, n_tok, tr, slab_rows):
    i = pl.program_id(0)
    z = z_ref[...]
    h = jnp.sin(f1_ref[...] * (_dot_hp(z, w1_ref[...]) + b1_ref[...]))
    h = jnp.sin(f2_ref[...] * (_dot_hp(h, w2_ref[...]) + b2_ref[...]))
    h = _dot(h.astype(BF16), w3_ref[...].astype(BF16)) * jnp.exp(-z[:, 0:1] * dl_ref[...])
    row = i * tr + lax.broadcasted_iota(jnp.int32, (tr, 1), 0)
    h = jnp.where(row == n_tok, 0.0, h)
    if slab_rows is None:
        k_ref[...] = h
    else:
        step = slab_rows + SLAB_PAD
        for s in range(tr // slab_rows):
            k_ref[s * step:s * step + slab_rows, :] = h[s * slab_rows:(s + 1) * slab_rows]
            k_ref[s * step + slab_rows:(s + 1) * step, :] = jnp.zeros((SLAB_PAD, h.shape[1]), F32)

    @pl.when(i == 0)
    def _():
        s_ref[...] = jnp.zeros_like(s_ref)

    s_ref[...] += jnp.sum(jnp.abs(h), axis=0, keepdims=True)


def _hyfilt_call(n_tok, w1p, b1, f1, w2, b2, f2, w3):
    ztab, dl = _hy_tables(n_tok)
    n = 2 * n_tok
    tr = _pick(n_tok, 512, 8)
    ow = HY_ORDER * HY_W
    hid = HY_FILTER_HIDDEN
    n1, _ = _fft_split(n)
    slab_rows = n1 if n1 > 1 else None
    assert slab_rows is None or tr % slab_rows == 0
    out_tr = tr if slab_rows is None else (tr // slab_rows) * (slab_rows + SLAB_PAD)
    kern = functools.partial(_hyfilt_kernel, n_tok=n_tok, tr=tr, slab_rows=slab_rows)
    c2 = lambda i: (0, 0)
    return pl.pallas_call(
        kern,
        grid=(n // tr,),
        in_specs=[pl.BlockSpec((tr, LANE), lambda i: (i, 0)),
                  pl.BlockSpec((LANE, hid), c2), pl.BlockSpec((1, hid), c2), pl.BlockSpec((1, hid), c2),
                  pl.BlockSpec((hid, hid), c2), pl.BlockSpec((1, hid), c2), pl.BlockSpec((1, hid), c2),
                  pl.BlockSpec((hid, ow), lambda i: (0, i // (n_tok // tr))),
                  pl.BlockSpec((1, ow), c2)],
        out_specs=[pl.BlockSpec((out_tr, ow), lambda i: (i, 0)), pl.BlockSpec((1, ow), c2)],
        out_shape=[jax.ShapeDtypeStruct((n // tr * out_tr, ow), F32), jax.ShapeDtypeStruct((1, ow), F32)],
        compiler_params=_cp(("arbitrary",)),
        name="hyena_filter",
    )(ztab, w1p, b1.reshape(1, hid), f1.reshape(1, hid), w2, b2.reshape(1, hid), f2.reshape(1, hid), w3, dl)


FFT_GROUP = 8


def _filtfft_kernel(k_ref, s_ref, gf_ref, fb_ref, o_ref, z_ref, *, n, n1, n2):
    inv = 1.0 / (s_ref[...] + 1e-6)
    if n1 == 1:
        o_ref[...] = _dot(gf_ref[0], k_ref[...].astype(BF16)) * inv
        return
    slab = 2 * n1
    zs, ks = slab + SLAB_PAD, n1 + SLAB_PAD

    def stage_a(a, carry):
        out = _dot(gf_ref[a], k_ref[pl.ds(a, n2, stride=ks), :].astype(BF16))
        z_ref[pl.ds(a, n2, stride=zs), :] = out[:n2]
        z_ref[pl.ds(n1 + a, n2, stride=zs), :] = out[n2:]
        return carry
    lax.fori_loop(0, n1, stage_a, 0, unroll=FFT_UNROLL)

    def stage_b(g, carry):
        for u in range(FFT_GROUP):
            k2 = g * FFT_GROUP + u
            z = z_ref[pl.ds(pl.multiple_of(k2 * zs, 8), slab), :].astype(BF16)
            o_ref[pl.ds(pl.multiple_of(k2 * slab, slab), slab), :] = _dot(fb_ref[...], z) * inv
        return carry
    lax.fori_loop(0, n2 // FFT_GROUP, stage_b, 0)


def _filtfft_call(k_un, asum, n_tok):
    k_rows, ow = k_un.shape
    n = 2 * n_tok
    n1, n2 = _fft_split(n)
    assert k_rows == (n if n1 == 1 else n2 * (n1 + SLAB_PAD))
    m = _fft_mats(n, n1, n2)
    cb = LANE
    kern = functools.partial(_filtfft_kernel, n=n, n1=n1, n2=n2)
    return pl.pallas_call(
        kern,
        grid=(ow // cb,),
        in_specs=[pl.BlockSpec((k_rows, cb), lambda j: (0, j)),
                  pl.BlockSpec((1, cb), lambda j: (0, j)),
                  pl.BlockSpec(m["gf"].shape, lambda j: (0, 0, 0)),
                  pl.BlockSpec(m["fb"].shape, lambda j: (0, 0))],
        out_specs=pl.BlockSpec((2 * n, cb), lambda j: (0, j)),
        out_shape=jax.ShapeDtypeStruct((2 * n, ow), F32),
        scratch_shapes=[pltpu.VMEM((n2 * (2 * n1 + SLAB_PAD) if n1 > 1 else 8, cb), F32)],
        compiler_params=_cp(("parallel",)),
        name="hyena_filter_fft",
    )(k_un, asum, m["gf"], m["fb"])


def _hyena_kernel(*refs, n_tok, n1, n2, conv_a, has_gate):
    it = iter(refs)
    a_ref, m_ref = next(it), next(it)
    g_ref = next(it) if has_gate else None
    if conv_a:
        cwa_ref, cba_ref = next(it), next(it)
    cwm_ref, cbm_ref, d_ref, ks_ref = next(it), next(it), next(it), next(it)
    ga_ref, ha_ref = next(it), next(it)
    if n1 > 1:
        fb_ref, fbi_ref = next(it), next(it)
    o_ref, ac_ref, z_ref = next(it), next(it), next(it)

    n = 2 * n_tok
    half = n2 // 2
    row = lax.broadcasted_iota(jnp.int32, (n_tok, 1), 0)
    first, last = row == 0, row == n_tok - 1
    sr = n1 if n1 > 1 else n_tok
    astep = sr + SLAB_PAD if n1 > 1 else sr
    for b in range(2):
        a = a_ref[b]
        if conv_a:
            a = _conv3(a, cwa_ref[...], first, last) + cba_ref[...]
        for s in range(n_tok // sr):
            ac_ref[b, s * astep:s * astep + sr, :] = a[s * sr:(s + 1) * sr]

    def spectrum_mul(x, k, h):
        xre, xim, kre, kim = x[:h], x[h:], k[:h], k[h:]
        return jnp.concatenate([xre * kre - xim * kim, xre * kim + xim * kre], axis=0).astype(BF16)

    if n1 == 1:
        x = _dot(ga_ref[0], jnp.concatenate([ac_ref[0], ac_ref[1]], axis=0).astype(BF16))
        y = _dot(ha_ref[0], spectrum_mul(x, ks_ref[...], n))
        for b in range(2):
            ac_ref[b] = y[b * n_tok:(b + 1) * n_tok] + ac_ref[b] * d_ref[0]
    else:
        slab = 2 * n1
        zs = slab + SLAB_PAD

        def stage_a(a, carry):
            rows = jnp.concatenate([ac_ref[0, pl.ds(a, half, stride=astep), :],
                                    ac_ref[1, pl.ds(a, half, stride=astep), :]], axis=0).astype(BF16)
            out = _dot(ga_ref[a], rows)
            z_ref[pl.ds(a, n2, stride=zs), :] = out[:n2]
            z_ref[pl.ds(n1 + a, n2, stride=zs), :] = out[n2:]
            return carry
        lax.fori_loop(0, n1, stage_a, 0, unroll=FFT_UNROLL)

        def stage_b(g, carry):
            k2s = [g * FFT_GROUP + u for u in range(FFT_GROUP)]
            zrows = [pl.ds(pl.multiple_of(k2 * zs, 8), slab) for k2 in k2s]
            xs = [_dot(fb_ref[...], z_ref[zr, :].astype(BF16)) for zr in zrows]
            ys = [spectrum_mul(x, ks_ref[pl.ds(pl.multiple_of(k2 * slab, slab), slab), :], n1)
                  for x, k2 in zip(xs, k2s)]
            for y, zr in zip(ys, zrows):
                z_ref[zr, :] = _dot(fbi_ref[...], y)
            return carry
        lax.fori_loop(0, n2 // FFT_GROUP, stage_b, 0)

        def stage_a_inv(a, carry):
            rows = jnp.concatenate([z_ref[pl.ds(a, n2, stride=zs), :],
                                    z_ref[pl.ds(n1 + a, n2, stride=zs), :]], axis=0).astype(BF16)
            y = _dot(ha_ref[a], rows)
            for b in range(2):
                cur = ac_ref[b, pl.ds(a, half, stride=astep), :]
                ac_ref[b, pl.ds(a, half, stride=astep), :] = y[b * half:(b + 1) * half] + cur * d_ref[0]
            return carry
        lax.fori_loop(0, n1, stage_a_inv, 0, unroll=FFT_UNROLL)

    for b in range(2):
        mc = _conv3(m_ref[b], cwm_ref[...], first, last) + cbm_ref[...]
        for s in range(n_tok // sr):
            rows = slice(s * sr, (s + 1) * sr)
            out = mc[rows] * ac_ref[b, s * astep:s * astep + sr, :]
            if has_gate:
                out = out * _silu(g_ref[b, rows, :])
            o_ref[b, rows, :] = out.astype(o_ref.dtype)


def _hyena_call(a_arr, a_blk, m_arr, m_blk, gate, conv_a, conv_m, d_row, spec, spec_blk, *, n_tok, out_dtype):
    n_batch = a_arr.shape[0]
    assert n_batch % 2 == 0
    n = 2 * n_tok
    n1, n2 = _fft_split(n)
    m = _fft_mats(n, n1, n2)
    cb = LANE
    one = pl.Buffered(1)
    tile_mode, spec_mode = (one, pl.Buffered(2)) if gate is not None else (pl.Buffered(2), one)

    def tile(blk):
        return pl.BlockSpec((2, n_tok, cb), lambda p, j, blk=blk: (p, blk[0], blk[1] + j), pipeline_mode=tile_mode)

    def convspec(c0):
        return [pl.BlockSpec((3, cb), lambda p, j, c0=c0: (0, c0 + j)),
                pl.BlockSpec((1, cb), lambda p, j, c0=c0: (0, c0 + j))]

    ins, specs = [a_arr, m_arr], [tile(a_blk), tile(m_blk)]
    if gate is not None:
        ins.append(gate[0])
        specs.append(tile(gate[1]))
    if conv_a is not None:
        ins += [conv_a[0], conv_a[1]]
        specs += convspec(conv_a[2])
    ins += [conv_m[0], conv_m[1], d_row, spec]
    specs += convspec(conv_m[2])
    specs += [pl.BlockSpec((1, 1, cb), lambda p, j: (0, 0, j)),
              pl.BlockSpec((2 * n, cb), lambda p, j, s0=spec_blk: (0, s0 + j), pipeline_mode=spec_mode)]
    names = ["ga", "ha"] + (["fb", "fbi"] if n1 > 1 else [])
    for nm in names:
        ins.append(m[nm])
        specs.append(pl.BlockSpec(m[nm].shape, lambda p, j, nd=m[nm].ndim: (0,) * nd, pipeline_mode=one))
    kern = functools.partial(_hyena_kernel, n_tok=n_tok, n1=n1, n2=n2, conv_a=conv_a is not None,
                             has_gate=gate is not None)
    return pl.pallas_call(
        kern,
        grid=(n_batch // 2, HY_W // cb),
        in_specs=specs,
        out_specs=pl.BlockSpec((2, n_tok, cb), lambda p, j: (p, 0, j)),
        out_shape=jax.ShapeDtypeStruct((n_batch, n_tok, HY_W), out_dtype),
        scratch_shapes=[pltpu.VMEM((2, n_tok // n1 * (n1 + SLAB_PAD) if n1 > 1 else n_tok, cb), F32),
                        pltpu.VMEM((n2 * (2 * n1 + SLAB_PAD) if n1 > 1 else 8, cb), F32)],
        compiler_params=_cp(("parallel", "arbitrary"), 56),
        name="hyena_conv",
    )(*ins)


def _hyena_mixer(p3, row_blk, n_tok, conv_w, conv_b, hy_d, filt):
    k_un, asum = _hyfilt_call(n_tok, *filt)
    spec = _filtfft_call(k_un, asum, n_tok)
    cbias = conv_b.reshape(1, -1)
    cblk = HY_W // LANE
    d3 = hy_d.reshape(HY_ORDER, 1, HY_W)
    z = _hyena_call(p3, (row_blk, OFF_HV // LANE), p3, (row_blk, OFF_HX1 // LANE), None,
                    (conv_w, cbias, 0), (conv_w, cbias, cblk), d3[0:1], spec, 0,
                    n_tok=n_tok, out_dtype=F32)
    return _hyena_call(z, (0, 0), p3, (row_blk, OFF_HX2 // LANE), (p3, (row_blk, OFF_HG // LANE)),
                       None, (conv_w, cbias, 2 * cblk), d3[1:2], spec, cblk,
                       n_tok=n_tok, out_dtype=BF16)


def _softplus(x):
    return jnp.maximum(x, 0.0) + jnp.log1p(jnp.exp(-jnp.abs(x)))


FLIP_ROWS = 256


def _exchange_matrix():
    return jnp.asarray(np.eye(FLIP_ROWS)[::-1], BF16)


def _flip_rows(jm, x):
    x1, x2, x3 = _split3(x)
    return _dot(jm, x1) + (_dot(jm, x2) + _dot(jm, x3))


def _dnprep_kernel(x_ref, w_ref, jm_ref, o_ref, *, r_len, s_len):
    j = pl.program_id(1)
    row = lax.broadcasted_iota(jnp.int32, (r_len, 1), 0)
    first = (row == 0) | (row == s_len)
    last = (row == s_len - 1) | (row == r_len - 1)
    u = _silu(_conv3(x_ref[0], w_ref[...], first, last))
    nrm = u * lax.rsqrt(jnp.sum(u * u, axis=-1, keepdims=True) + 1e-6)
    o_ref[0, 0] = jnp.where(j < 2 * DN_QK_HEADS, nrm, u)
    for seg0, seg_len in ((0, s_len), (s_len, r_len - s_len)):
        nt = seg_len // FLIP_ROWS
        for t in range(nt):
            src = seg0 + t * FLIP_ROWS
            dst = seg0 + (nt - 1 - t) * FLIP_ROWS
            o_ref[1, 0, dst:dst + FLIP_ROWS, :] = _flip_rows(jm_ref[...], o_ref[0, 0, src:src + FLIP_ROWS, :])


def _dnprep_call(p3, conv_w, *, s_len):
    n_batch, r_len, _ = p3.shape
    assert s_len % FLIP_ROWS == 0 and (r_len - s_len) % FLIP_ROWS == 0
    kern = functools.partial(_dnprep_kernel, r_len=r_len, s_len=s_len)
    return pl.pallas_call(
        kern,
        grid=(n_batch, DN_W // LANE),
        in_specs=[pl.BlockSpec((1, r_len, LANE), lambda b, j: (b, 0, OFF_DQ // LANE + j)),
                  pl.BlockSpec((3, LANE), lambda b, j: (0, j)),
                  pl.BlockSpec((FLIP_ROWS, FLIP_ROWS), lambda b, j: (0, 0))],
        out_specs=pl.BlockSpec((2, 1, r_len, LANE), lambda b, j: (0, b, 0, j)),
        out_shape=jax.ShapeDtypeStruct((2, n_batch, r_len, DN_W), F32),
        compiler_params=_cp(("parallel", "parallel")),
        name="deltanet_prep",
    )(p3, conv_w, _exchange_matrix())


def _tri_mats(tr):
    idx = np.arange(tr)
    same = (idx[:, None] // DN_CHUNK) == (idx[None, :] // DN_CHUNK)
    low = same & (idx[:, None] >= idx[None, :])
    return jnp.asarray(low, BF16), jnp.asarray(low.T, BF16), jnp.asarray(same, BF16)


def _dnintra_kernel(u_ref, la_ref, lb_ref, lat_ref, alr_ref, dtr_ref, alc_ref, dtc_ref, low_ref, upp_ref,
                    one_ref, a_ref, qk_ref, be_ref, eg_ref, ek_ref, gt_ref, *, tr):
    g_col = -jnp.exp(alr_ref[0]) * _softplus(la_ref[0, 0] + dtr_ref[0])
    beta = jax.nn.sigmoid(lb_ref[0, 0])
    g1, g2, g3 = _split3(g_col)
    gc_col = _dot(low_ref[...], g1) + (_dot(low_ref[...], g2) + _dot(low_ref[...], g3))
    gt_col = _dot(one_ref[...], g1) + (_dot(one_ref[...], g2) + _dot(one_ref[...], g3))
    be_ref[0, 0] = beta
    eg_ref[0, 0] = jnp.exp(gc_col)
    ek_ref[0, 0] = jnp.exp(gt_col - gc_col)
    gt_ref[0, 0] = jnp.exp(gt_col)
    g_row = -jnp.exp(alc_ref[0]) * _softplus(lat_ref[0, 0] + dtc_ref[0])
    r1, r2, r3 = _split3(g_row)
    gc_row = _dot(r1, upp_ref[...]) + (_dot(r2, upp_ref[...]) + _dot(r3, upp_ref[...]))

    ii = lax.broadcasted_iota(jnp.int32, (DN_CHUNK, DN_CHUNK), 0)
    jj = lax.broadcasted_iota(jnp.int32, (DN_CHUNK, DN_CHUNK), 1)
    scale = DN_HEAD_DIM ** -0.5
    nt = (((1,), (1,)), ((), ()))
    for c in range(tr // DN_CHUNK):
        rows = slice(c * DN_CHUNK, (c + 1) * DN_CHUNK)
        for hq in range(DN_QK_HEADS):
            q = u_ref[0, 0, rows, hq * DN_HEAD_DIM:(hq + 1) * DN_HEAD_DIM].astype(BF16)
            k = u_ref[0, 0, rows, DN_QK_W + hq * DN_HEAD_DIM:DN_QK_W + (hq + 1) * DN_HEAD_DIM].astype(BF16)
            kk = lax.dot_general(k, k, nt, preferred_element_type=F32)
            qk = lax.dot_general(q, k, nt, preferred_element_type=F32) * scale
            for h in range(hq * (DN_V_HEADS // DN_QK_HEADS), (hq + 1) * (DN_V_HEADS // DN_QK_HEADS)):
                diff = gc_col[rows, h:h + 1] - gc_row[h:h + 1, rows]
                dec = jnp.where(ii >= jj, jnp.exp(jnp.minimum(diff, 0.0)), 0.0)
                a_ref[0, 0, h, c] = jnp.where(ii > jj, kk * beta[rows, h:h + 1] * dec, 0.0)
                qk_ref[0, 0, h, c] = (qk * dec).astype(qk_ref.dtype)


def _dnintra_call(u2, la, lb, lat, alog, dtb):
    n_dir, n_batch, r_len, _ = u2.shape
    tr = 4 * DN_CHUNK
    nc = r_len // DN_CHUNK
    low, upp, one = _tri_mats(tr)
    kern = functools.partial(_dnintra_kernel, tr=tr)
    pad = lambda v: jnp.pad(v, ((0, 0), (0, LANE - DN_V_HEADS))).reshape(n_dir, 1, LANE)
    col = lambda v: v.reshape(n_dir, DN_V_HEADS, 1)
    cm = lambda d, b, t: (0, 0)
    gspec = pl.BlockSpec((1, 1, tr, LANE), lambda d, b, t: (d, b, t, 0))
    mspec = pl.BlockSpec((1, 1, DN_V_HEADS, tr // DN_CHUNK, DN_CHUNK, DN_CHUNK), lambda d, b, t: (d, b, 0, t, 0, 0))
    gshape = jax.ShapeDtypeStruct((n_dir, n_batch, r_len, LANE), F32)
    mshape = jax.ShapeDtypeStruct((n_dir, n_batch, DN_V_HEADS, nc, DN_CHUNK, DN_CHUNK), F32)
    return pl.pallas_call(
        kern,
        grid=(n_dir, n_batch, r_len // tr),
        in_specs=[pl.BlockSpec((1, 1, tr, 2 * DN_QK_W), lambda d, b, t: (d, b, t, 0)),
                  gspec, gspec,
                  pl.BlockSpec((1, 1, DN_V_HEADS, tr), lambda d, b, t: (d, b, 0, t)),
                  pl.BlockSpec((1, 1, LANE), lambda d, b, t: (d, 0, 0)),
                  pl.BlockSpec((1, 1, LANE), lambda d, b, t: (d, 0, 0)),
                  pl.BlockSpec((1, DN_V_HEADS, 1), lambda d, b, t: (d, 0, 0)),
                  pl.BlockSpec((1, DN_V_HEADS, 1), lambda d, b, t: (d, 0, 0)),
                  pl.BlockSpec((tr, tr), cm), pl.BlockSpec((tr, tr), cm), pl.BlockSpec((tr, tr), cm)],
        out_specs=[mspec, mspec, gspec, gspec, gspec, gspec],
        out_shape=[mshape, jax.ShapeDtypeStruct(mshape.shape, BF16), gshape, gshape, gshape, gshape],
        compiler_params=_cp(("parallel", "parallel", "parallel")),
        name="deltanet_intra",
    )(u2, la, lb, lat, pad(alog), pad(dtb), col(alog), col(dtb), low, upp, one)


def _dnsolve_kernel(a_ref, o_ref, at_ref, tt_ref):
    c = DN_CHUNK
    for blk in range(c * c // LANE):
        at_ref[blk * LANE:(blk + 1) * LANE, :] = a_ref[:, blk * LANE:(blk + 1) * LANE].T
    tt_ref[...] = jnp.zeros_like(tt_ref)
    for i in range(c):
        nr = 8 * (i // 8 + 1)
        rr = lax.broadcasted_iota(jnp.int32, (nr, LANE), 0)
        acc = jnp.where(rr == i, 1.0, 0.0)

        def body(j, acc, i=i, nr=nr):
            a = at_ref[pl.ds(i * c + j, 1), :]
            return acc - a * tt_ref[pl.ds(pl.multiple_of(j * c, c), nr), :]
        if i > 0:
            acc = lax.fori_loop(0, i, body, acc, unroll=min(i, 8))
        tt_ref[i * c:i * c + nr, :] = acc
    for blk in range(c * c // LANE):
        o_ref[:, blk * LANE:(blk + 1) * LANE] = tt_ref[blk * LANE:(blk + 1) * LANE, :].T.astype(o_ref.dtype)


def _dnsolve_call(a2):
    ni, cc = a2.shape
    assert ni % LANE == 0
    return pl.pallas_call(
        _dnsolve_kernel,
        grid=(ni // LANE,),
        in_specs=[pl.BlockSpec((LANE, cc), lambda i: (i, 0))],
        out_specs=pl.BlockSpec((LANE, cc), lambda i: (i, 0)),
        out_shape=jax.ShapeDtypeStruct((ni, cc), BF16),
        scratch_shapes=[pltpu.VMEM((cc, LANE), F32), pltpu.VMEM((cc, LANE), F32)],
        compiler_params=_cp(("parallel",)),
        name="deltanet_solve",
    )(a2)


def _dnscan_kernel(u_ref, t_ref, qk_ref, be_ref, eg_ref, ek_ref, gt_ref, o_ref, *s_refs, n_dir, n_batch):
    @pl.when(pl.program_id(0) == 0)
    def _():
        for s_ref in s_refs:
            s_ref[...] = jnp.zeros_like(s_ref)

    scale = DN_HEAD_DIM ** -0.5
    rep = DN_V_HEADS // DN_QK_HEADS
    streams = [(d, b, h) for d in range(n_dir) for b in range(n_batch) for h in range(DN_V_HEADS)]

    def qkv(d, b, h):
        hq = h // rep
        q = u_ref[d, b, :, hq * DN_HEAD_DIM:(hq + 1) * DN_HEAD_DIM]
        k = u_ref[d, b, :, DN_QK_W + hq * DN_HEAD_DIM:DN_QK_W + (hq + 1) * DN_HEAD_DIM]
        v = u_ref[d, b, :, 2 * DN_QK_W + h * DN_HEAD_DIM:2 * DN_QK_W + (h + 1) * DN_HEAD_DIM]
        return q, k, v

    uws, egs = [], []
    for d, b, h in streams:
        _, k, v = qkv(d, b, h)
        be = be_ref[d, b, :, h:h + 1]
        egs.append(jnp.broadcast_to(eg_ref[d, b, :, h:h + 1], (DN_CHUNK, DN_HEAD_DIM)))
        rhs = jnp.concatenate([v * be, (k * be) * egs[-1]], axis=1).astype(BF16)
        uws.append(_dot(t_ref[d, b, h, 0], rhs))
    wqs = []
    for i, (d, b, h) in enumerate(streams):
        q, _, _ = qkv(d, b, h)
        lhs = jnp.concatenate([uws[i][:, DN_HEAD_DIM:], q * (scale * egs[i])], axis=0)
        wqs.append(_dot(lhs.astype(BF16), s_refs[i][...].astype(BF16)))
    for i, (d, b, h) in enumerate(streams):
        _, k, _ = qkv(d, b, h)
        vnb = (uws[i][:, :DN_HEAD_DIM] - wqs[i][:DN_CHUNK]).astype(BF16)
        o_ref[d, b, :, h * DN_HEAD_DIM:(h + 1) * DN_HEAD_DIM] = (
            wqs[i][DN_CHUNK:] + _dot(qk_ref[d, b, h, 0], vnb))
        kd = (k * ek_ref[d, b, :, h:h + 1]).astype(BF16)
        s_refs[i][...] = s_refs[i][...] * gt_ref[d, b, 0:1, h:h + 1] + lax.dot_general(
            kd, vnb, (((0,), (0,)), ((), ())), preferred_element_type=F32)


def _dnscan_call(u2, t6, qk6, be, eg, ek, gt, *, s_len):
    n_dir, n_batch, r_len, _ = u2.shape
    nc = r_len // DN_CHUNK
    ncl = s_len // DN_CHUNK

    def cidx(t):
        return jnp.where(t < nc - ncl, ncl + t, t - (nc - ncl))

    kern = functools.partial(_dnscan_kernel, n_dir=n_dir, n_batch=n_batch)
    gspec = pl.BlockSpec((n_dir, n_batch, DN_CHUNK, LANE), lambda t: (0, 0, cidx(t), 0))
    mspec = pl.BlockSpec((n_dir, n_batch, DN_V_HEADS, 1, DN_CHUNK, DN_CHUNK), lambda t: (0, 0, 0, cidx(t), 0, 0))
    return pl.pallas_call(
        kern,
        grid=(nc,),
        in_specs=[pl.BlockSpec((n_dir, n_batch, DN_CHUNK, DN_W), lambda t: (0, 0, cidx(t), 0)),
                  mspec, mspec, gspec, gspec, gspec, gspec],
        out_specs=pl.BlockSpec((n_dir, n_batch, DN_CHUNK, DN_V_W), lambda t: (0, 0, cidx(t), 0)),
        out_shape=jax.ShapeDtypeStruct((n_dir, n_batch, r_len, DN_V_W), F32),
        scratch_shapes=[pltpu.VMEM((DN_HEAD_DIM, DN_HEAD_DIM), F32)] * (n_dir * n_batch * DN_V_HEADS),
        compiler_params=_cp(("arbitrary",)),
        name="deltanet_scan",
    )(u2, t6, qk6, be, eg, ek, gt)


DN_OUT_COLS = 512


def _dnout_kernel(of_ref, ob_ref, z_ref, g_ref, jm_ref, o_ref):
    for h in range(DN_OUT_COLS // DN_HEAD_DIM):
        sl = slice(h * DN_HEAD_DIM, (h + 1) * DN_HEAD_DIM)
        o = of_ref[0, 0, :, sl] + _flip_rows(jm_ref[...], ob_ref[0, 0, :, sl])
        y = o * lax.rsqrt(jnp.mean(o * o, axis=-1, keepdims=True) + NORM_EPS) * g_ref[...]
        o_ref[0, :, sl] = (y * _silu(z_ref[0, :, sl])).astype(o_ref.dtype)


def _dnout_call(o2, p3, norm_g, *, s_len):
    _, n_batch, r_len, _ = o2.shape
    tr = FLIP_ROWS
    cw = DN_OUT_COLS
    ns, nc = s_len // tr, (r_len - s_len) // tr
    assert OFF_DZ % cw == 0

    def mirror(i):
        return jnp.where(i < ns, ns - 1 - i, 2 * ns + nc - 1 - i)

    return pl.pallas_call(
        _dnout_kernel,
        grid=(n_batch, r_len // tr, DN_V_W // cw),
        in_specs=[pl.BlockSpec((1, 1, tr, cw), lambda b, i, j: (0, b, i, j)),
                  pl.BlockSpec((1, 1, tr, cw), lambda b, i, j: (1, b, mirror(i), j)),
                  pl.BlockSpec((1, tr, cw), lambda b, i, j: (b, i, OFF_DZ // cw + j)),
                  pl.BlockSpec((1, DN_HEAD_DIM), lambda b, i, j: (0, 0)),
                  pl.BlockSpec((FLIP_ROWS, FLIP_ROWS), lambda b, i, j: (0, 0))],
        out_specs=pl.BlockSpec((1, tr, cw), lambda b, i, j: (b, i, j)),
        out_shape=jax.ShapeDtypeStruct((n_batch, r_len, DN_V_W), BF16),
        compiler_params=_cp(("parallel", "parallel", "parallel")),
        name="deltanet_out",
    )(o2, o2, p3, norm_g.reshape(1, DN_HEAD_DIM), _exchange_matrix())


def _seq_flip(a, s_len, axis):
    lat, ctx = jnp.split(a, [s_len], axis=axis)
    return jnp.concatenate([jnp.flip(lat, axis), jnp.flip(ctx, axis)], axis=axis)


def _deltanet_mixer(p3, conv_w, a_log, dt_bias, norm_g, *, s_len):
    n_batch, r_len, _ = p3.shape
    u2 = _dnprep_call(p3, conv_w, s_len=s_len)
    lg =p3[:, :, OFF_DL:OFF_DL + N_LOGITS].reshape(n_batch, r_len, 2, 2, DN_V_HEADS)

    def dirs(x):
        return jnp.stack([x[:, :, 0], _seq_flip(x[:, :, 1], s_len, 1)])

    la, lb = dirs(lg[:, :, 0]), dirs(lg[:, :, 1])
    padl = lambda x: jnp.pad(x, ((0, 0), (0, 0), (0, 0), (0, LANE - DN_V_HEADS)))
    a6, qk6, be, eg, ek, gt = _dnintra_call(u2, padl(la), padl(lb), jnp.swapaxes(la, 2, 3), a_log, dt_bias)
    t6 = _dnsolve_call(a6.reshape(-1, DN_CHUNK * DN_CHUNK)).reshape(a6.shape)
    o2 = _dnscan_call(u2, t6, qk6, be, eg, ek, gt, s_len=s_len)
    return _dnout_call(o2, p3, norm_g, s_len=s_len)


def _merge_kernel(ya_ref, yb_ref, yc_ref, wa_ref, wb_ref, wc_ref, ga_ref, gb_ref, gc_ref, o_ref,
                  wa_s, wb_s, wc_s):
    @pl.when(pl.program_id(1) == 0)
    def _():
        wa_s[...] = wa_ref[0].astype(BF16)
        wb_s[...] = wb_ref[0].astype(BF16)
        wc_s[...] = wc_ref[0].astype(BF16)

    m = (jax.nn.sigmoid(ga_ref[...]) * _dot(ya_ref[...], wa_s[...])
         + jax.nn.sigmoid(gb_ref[...]) * _dot(yb_ref[...], wb_s[...])
         + jax.nn.sigmoid(gc_ref[...]) * _dot(yc_ref[...], wc_s[...]))
    o_ref[...] = m.astype(o_ref.dtype)


def _merge_call(ya, yb, yc, w_pa, w_pb, w_pc, layer, p2, *, r_len):
    rt = ya.shape[0]
    d = w_pa.shape[2]
    tm = _pick(r_len, 1088, 16)
    tn = _pick(d, MERGE_ALIGN)
    assert OFF_MG % tn == 0
    yspec = lambda w: pl.BlockSpec((tm, w), lambda j, i: (i, 0))
    wspec = lambda w: pl.BlockSpec((1, w, tn), lambda j, i: (layer, 0, j))
    gspec = lambda br: pl.BlockSpec((tm, tn), lambda j, i, br=br: (i, (OFF_MG + br * d) // tn + j))
    return pl.pallas_call(
        _merge_kernel,
        grid=(d // tn, rt // tm),
        in_specs=[yspec(ATT_W), yspec(HY_W), yspec(DN_V_W), wspec(ATT_W), wspec(HY_W), wspec(DN_V_W),
                  gspec(0), gspec(1), gspec(2)],
        out_specs=pl.BlockSpec((tm, tn), lambda j, i: (i, j)),
        out_shape=jax.ShapeDtypeStruct((rt, d), BF16),
        scratch_shapes=[pltpu.VMEM((ATT_W, tn), BF16), pltpu.VMEM((HY_W, tn), BF16),
                        pltpu.VMEM((DN_V_W, tn), BF16)],
        compiler_params=_cp(("parallel", "arbitrary"), 56),
        name="branch_merge",
    )(ya, yb, yc, w_pa, w_pb, w_pc, p2, p2, p2)


def _outproj_kernel(m_ref, w_ref, x_ref, gate_ref, o_ref, w_s, *, tm, tiles_per_batch, s_len, n_batch):
    @pl.when(pl.program_id(1) == 0)
    def _():
        w_s[...] = w_ref[0].astype(BF16)

    is_ctx, gl, gc = _row_mods(gate_ref, pl.program_id(1), 0, tm, tm, tiles_per_batch, s_len, n_batch)
    o_ref[...] = x_ref[...] + jnp.where(is_ctx, gc, gl) * _dot(m_ref[...], w_s[...])


def _outproj_call(m, w_out, layer, xs2, mods, *, n_batch, r_len, s_len):
    rt, d = xs2.shape
    tm = _pick(r_len, 1088, 16)
    tn = _pick(d, 512)
    kern = functools.partial(_outproj_kernel, tm=tm, tiles_per_batch=r_len // tm, s_len=s_len, n_batch=n_batch)
    return pl.pallas_call(
        kern,
        grid=(d // tn, rt // tm),
        in_specs=[pl.BlockSpec((tm, d), lambda j, i: (i, 0)),
                  pl.BlockSpec((1, d, tn), lambda j, i: (layer, 0, j)),
                  pl.BlockSpec((tm, tn), lambda j, i: (i, j)),
                  pl.BlockSpec((8, tn), lambda j, i: (0, 2 * d // tn + j))],
        out_specs=pl.BlockSpec((tm, tn), lambda j, i: (i, j)),
        out_shape=jax.ShapeDtypeStruct((rt, d), F32),
        scratch_shapes=[pltpu.VMEM((d, tn), BF16)],
        compiler_params=_cp(("parallel", "arbitrary")),
        name="out_proj_residual",
    )(m, w_out, xs2, mods)


def _finalnorm_kernel(x_ref, g_ref, o_ref):
    x = x_ref[0]
    o_ref[0] = x * lax.rsqrt(jnp.mean(x * x, axis=-1, keepdims=True) + NORM_EPS) * g_ref[...]


def _finalnorm_call(xs, final_g, *, s_len):
    n_batch, _, d = xs.shape
    tr = _pick(s_len, 512, 8)
    return pl.pallas_call(
        _finalnorm_kernel,
        grid=(n_batch, s_len // tr),
        in_specs=[pl.BlockSpec((1, tr, d), lambda b, i: (b, i, 0)), pl.BlockSpec((1, d), lambda b, i: (0, 0))],
        out_specs=pl.BlockSpec((1, tr, d), lambda b, i: (b, i, 0)),
        out_shape=jax.ShapeDtypeStruct((n_batch, s_len, d), F32),
        compiler_params=_cp(("parallel", "parallel")),
        name="final_norm",
    )(xs, final_g.reshape(1, d))


def kernel(x, c, ctx, c_ctx, norm_g, w_mod, b_mod, w_in, q_norm_g, k_norm_g, hy_conv_w, hy_conv_b, hy_w1, hy_b1, hy_freq1, hy_w2, hy_b2, hy_freq2, hy_w3, hy_d, dn_conv_w, dn_a_log, dn_dt_bias, dn_norm_g, w_pa, w_pb, w_pc, w_out, final_g):
    n_batch, s_len, d = x.shape
    ctx_len = ctx.shape[1]
    r_len = s_len + ctx_len
    depth = w_in.shape[0]
    assert n_batch + 1 <= 8 and w_in.shape[2] == OFF_DL + N_LOGITS + N_BRANCH * d

    xs = jnp.concatenate([x, ctx], axis=1)
    cs = jnp.zeros((8, d), F32).at[:n_batch].set(c).at[n_batch].set(c_ctx)
    mods = _mod_call(cs, w_mod, b_mod)
    cos_t, sin_t = _rope_tables(s_len, ctx_len)
    w1p = jnp.pad(hy_w1, ((0, 0), (0, LANE - HY_EMB), (0, 0)))
    w_head, w_gate = _split_w_in(w_in)

    for layer in range(depth):
        need_ctx = layer < depth - 1
        xs2 = xs.reshape(n_batch * r_len, d)
        h = _modnorm_call(xs2, norm_g[layer], mods[layer], n_batch=n_batch, r_len=r_len, s_len=s_len)
        p2 = _inproj_call(h, w_head, w_gate, layer)
        p3 = p2.reshape(n_batch, r_len, -1)

        ya = _attn_call(p3, cos_t, sin_t, q_norm_g[layer], k_norm_g[layer], s_len=s_len)

        filt = (w1p[layer], hy_b1[layer], hy_freq1[layer], hy_w2[layer], hy_b2[layer], hy_freq2[layer],
                hy_w3[layer])
        yb = _hyena_mixer(p3, 0, s_len, hy_conv_w[layer], hy_conv_b[layer], hy_d[layer], filt)
        if need_ctx:
            yb_c = _hyena_mixer(p3, s_len // ctx_len, ctx_len, hy_conv_w[layer], hy_conv_b[layer],
                                hy_d[layer], filt)
        else:
            yb_c = jnp.zeros((n_batch, ctx_len, HY_W), BF16)
        yb = jnp.concatenate([yb, yb_c], axis=1)

        yc = _deltanet_mixer(p3, dn_conv_w[layer], dn_a_log[layer], dn_dt_bias[layer], dn_norm_g[layer],
                             s_len=s_len)

        rt = n_batch * r_len
        m = _merge_call(ya.reshape(rt, ATT_W), yb.reshape(rt, HY_W), yc.reshape(rt, DN_V_W),
                        w_pa, w_pb, w_pc, layer, p2, r_len=r_len)
        xs = _outproj_call(m, w_out, layer, xs2, mods[layer],
                           n_batch=n_batch, r_len=r_len, s_len=s_len).reshape(n_batch, r_len, d)

    return _finalnorm_call(xs, final_g, s_len=s_len)
```

```python
import functools
import math

import jax
import jax.numpy as jnp
import numpy as np
from jax import lax
from jax.experimental import pallas as pl
from jax.experimental.pallas import tpu as pltpu

F32 = jnp.float32
BF16 = jnp.bfloat16

GRID_W = 64
NORM_EPS = 1e-6
N_BRANCH = 3

ATT_HEADS = 8
ATT_KV_HEADS = 2
HEAD_DIM = 128
ATT_GROUP = ATT_HEADS // ATT_KV_HEADS
ATT_W = ATT_HEADS * HEAD_DIM
ATT_KV_W = ATT_KV_HEADS * HEAD_DIM
AXIS_ROPE_DIM = HEAD_DIM // 2
ROPE_THETA = 10000.0

HY_W = 1024
HY_ORDER = 2
HY_BANDS = 16
HY_EMB = 1 + 2 * HY_BANDS
HY_FILTER_HIDDEN = 64
HY_DECAY_TARGET = 1e-2
HY_DECAY_FAST = 0.3
HY_DECAY_SLOW = 1.5

DN_QK_HEADS = 4
DN_V_HEADS = 8
DN_HEAD_DIM = 128
DN_QK_W = DN_QK_HEADS * DN_HEAD_DIM
DN_V_W = DN_V_HEADS * DN_HEAD_DIM
DN_CHUNK = 64
DN_W = 2 * DN_QK_W + DN_V_W

LANE = 128
MIB = 1024 * 1024

OFF_AQ = 0
OFF_AK = OFF_AQ + ATT_W
OFF_AV = OFF_AK + ATT_KV_W
OFF_AG = OFF_AV + ATT_KV_W
OFF_HV = OFF_AG + ATT_W
OFF_HX1 = OFF_HV + HY_W
OFF_HX2 = OFF_HX1 + HY_W
OFF_HG = OFF_HX2 + HY_W
OFF_DQ = OFF_HG + HY_W
OFF_DV = OFF_DQ + 2 * DN_QK_W
OFF_DZ = OFF_DV + DN_V_W
OFF_DL = OFF_DZ + DN_V_W
N_LOGITS = 4 * DN_V_HEADS
MERGE_ALIGN = 512
OFF_MG = -(-(OFF_DL + N_LOGITS) // MERGE_ALIGN) * MERGE_ALIGN


def _pick(n, cap, mult=LANE):
    best = None
    for t in range(mult, min(n, cap) + 1, mult):
        if n % t == 0:
            best = t
    assert best is not None, (n, cap, mult)
    return best


def _cp(sem, vmem_mib=48):
    return pltpu.CompilerParams(dimension_semantics=sem, vmem_limit_bytes=vmem_mib * MIB)


def _silu(x):
    return x * jax.nn.sigmoid(x)


def _split3(x):
    x1 = x.astype(BF16)
    r = x - x1.astype(F32)
    x2 = r.astype(BF16)
    x3 = (r - x2.astype(F32)).astype(BF16)
    return x1, x2, x3


def _dot(a, b):
    return jnp.dot(a, b, preferred_element_type=F32)


def _dot_hp(a, b):
    a1 = a.astype(BF16)
    a2 = (a - a1.astype(F32)).astype(BF16)
    b1 = b.astype(BF16)
    b2 = (b - b1.astype(F32)).astype(BF16)
    return _dot(a1, b1) + (_dot(a1, b2) + _dot(a2, b1))


def _mod_kernel(cs_ref, w_ref, b_ref, o_ref):
    cs = cs_ref[...]
    o_ref[0] = _dot(_silu(cs).astype(BF16), w_ref[0].astype(BF16)) + b_ref[0]


def _mod_call(cs, w_mod, b_mod):
    depth, d, d3 = w_mod.shape
    tn = _pick(d3, 512)
    return pl.pallas_call(
        _mod_kernel,
        grid=(depth, d3 // tn),
        in_specs=[pl.BlockSpec((8, d), lambda l, j: (0, 0)),
                  pl.BlockSpec((1, d, tn), lambda l, j: (l, 0, j)),
                  pl.BlockSpec((1, 1, tn), lambda l, j: (l, 0, j))],
        out_specs=pl.BlockSpec((1, 8, tn), lambda l, j: (l, 0, j)),
        out_shape=jax.ShapeDtypeStruct((depth, 8, d3), F32),
        compiler_params=_cp(("parallel", "parallel")),
        name="adaln_mod",
    )(cs, w_mod, b_mod.reshape(depth, 1, d3))


def _row_mods(mod_ref, i, r0, rows, tm, tiles_per_batch, s_len, n_batch):
    b = i // tiles_per_batch
    row = (i % tiles_per_batch) * tm + r0 + lax.broadcasted_iota(jnp.int32, (rows, 1), 0)
    return row >= s_len, mod_ref[pl.ds(b, 1), :], mod_ref[n_batch:n_batch + 1, :]


NORM_ROWS = 64


def _modnorm_kernel(x_ref, g_ref, mod_ref, h_ref, *, tm, tiles_per_batch, s_len, n_batch, d):
    i = pl.program_id(0)

    def body(c, carry):
        r0 = pl.multiple_of(c * NORM_ROWS, NORM_ROWS)
        x = x_ref[pl.ds(r0, NORM_ROWS), :]
        y = x * lax.rsqrt(jnp.mean(x * x, axis=-1, keepdims=True) + NORM_EPS) * g_ref[...]
        is_ctx, ml, mc = _row_mods(mod_ref, i, r0, NORM_ROWS, tm, tiles_per_batch, s_len, n_batch)
        shift = jnp.where(is_ctx, mc[:, :d], ml[:, :d])
        scale = jnp.where(is_ctx, mc[:, d:2 * d], ml[:, d:2 * d])
        h_ref[pl.ds(r0, NORM_ROWS), :] = (y * (1.0 + scale) + shift).astype(BF16)
        return carry
    lax.fori_loop(0, tm // NORM_ROWS, body, 0)


def _modnorm_call(xs2, norm_g, mods, *, n_batch, r_len, s_len):
    rt, d = xs2.shape
    tm = _pick(r_len, 576, NORM_ROWS)
    kern = functools.partial(_modnorm_kernel, tm=tm, tiles_per_batch=r_len // tm, s_len=s_len,
                             n_batch=n_batch, d=d)
    return pl.pallas_call(
        kern,
        grid=(rt // tm,),
        in_specs=[pl.BlockSpec((tm, d), lambda i: (i, 0)),
                  pl.BlockSpec((1, d), lambda i: (0, 0)),
                  pl.BlockSpec((8, 3 * d), lambda i: (0, 0))],
        out_specs=pl.BlockSpec((tm, d), lambda i: (i, 0)),
        out_shape=jax.ShapeDtypeStruct((rt, d), BF16),
        compiler_params=_cp(("parallel",)),
        name="mod_norm",
    )(xs2, norm_g.reshape(1, d), mods)


def _inproj_kernel(h_ref, w_ref, o_ref):
    o_ref[...] = _dot(h_ref[...], w_ref[0])


def _split_w_in(w_in):
    cut = OFF_DL + N_LOGITS
    head = jnp.pad(w_in[..., :cut].astype(BF16), ((0, 0), (0, 0), (0, OFF_MG - cut)))
    return head, w_in[..., cut:].astype(BF16)


def _inproj_call(h, w_bf, layer):
    rt, d = h.shape
    nw = w_bf.shape[2]
    tm = _pick(rt, 2176, 16)
    tn = _pick(nw, 1024)
    return pl.pallas_call(
        _inproj_kernel,
        grid=(rt // tm, nw // tn),
        in_specs=[pl.BlockSpec((tm, d), lambda i, j: (i, 0)),
                  pl.BlockSpec((1, d, tn), lambda i, j: (layer, 0, j))],
        out_specs=pl.BlockSpec((tm, tn), lambda i, j: (i, j)),
        out_shape=jax.ShapeDtypeStruct((rt, nw), F32),
        compiler_params=_cp(("parallel", "arbitrary"), 56),
        name="in_proj",
    )(h, w_bf)


def _rope_tables(s_len, ctx_len):
    m = AXIS_ROPE_DIM // 2
    inv_freq = ROPE_THETA ** (-np.arange(0, AXIS_ROPE_DIM, 2, dtype=np.float64) / AXIS_ROPE_DIM)
    t = np.arange(s_len)
    ang_r = (t // GRID_W)[:, None] * inv_freq
    ang_c = (t % GRID_W)[:, None] * inv_freq
    cos = np.concatenate([np.cos(ang_r)] * 2 + [np.cos(ang_c)] * 2, axis=-1)
    sin = np.concatenate([-np.sin(ang_r), np.sin(ang_r), -np.sin(ang_c), np.sin(ang_c)], axis=-1)
    assert cos.shape[1] == 4 * m == HEAD_DIM
    cos = np.concatenate([cos, np.ones((ctx_len, HEAD_DIM))], axis=0)
    sin = np.concatenate([sin, np.zeros((ctx_len, HEAD_DIM))], axis=0)
    return jnp.asarray(cos, F32), jnp.asarray(sin, F32)


def _norm_rope(x, g, cs, sn):
    y = x * lax.rsqrt(jnp.mean(x * x, axis=-1, keepdims=True) + NORM_EPS) * g
    lane = lax.broadcasted_iota(jnp.int32, (1, HEAD_DIM), 1)
    first = (lane % AXIS_ROPE_DIM) < (AXIS_ROPE_DIM // 2)
    q = AXIS_ROPE_DIM // 2
    partner = jnp.where(first, pltpu.roll(y, HEAD_DIM - q, 1), pltpu.roll(y, q, 1))
    return y * cs + partner * sn


def _attn_kernel(q_ref, k_ref, v_ref, gt_ref, cos_ref, sin_ref, qg_ref, kg_ref, o_ref, ks_ref, vs_ref,
                 *, tq, r_len, s_len, kv_chunk):
    i = pl.program_id(2)

    @pl.when(i == 0)
    def _():
        def body(c, carry):
            r0 = pl.multiple_of(c * kv_chunk, kv_chunk)
            kk = _norm_rope(k_ref[0, pl.ds(r0, kv_chunk), :], kg_ref[...],
                            cos_ref[pl.ds(r0, kv_chunk), :], sin_ref[pl.ds(r0, kv_chunk), :])
            ks_ref[pl.ds(r0, kv_chunk), :] = kk.astype(BF16)
            vs_ref[pl.ds(r0, kv_chunk), :] = v_ref[0, pl.ds(r0, kv_chunk), :].astype(BF16)
            return carry
        lax.fori_loop(0, r_len // kv_chunk, body, 0)

    r0 = pl.multiple_of(i * tq, tq)
    cs = cos_ref[pl.ds(r0, tq), :]
    sn = sin_ref[pl.ds(r0, tq), :]
    scale = HEAD_DIM ** -0.5

    def heads(k_lo, k_hi):
        for g in range(ATT_GROUP):
            sl = slice(g * HEAD_DIM, (g + 1) * HEAD_DIM)
            qh = (_norm_rope(q_ref[0, :, sl], qg_ref[...], cs, sn) * scale).astype(BF16)
            s = lax.dot_general(qh, ks_ref[k_lo:k_hi, :], (((1,), (1,)), ((), ())),
                                preferred_element_type=F32)
            e = jnp.exp(s - jnp.max(s, axis=-1, keepdims=True))
            den = jnp.sum(e, axis=-1, keepdims=True)
            o = _dot(e.astype(BF16), vs_ref[k_lo:k_hi, :]) / den
            o_ref[0, :, sl] = (o * _silu(gt_ref[0, :, sl])).astype(o_ref.dtype)

    @pl.when(i < s_len // tq)
    def _():
        heads(0, r_len)

    @pl.when(i >= s_len // tq)
    def _():
        heads(s_len, r_len)


def _attn_call(p3, cos_t, sin_t, q_g, k_g, *, s_len):
    n_batch, r_len, _ = p3.shape
    ctx_len = r_len - s_len
    tq = 256 if (ctx_len % 256 == 0 and s_len % 256 == 0) else 128
    assert ctx_len % tq == 0 and s_len % tq == 0
    gw = ATT_GROUP * HEAD_DIM
    kern = functools.partial(_attn_kernel, tq=tq, r_len=r_len, s_len=s_len, kv_chunk=tq)
    return pl.pallas_call(
        kern,
        grid=(n_batch, ATT_KV_HEADS, r_len // tq),
        in_specs=[pl.BlockSpec((1, tq, gw), lambda b, h, i: (b, i, OFF_AQ // gw + h)),
                  pl.BlockSpec((1, r_len, HEAD_DIM), lambda b, h, i: (b, 0, OFF_AK // HEAD_DIM + h)),
                  pl.BlockSpec((1, r_len, HEAD_DIM), lambda b, h, i: (b, 0, OFF_AV // HEAD_DIM + h)),
                  pl.BlockSpec((1, tq, gw), lambda b, h, i: (b, i, OFF_AG // gw + h)),
                  pl.BlockSpec((r_len, HEAD_DIM), lambda b, h, i: (0, 0)),
                  pl.BlockSpec((r_len, HEAD_DIM), lambda b, h, i: (0, 0)),
                  pl.BlockSpec((1, HEAD_DIM), lambda b, h, i: (0, 0)),
                  pl.BlockSpec((1, HEAD_DIM), lambda b, h, i: (0, 0))],
        out_specs=pl.BlockSpec((1, tq, gw), lambda b, h, i: (b, i, h)),
        out_shape=jax.ShapeDtypeStruct((n_batch, r_len, ATT_W), BF16),
        scratch_shapes=[pltpu.VMEM((r_len, HEAD_DIM), BF16), pltpu.VMEM((r_len, HEAD_DIM), BF16)],
        compiler_params=_cp(("parallel", "parallel", "arbitrary")),
        name="gqa_attention",
    )(p3, p3, p3, p3, cos_t, sin_t, q_g.reshape(1, HEAD_DIM), k_g.reshape(1, HEAD_DIM))


def _conv3(x, w, first, last):
    n = x.shape[0]
    prev = jnp.where(first, 0.0, pltpu.roll(x, 1, 0))
    nxt = jnp.where(last, 0.0, pltpu.roll(x, n - 1, 0))
    return prev * w[0:1, :] + x * w[1:2, :] + nxt * w[2:3, :]


FFT_UNROLL = 16


def _fft_split(n):
    if n <= 512:
        return 1, n
    n2 = 64
    return n // n2, n2


@functools.lru_cache(maxsize=None)
def _fft_mats(n, n1, n2):
    half = n2 // 2
    a = np.arange(n1)[:, None, None]
    k2 = np.arange(n2)[None, :, None]

    def g(bs):
        ph = (k2 * (a + n1 * bs[None, None, :])) % n
        ang = -2.0 * np.pi * ph / n
        return np.cos(ang), np.sin(ang)

    gre, gim = g(np.arange(half))
    gal = np.concatenate([gre, gim], axis=1)
    gar = np.concatenate([-gim, gre], axis=1)
    hre = np.swapaxes(gre, 1, 2) / n
    him = -np.swapaxes(gim, 1, 2) / n
    hal = np.concatenate([hre, him], axis=1)
    har = np.concatenate([-him, hre], axis=1)
    fre_f, fim_f = g(np.arange(n2))
    gf = np.concatenate([fre_f, fim_f], axis=1)
    k1 = np.arange(n1)
    ang1 = -2.0 * np.pi * ((k1[:, None] * k1[None, :]) % n1) / n1
    f1re, f1im = np.cos(ang1), np.sin(ang1)
    fbl = np.concatenate([f1re, f1im], axis=0)
    fbr = np.concatenate([-f1im, f1re], axis=0)
    fbil = np.concatenate([f1re, -f1im], axis=0)
    fbir = np.concatenate([f1im, f1re], axis=0)
    cast = lambda m: jnp.asarray(m, BF16)
    return dict(ga=cast(np.concatenate([gal, gar], axis=2)), ha=cast(np.concatenate([hal, har], axis=2)),
                gf=cast(gf), fb=cast(np.concatenate([fbl, fbr], axis=1)),
                fbi=cast(np.concatenate([fbil, fbir], axis=1)))


@functools.lru_cache(maxsize=None)
def _hy_tables(n_tok):
    pos = np.arange(n_tok, dtype=np.float64)
    t = pos / max(n_tok - 1, 1)
    bands = np.linspace(1e-4, HY_BANDS - 1, HY_BANDS)
    ang = (2.0 * np.pi / n_tok) * pos[:, None] * bands
    z = np.concatenate([t[:, None], np.cos(ang), np.sin(ang)], axis=-1)
    zrev = np.zeros_like(z)
    zrev[1:] = z[:0:-1]
    ztab = np.zeros((2 * n_tok, LANE))
    ztab[:, :HY_EMB] = np.concatenate([z, zrev], axis=0)
    deltas = np.abs(np.linspace(math.log(HY_DECAY_TARGET) / HY_DECAY_SLOW,
                                math.log(HY_DECAY_TARGET) / HY_DECAY_FAST, HY_W))
    return jnp.asarray(ztab, F32), jnp.asarray(np.tile(deltas, HY_ORDER)[None, :], F32)


SLAB_PAD = 8


def _hyfilt_kernel(z_ref, w1_ref, b1_ref, f1_ref, w2_ref, b2_ref, f2_ref, w3_ref, dl_ref, k_ref, s_ref,
                   *, n_tok, tr, slab_rows):
    i = pl.program_id(0)
    z = z_ref[...]
    h = jnp.sin(f1_ref[...] * (_dot_hp(z, w1_ref[...]) + b1_ref[...]))
    h = jnp.sin(f2_ref[...] * (_dot_hp(h, w2_ref[...]) + b2_ref[...]))
    h = _dot(h.astype(BF16), w3_ref[...].astype(BF16)) * jnp.exp(-z[:, 0:1] * dl_ref[...])
    row = i * tr + lax.broadcasted_iota(jnp.int32, (tr, 1), 0)
    h = jnp.where(row == n_tok, 0.0, h)
    if slab_rows is None:
        k_ref[...] = h
    else:
        step = slab_rows + SLAB_PAD
        for s in range(tr // slab_rows):
            k_ref[s * step:s * step + slab_rows, :] = h[s * slab_rows:(s + 1) * slab_rows]
            k_ref[s * step + slab_rows:(s + 1) * step, :] = jnp.zeros((SLAB_PAD, h.shape[1]), F32)

    @pl.when(i == 0)
    def _():
        s_ref[...] = jnp.zeros_like(s_ref)

    s_ref[...] += jnp.sum(jnp.abs(h), axis=0, keepdims=True)


def _hyfilt_call(n_tok, w1p, b1, f1, w2, b2, f2, w3):
    ztab, dl = _hy_tables(n_tok)
    n = 2 * n_tok
    tr = _pick(n_tok, 512, 8)
    ow = HY_ORDER * HY_W
    hid = HY_FILTER_HIDDEN
    n1, _ = _fft_split(n)
    slab_rows = n1 if n1 > 1 else None
    assert slab_rows is None or tr % slab_rows == 0
    out_tr = tr if slab_rows is None else (tr // slab_rows) * (slab_rows + SLAB_PAD)
    kern = functools.partial(_hyfilt_kernel, n_tok=n_tok, tr=tr, slab_rows=slab_rows)
    c2 = lambda i: (0, 0)
    return pl.pallas_call(
        kern,
        grid=(n // tr,),
        in_specs=[pl.BlockSpec((tr, LANE), lambda i: (i, 0)),
                  pl.BlockSpec((LANE, hid), c2), pl.BlockSpec((1, hid), c2), pl.BlockSpec((1, hid), c2),
                  pl.BlockSpec((hid, hid), c2), pl.BlockSpec((1, hid), c2), pl.BlockSpec((1, hid), c2),
                  pl.BlockSpec((hid, ow), lambda i: (0, i // (n_tok // tr))),
                  pl.BlockSpec((1, ow), c2)],
        out_specs=[pl.BlockSpec((out_tr, ow), lambda i: (i, 0)), pl.BlockSpec((1, ow), c2)],
        out_shape=[jax.ShapeDtypeStruct((n // tr * out_tr, ow), F32), jax.ShapeDtypeStruct((1, ow), F32)],
        compiler_params=_cp(("arbitrary",)),
        name="hyena_filter",
    )(ztab, w1p, b1.reshape(1, hid), f1.reshape(1, hid), w2, b2.reshape(1, hid), f2.reshape(1, hid), w3, dl)


FFT_GROUP = 8


def _filtfft_kernel(k_ref, s_ref, gf_ref, fb_ref, o_ref, z_ref, *, n, n1, n2):
    inv = 1.0 / (s_ref[...] + 1e-6)
    if n1 == 1:
        o_ref[...] = _dot(gf_ref[0], k_ref[...].astype(BF16)) * inv
        return
    slab = 2 * n1
    zs, ks = slab + SLAB_PAD, n1 + SLAB_PAD

    def stage_a(a, carry):
        out = _dot(gf_ref[a], k_ref[pl.ds(a, n2, stride=ks), :].astype(BF16))
        z_ref[pl.ds(a, n2, stride=zs), :] = out[:n2]
        z_ref[pl.ds(n1 + a, n2, stride=zs), :] = out[n2:]
        return carry
    lax.fori_loop(0, n1, stage_a, 0, unroll=FFT_UNROLL)

    def stage_b(g, carry):
        for u in range(FFT_GROUP):
            k2 = g * FFT_GROUP + u
            z = z_ref[pl.ds(pl.multiple_of(k2 * zs, 8), slab), :].astype(BF16)
            o_ref[pl.ds(pl.multiple_of(k2 * slab, slab), slab), :] = _dot(fb_ref[...], z) * inv
        return carry
    lax.fori_loop(0, n2 // FFT_GROUP, stage_b, 0)


def _filtfft_call(k_un, asum, n_tok):
    k_rows, ow = k_un.shape
    n = 2 * n_tok
    n1, n2 = _fft_split(n)
    assert k_rows == (n if n1 == 1 else n2 * (n1 + SLAB_PAD))
    m = _fft_mats(n, n1, n2)
    cb = LANE
    kern = functools.partial(_filtfft_kernel, n=n, n1=n1, n2=n2)
    return pl.pallas_call(
        kern,
        grid=(ow // cb,),
        in_specs=[pl.BlockSpec((k_rows, cb), lambda j: (0, j)),
                  pl.BlockSpec((1, cb), lambda j: (0, j)),
                  pl.BlockSpec(m["gf"].shape, lambda j: (0, 0, 0)),
                  pl.BlockSpec(m["fb"].shape, lambda j: (0, 0))],
        out_specs=pl.BlockSpec((2 * n, cb), lambda j: (0, j)),
        out_shape=jax.ShapeDtypeStruct((2 * n, ow), F32),
        scratch_shapes=[pltpu.VMEM((n2 * (2 * n1 + SLAB_PAD) if n1 > 1 else 8, cb), F32)],
        compiler_params=_cp(("parallel",)),
        name="hyena_filter_fft",
    )(k_un, asum, m["gf"], m["fb"])


def _hyena_kernel(*refs, n_tok, n1, n2, conv_a, has_gate):
    it = iter(refs)
    a_ref, m_ref = next(it), next(it)
    g_ref = next(it) if has_gate else None
    if conv_a:
        cwa_ref, cba_ref = next(it), next(it)
    cwm_ref, cbm_ref, d_ref, ks_ref = next(it), next(it), next(it), next(it)
    ga_ref, ha_ref = next(it), next(it)
    if n1 > 1:
        fb_ref, fbi_ref = next(it), next(it)
    o_ref, ac_ref, z_ref = next(it), next(it), next(it)

    n = 2 * n_tok
    half = n2 // 2
    row = lax.broadcasted_iota(jnp.int32, (n_tok, 1), 0)
    first, last = row == 0, row == n_tok - 1
    sr = n1 if n1 > 1 else n_tok
    astep = sr + SLAB_PAD if n1 > 1 else sr
    for b in range(2):
        a = a_ref[b]
        if conv_a:
            a = _conv3(a, cwa_ref[...], first, last) + cba_ref[...]
        for s in range(n_tok // sr):
            ac_ref[b, s * astep:s * astep + sr, :] = a[s * sr:(s + 1) * sr]

    def spectrum_mul(x, k, h):
        xre, xim, kre, kim = x[:h], x[h:], k[:h], k[h:]
        return jnp.concatenate([xre * kre - xim * kim, xre * kim + xim * kre], axis=0).astype(BF16)

    if n1 == 1:
        x = _dot(ga_ref[0], jnp.concatenate([ac_ref[0], ac_ref[1]], axis=0).astype(BF16))
        y = _dot(ha_ref[0], spectrum_mul(x, ks_ref[...], n))
        for b in range(2):
            ac_ref[b] = y[b * n_tok:(b + 1) * n_tok] + ac_ref[b] * d_ref[0]
    else:
        slab = 2 * n1
        zs = slab + SLAB_PAD

        def stage_a(a, carry):
            rows = jnp.concatenate([ac_ref[0, pl.ds(a, half, stride=astep), :],
                                    ac_ref[1, pl.ds(a, half, stride=astep), :]], axis=0).astype(BF16)
            out = _dot(ga_ref[a], rows)
            z_ref[pl.ds(a, n2, stride=zs), :] = out[:n2]
            z_ref[pl.ds(n1 + a, n2, stride=zs), :] = out[n2:]
            return carry
        lax.fori_loop(0, n1, stage_a, 0, unroll=FFT_UNROLL)

        def stage_b(g, carry):
            k2s = [g * FFT_GROUP + u for u in range(FFT_GROUP)]
            zrows = [pl.ds(pl.multiple_of(k2 * zs, 8), slab) for k2 in k2s]
            xs = [_dot(fb_ref[...], z_ref[zr, :].astype(BF16)) for zr in zrows]
            ys = [spectrum_mul(x, ks_ref[pl.ds(pl.multiple_of(k2 * slab, slab), slab), :], n1)
                  for x, k2 in zip(xs, k2s)]
            for y, zr in zip(ys, zrows):
                z_ref[zr, :] = _dot(fbi_ref[...], y)
            return carry
        lax.fori_loop(0, n2 // FFT_GROUP, stage_b, 0)

        def stage_a_inv(a, carry):
            rows = jnp.concatenate([z_ref[pl.ds(a, n2, stride=zs), :],
                                    z_ref[pl.ds(n1 + a, n2, stride=zs), :]], axis=0).astype(BF16)
            y = _dot(ha_ref[a], rows)
            for b in range(2):
                cur = ac_ref[b, pl.ds(a, half, stride=astep), :]
                ac_ref[b, pl.ds(a, half, stride=astep), :] = y[b * half:(b + 1) * half] + cur * d_ref[0]
            return carry
        lax.fori_loop(0, n1, stage_a_inv, 0, unroll=FFT_UNROLL)

    for b in range(2):
        mc = _conv3(m_ref[b], cwm_ref[...], first, last) + cbm_ref[...]
        for s in range(n_tok // sr):
            rows = slice(s * sr, (s + 1) * sr)
            out = mc[rows] * ac_ref[b, s * astep:s * astep + sr, :]
            if has_gate:
                out = out * _silu(g_ref[b, rows, :])
            o_ref[b, rows, :] = out.astype(o_ref.dtype)


def _hyena_call(a_arr, a_blk, m_arr, m_blk, gate, conv_a, conv_m, d_row, spec, spec_blk, *, n_tok, out_dtype):
    n_batch = a_arr.shape[0]
    assert n_batch % 2 == 0
    n = 2 * n_tok
    n1, n2 = _fft_split(n)
    m = _fft_mats(n, n1, n2)
    cb = LANE
    one = pl.Buffered(1)
    tile_mode, spec_mode = (one, pl.Buffered(2)) if gate is not None else (pl.Buffered(2), one)

    def tile(blk):
        return pl.BlockSpec((2, n_tok, cb), lambda p, j, blk=blk: (p, blk[0], blk[1] + j), pipeline_mode=tile_mode)

    def convspec(c0):
        return [pl.BlockSpec((3, cb), lambda p, j, c0=c0: (0, c0 + j)),
                pl.BlockSpec((1, cb), lambda p, j, c0=c0: (0, c0 + j))]

    ins, specs = [a_arr, m_arr], [tile(a_blk), tile(m_blk)]
    if gate is not None:
        ins.append(gate[0])
        specs.append(tile(gate[1]))
    if conv_a is not None:
        ins += [conv_a[0], conv_a[1]]
        specs += convspec(conv_a[2])
    ins += [conv_m[0], conv_m[1], d_row, spec]
    specs += convspec(conv_m[2])
    specs += [pl.BlockSpec((1, 1, cb), lambda p, j: (0, 0, j)),
              pl.BlockSpec((2 * n, cb), lambda p, j, s0=spec_blk: (0, s0 + j), pipeline_mode=spec_mode)]
    names = ["ga", "ha"] + (["fb", "fbi"] if n1 > 1 else [])
    for nm in names:
        ins.append(m[nm])
        specs.append(pl.BlockSpec(m[nm].shape, lambda p, j, nd=m[nm].ndim: (0,) * nd, pipeline_mode=one))
    kern = functools.partial(_hyena_kernel, n_tok=n_tok, n1=n1, n2=n2, conv_a=conv_a is not None,
                             has_gate=gate is not None)
    return pl.pallas_call(
        kern,
        grid=(n_batch // 2, HY_W // cb),
        in_specs=specs,
        out_specs=pl.BlockSpec((2, n_tok, cb), lambda p, j: (p, 0, j)),
        out_shape=jax.ShapeDtypeStruct((n_batch, n_tok, HY_W), out_dtype),
        scratch_shapes=[pltpu.VMEM((2, n_tok // n1 * (n1 + SLAB_PAD) if n1 > 1 else n_tok, cb), F32),
                        pltpu.VMEM((n2 * (2 * n1 + SLAB_PAD) if n1 > 1 else 8, cb), F32)],
        compiler_params=_cp(("parallel", "arbitrary"), 56),
        name="hyena_conv",
    )(*ins)


def _hyena_mixer(p3, row_blk, n_tok, conv_w, conv_b, hy_d, filt):
    k_un, asum = _hyfilt_call(n_tok, *filt)
    spec = _filtfft_call(k_un, asum, n_tok)
    cbias = conv_b.reshape(1, -1)
    cblk = HY_W // LANE
    d3 = hy_d.reshape(HY_ORDER, 1, HY_W)
    z = _hyena_call(p3, (row_blk, OFF_HV // LANE), p3, (row_blk, OFF_HX1 // LANE), None,
                    (conv_w, cbias, 0), (conv_w, cbias, cblk), d3[0:1], spec, 0,
                    n_tok=n_tok, out_dtype=F32)
    return _hyena_call(z, (0, 0), p3, (row_blk, OFF_HX2 // LANE), (p3, (row_blk, OFF_HG // LANE)),
                       None, (conv_w, cbias, 2 * cblk), d3[1:2], spec, cblk,
                       n_tok=n_tok, out_dtype=BF16)


def _softplus(x):
    return jnp.maximum(x, 0.0) + jnp.log1p(jnp.exp(-jnp.abs(x)))


FLIP_ROWS = 256


def _exchange_matrix():
    return jnp.asarray(np.eye(FLIP_ROWS)[::-1], BF16)


def _flip_rows(jm, x):
    x1, x2, x3 = _split3(x)
    return _dot(jm, x1) + (_dot(jm, x2) + _dot(jm, x3))


def _dnprep_kernel(x_ref, w_ref, jm_ref, o_ref, *, r_len, s_len):
    j = pl.program_id(1)
    row = lax.broadcasted_iota(jnp.int32, (r_len, 1), 0)
    first = (row == 0) | (row == s_len)
    last = (row == s_len - 1) | (row == r_len - 1)
    u = _silu(_conv3(x_ref[0], w_ref[...], first, last))
    nrm = u * lax.rsqrt(jnp.sum(u * u, axis=-1, keepdims=True) + 1e-6)
    o_ref[0, 0] = jnp.where(j < 2 * DN_QK_HEADS, nrm, u)
    for seg0, seg_len in ((0, s_len), (s_len, r_len - s_len)):
        nt = seg_len // FLIP_ROWS
        for t in range(nt):
            src = seg0 + t * FLIP_ROWS
            dst = seg0 + (nt - 1 - t) * FLIP_ROWS
            o_ref[1, 0, dst:dst + FLIP_ROWS, :] = _flip_rows(jm_ref[...], o_ref[0, 0, src:src + FLIP_ROWS, :])


def _dnprep_call(p3, conv_w, *, s_len):
    n_batch, r_len, _ = p3.shape
    assert s_len % FLIP_ROWS == 0 and (r_len - s_len) % FLIP_ROWS == 0
    kern = functools.partial(_dnprep_kernel, r_len=r_len, s_len=s_len)
    return pl.pallas_call(
        kern,
        grid=(n_batch, DN_W // LANE),
        in_specs=[pl.BlockSpec((1, r_len, LANE), lambda b, j: (b, 0, OFF_DQ // LANE + j)),
                  pl.BlockSpec((3, LANE), lambda b, j: (0, j)),
                  pl.BlockSpec((FLIP_ROWS, FLIP_ROWS), lambda b, j: (0, 0))],
        out_specs=pl.BlockSpec((2, 1, r_len, LANE), lambda b, j: (0, b, 0, j)),
        out_shape=jax.ShapeDtypeStruct((2, n_batch, r_len, DN_W), F32),
        compiler_params=_cp(("parallel", "parallel")),
        name="deltanet_prep",
    )(p3, conv_w, _exchange_matrix())


def _tri_mats(tr):
    idx = np.arange(tr)
    same = (idx[:, None] // DN_CHUNK) == (idx[None, :] // DN_CHUNK)
    low = same & (idx[:, None] >= idx[None, :])
    return jnp.asarray(low, BF16), jnp.asarray(low.T, BF16), jnp.asarray(same, BF16)


def _dnintra_kernel(u_ref, la_ref, lb_ref, lat_ref, alr_ref, dtr_ref, alc_ref, dtc_ref, low_ref, upp_ref,
                    one_ref, a_ref, qk_ref, be_ref, eg_ref, ek_ref, gt_ref, *, tr):
    g_col = -jnp.exp(alr_ref[0]) * _softplus(la_ref[0, 0] + dtr_ref[0])
    beta = jax.nn.sigmoid(lb_ref[0, 0])
    g1, g2, g3 = _split3(g_col)
    gc_col = _dot(low_ref[...], g1) + (_dot(low_ref[...], g2) + _dot(low_ref[...], g3))
    gt_col = _dot(one_ref[...], g1) + (_dot(one_ref[...], g2) + _dot(one_ref[...], g3))
    be_ref[0, 0] = beta
    eg_ref[0, 0] = jnp.exp(gc_col)
    ek_ref[0, 0] = jnp.exp(gt_col - gc_col)
    gt_ref[0, 0] = jnp.exp(gt_col)
    g_row = -jnp.exp(alc_ref[0]) * _softplus(lat_ref[0, 0] + dtc_ref[0])
    r1, r2, r3 = _split3(g_row)
    gc_row = _dot(r1, upp_ref[...]) + (_dot(r2, upp_ref[...]) + _dot(r3, upp_ref[...]))

    ii = lax.broadcasted_iota(jnp.int32, (DN_CHUNK, DN_CHUNK), 0)
    jj = lax.broadcasted_iota(jnp.int32, (DN_CHUNK, DN_CHUNK), 1)
    scale = DN_HEAD_DIM ** -0.5
    nt = (((1,), (1,)), ((), ()))
    for c in range(tr // DN_CHUNK):
        rows = slice(c * DN_CHUNK, (c + 1) * DN_CHUNK)
        for hq in range(DN_QK_HEADS):
            q = u_ref[0, 0, rows, hq * DN_HEAD_DIM:(hq + 1) * DN_HEAD_DIM].astype(BF16)
            k = u_ref[0, 0, rows, DN_QK_W + hq * DN_HEAD_DIM:DN_QK_W + (hq + 1) * DN_HEAD_DIM].astype(BF16)
            kk = lax.dot_general(k, k, nt, preferred_element_type=F32)
            qk = lax.dot_general(q, k, nt, preferred_element_type=F32) * scale
            for h in range(hq * (DN_V_HEADS // DN_QK_HEADS), (hq + 1) * (DN_V_HEADS // DN_QK_HEADS)):
                diff = gc_col[rows, h:h + 1] - gc_row[h:h + 1, rows]
                dec = jnp.where(ii >= jj, jnp.exp(jnp.minimum(diff, 0.0)), 0.0)
                a_ref[0, 0, h, c] = jnp.where(ii > jj, kk * beta[rows, h:h + 1] * dec, 0.0)
                qk_ref[0, 0, h, c] = (qk * dec).astype(qk_ref.dtype)


def _dnintra_call(u2, la, lb, lat, alog, dtb):
    n_dir, n_batch, r_len, _ = u2.shape
    tr = 4 * DN_CHUNK
    nc = r_len // DN_CHUNK
    low, upp, one = _tri_mats(tr)
    kern = functools.partial(_dnintra_kernel, tr=tr)
    pad = lambda v: jnp.pad(v, ((0, 0), (0, LANE - DN_V_HEADS))).reshape(n_dir, 1, LANE)
    col = lambda v: v.reshape(n_dir, DN_V_HEADS, 1)
    cm = lambda d, b, t: (0, 0)
    gspec = pl.BlockSpec((1, 1, tr, LANE), lambda d, b, t: (d, b, t, 0))
    mspec = pl.BlockSpec((1, 1, DN_V_HEADS, tr // DN_CHUNK, DN_CHUNK, DN_CHUNK), lambda d, b, t: (d, b, 0, t, 0, 0))
    gshape = jax.ShapeDtypeStruct((n_dir, n_batch, r_len, LANE), F32)
    mshape = jax.ShapeDtypeStruct((n_dir, n_batch, DN_V_HEADS, nc, DN_CHUNK, DN_CHUNK), F32)
    return pl.pallas_call(
        kern,
        grid=(n_dir, n_batch, r_len // tr),
        in_specs=[pl.BlockSpec((1, 1, tr, 2 * DN_QK_W), lambda d, b, t: (d, b, t, 0)),
                  gspec, gspec,
                  pl.BlockSpec((1, 1, DN_V_HEADS, tr), lambda d, b, t: (d, b, 0, t)),
                  pl.BlockSpec((1, 1, LANE), lambda d, b, t: (d, 0, 0)),
                  pl.BlockSpec((1, 1, LANE), lambda d, b, t: (d, 0, 0)),
                  pl.BlockSpec((1, DN_V_HEADS, 1), lambda d, b, t: (d, 0, 0)),
                  pl.BlockSpec((1, DN_V_HEADS, 1), lambda d, b, t: (d, 0, 0)),
                  pl.BlockSpec((tr, tr), cm), pl.BlockSpec((tr, tr), cm), pl.BlockSpec((tr, tr), cm)],
        out_specs=[mspec, mspec, gspec, gspec, gspec, gspec],
        out_shape=[mshape, jax.ShapeDtypeStruct(mshape.shape, BF16), gshape, gshape, gshape, gshape],
        compiler_params=_cp(("parallel", "parallel", "parallel")),
        name="deltanet_intra",
    )(u2, la, lb, lat, pad(alog), pad(dtb), col(alog), col(dtb), low, upp, one)


def _dnsolve_kernel(a_ref, o_ref, at_ref, tt_ref):
    c = DN_CHUNK
    for blk in range(c * c // LANE):
        at_ref[blk * LANE:(blk + 1) * LANE, :] = a_ref[:, blk * LANE:(blk + 1) * LANE].T
    tt_ref[...] = jnp.zeros_like(tt_ref)
    for i in range(c):
        nr = 8 * (i // 8 + 1)
        rr = lax.broadcasted_iota(jnp.int32, (nr, LANE), 0)
        acc = jnp.where(rr == i, 1.0, 0.0)

        def body(j, acc, i=i, nr=nr):
            a = at_ref[pl.ds(i * c + j, 1), :]
            return acc - a * tt_ref[pl.ds(pl.multiple_of(j * c, c), nr), :]
        if i > 0:
            acc = lax.fori_loop(0, i, body, acc, unroll=min(i, 8))
        tt_ref[i * c:i * c + nr, :] = acc
    for blk in range(c * c // LANE):
        o_ref[:, blk * LANE:(blk + 1) * LANE] = tt_ref[blk * LANE:(blk + 1) * LANE, :].T.astype(o_ref.dtype)


def _dnsolve_call(a2):
    ni, cc = a2.shape
    assert ni % LANE == 0
    return pl.pallas_call(
        _dnsolve_kernel,
        grid=(ni // LANE,),
        in_specs=[pl.BlockSpec((LANE, cc), lambda i: (i, 0))],
        out_specs=pl.BlockSpec((LANE, cc), lambda i: (i, 0)),
        out_shape=jax.ShapeDtypeStruct((ni, cc), BF16),
        scratch_shapes=[pltpu.VMEM((cc, LANE), F32), pltpu.VMEM((cc, LANE), F32)],
        compiler_params=_cp(("parallel",)),
        name="deltanet_solve",
    )(a2)


def _dnscan_kernel(u_ref, t_ref, qk_ref, be_ref, eg_ref, ek_ref, gt_ref, o_ref, *s_refs, n_dir, n_batch):
    @pl.when(pl.program_id(0) == 0)
    def _():
        for s_ref in s_refs:
            s_ref[...] = jnp.zeros_like(s_ref)

    scale = DN_HEAD_DIM ** -0.5
    rep = DN_V_HEADS // DN_QK_HEADS
    streams = [(d, b, h) for d in range(n_dir) for b in range(n_batch) for h in range(DN_V_HEADS)]

    def qkv(d, b, h):
        hq = h // rep
        q = u_ref[d, b, :, hq * DN_HEAD_DIM:(hq + 1) * DN_HEAD_DIM]
        k = u_ref[d, b, :, DN_QK_W + hq * DN_HEAD_DIM:DN_QK_W + (hq + 1) * DN_HEAD_DIM]
        v = u_ref[d, b, :, 2 * DN_QK_W + h * DN_HEAD_DIM:2 * DN_QK_W + (h + 1) * DN_HEAD_DIM]
        return q, k, v

    uws, egs = [], []
    for d, b, h in streams:
        _, k, v = qkv(d, b, h)
        be = be_ref[d, b, :, h:h + 1]
        egs.append(jnp.broadcast_to(eg_ref[d, b, :, h:h + 1], (DN_CHUNK, DN_HEAD_DIM)))
        rhs = jnp.concatenate([v * be, (k * be) * egs[-1]], axis=1).astype(BF16)
        uws.append(_dot(t_ref[d, b, h, 0], rhs))
    wqs = []
    for i, (d, b, h) in enumerate(streams):
        q, _, _ = qkv(d, b, h)
        lhs = jnp.concatenate([uws[i][:, DN_HEAD_DIM:], q * (scale * egs[i])], axis=0)
        wqs.append(_dot(lhs.astype(BF16), s_refs[i][...].astype(BF16)))
    for i, (d, b, h) in enumerate(streams):
        _, k, _ = qkv(d, b, h)
        vnb = (uws[i][:, :DN_HEAD_DIM] - wqs[i][:DN_CHUNK]).astype(BF16)
        o_ref[d, b, :, h * DN_HEAD_DIM:(h + 1) * DN_HEAD_DIM] = (
            wqs[i][DN_CHUNK:] + _dot(qk_ref[d, b, h, 0], vnb))
        kd = (k * ek_ref[d, b, :, h:h + 1]).astype(BF16)
        s_refs[i][...] = s_refs[i][...] * gt_ref[d, b, 0:1, h:h + 1] + lax.dot_general(
            kd, vnb, (((0,), (0,)), ((), ())), preferred_element_type=F32)


def _dnscan_call(u2, t6, qk6, be, eg, ek, gt, *, s_len):
    n_dir, n_batch, r_len, _ = u2.shape
    nc = r_len // DN_CHUNK
    ncl = s_len // DN_CHUNK

    def cidx(t):
        return jnp.where(t < nc - ncl, ncl + t, t - (nc - ncl))

    kern = functools.partial(_dnscan_kernel, n_dir=n_dir, n_batch=n_batch)
    gspec = pl.BlockSpec((n_dir, n_batch, DN_CHUNK, LANE), lambda t: (0, 0, cidx(t), 0))
    mspec = pl.BlockSpec((n_dir, n_batch, DN_V_HEADS, 1, DN_CHUNK, DN_CHUNK), lambda t: (0, 0, 0, cidx(t), 0, 0))
    return pl.pallas_call(
        kern,
        grid=(nc,),
        in_specs=[pl.BlockSpec((n_dir, n_batch, DN_CHUNK, DN_W), lambda t: (0, 0, cidx(t), 0)),
                  mspec, mspec, gspec, gspec, gspec, gspec],
        out_specs=pl.BlockSpec((n_dir, n_batch, DN_CHUNK, DN_V_W), lambda t: (0, 0, cidx(t), 0)),
        out_shape=jax.ShapeDtypeStruct((n_dir, n_batch, r_len, DN_V_W), F32),
        scratch_shapes=[pltpu.VMEM((DN_HEAD_DIM, DN_HEAD_DIM), F32)] * (n_dir * n_batch * DN_V_HEADS),
        compiler_params=_cp(("arbitrary",)),
        name="deltanet_scan",
    )(u2, t6, qk6, be, eg, ek, gt)


DN_OUT_COLS = 512


def _dnout_kernel(of_ref, ob_ref, z_ref, g_ref, jm_ref, o_ref):
    for h in range(DN_OUT_COLS // DN_HEAD_DIM):
        sl = slice(h * DN_HEAD_DIM, (h + 1) * DN_HEAD_DIM)
        o = of_ref[0, 0, :, sl] + _flip_rows(jm_ref[...], ob_ref[0, 0, :, sl])
        y = o * lax.rsqrt(jnp.mean(o * o, axis=-1, keepdims=True) + NORM_EPS) * g_ref[...]
        o_ref[0, :, sl] = (y * _silu(z_ref[0, :, sl])).astype(o_ref.dtype)


def _dnout_call(o2, p3, norm_g, *, s_len):
    _, n_batch, r_len, _ = o2.shape
    tr = FLIP_ROWS
    cw = DN_OUT_COLS
    ns, nc = s_len // tr, (r_len - s_len) // tr
    assert OFF_DZ % cw == 0

    def mirror(i):
        return jnp.where(i < ns, ns - 1 - i, 2 * ns + nc - 1 - i)

    return pl.pallas_call(
        _dnout_kernel,
        grid=(n_batch, r_len // tr, DN_V_W // cw),
        in_specs=[pl.BlockSpec((1, 1, tr, cw), lambda b, i, j: (0, b, i, j)),
                  pl.BlockSpec((1, 1, tr, cw), lambda b, i, j: (1, b, mirror(i), j)),
                  pl.BlockSpec((1, tr, cw), lambda b, i, j: (b, i, OFF_DZ // cw + j)),
                  pl.BlockSpec((1, DN_HEAD_DIM), lambda b, i, j: (0, 0)),
                  pl.BlockSpec((FLIP_ROWS, FLIP_ROWS), lambda b, i, j: (0, 0))],
        out_specs=pl.BlockSpec((1, tr, cw), lambda b, i, j: (b, i, j)),
        out_shape=jax.ShapeDtypeStruct((n_batch, r_len, DN_V_W), BF16),
        compiler_params=_cp(("parallel", "parallel", "parallel")),
        name="deltanet_out",
    )(o2, o2, p3, norm_g.reshape(1, DN_HEAD_DIM), _exchange_matrix())


def _seq_flip(a, s_len, axis):
    lat, ctx = jnp.split(a, [s_len], axis=axis)
    return jnp.concatenate([jnp.flip(lat, axis), jnp.flip(ctx, axis)], axis=axis)


def _deltanet_mixer(p3, conv_w, a_log, dt_bias, norm_g, *, s_len):
    n_batch, r_len, _ = p3.shape
    u2 = _dnprep_call(p3, conv_w, s_len=s_len)
    lg =p3[:, :, OFF_DL:OFF_DL + N_LOGITS].reshape(n_batch, r_len, 2, 2, DN_V_HEADS)

    def dirs(x):
        return jnp.stack([x[:, :, 0], _seq_flip(x[:, :, 1], s_len, 1)])

    la, lb = dirs(lg[:, :, 0]), dirs(lg[:, :, 1])
    padl = lambda x: jnp.pad(x, ((0, 0), (0, 0), (0, 0), (0, LANE - DN_V_HEADS)))
    a6, qk6, be, eg, ek, gt = _dnintra_call(u2, padl(la), padl(lb), jnp.swapaxes(la, 2, 3), a_log, dt_bias)
    t6 = _dnsolve_call(a6.reshape(-1, DN_CHUNK * DN_CHUNK)).reshape(a6.shape)
    o2 = _dnscan_call(u2, t6, qk6, be, eg, ek, gt, s_len=s_len)
    return _dnout_call(o2, p3, norm_g, s_len=s_len)


def _merge_kernel(ya_ref, yb_ref, yc_ref, wa_ref, wb_ref, wc_ref, ga_ref, gb_ref, gc_ref, o_ref,
                  wa_s, wb_s, wc_s):
    @pl.when(pl.program_id(1) == 0)
    def _():
        wa_s[...] = wa_ref[0].astype(BF16)
        wb_s[...] = wb_ref[0].astype(BF16)
        wc_s[...] = wc_ref[0].astype(BF16)

    m = (jax.nn.sigmoid(ga_ref[...]) * _dot(ya_ref[...], wa_s[...])
         + jax.nn.sigmoid(gb_ref[...]) * _dot(yb_ref[...], wb_s[...])
         + jax.nn.sigmoid(gc_ref[...]) * _dot(yc_ref[...], wc_s[...]))
    o_ref[...] = m.astype(o_ref.dtype)


def _merge_call(ya, yb, yc, w_pa, w_pb, w_pc, layer, p2, *, r_len):
    rt = ya.shape[0]
    d = w_pa.shape[2]
    tm = _pick(r_len, 1088, 16)
    tn = _pick(d, MERGE_ALIGN)
    assert OFF_MG % tn == 0
    yspec = lambda w: pl.BlockSpec((tm, w), lambda j, i: (i, 0))
    wspec = lambda w: pl.BlockSpec((1, w, tn), lambda j, i: (layer, 0, j))
    gspec = lambda br: pl.BlockSpec((tm, tn), lambda j, i, br=br: (i, br * d // tn + j))
    return pl.pallas_call(
        _merge_kernel,
        grid=(d // tn, rt // tm),
        in_specs=[yspec(ATT_W), yspec(HY_W), yspec(DN_V_W), wspec(ATT_W), wspec(HY_W), wspec(DN_V_W),
                  gspec(0), gspec(1), gspec(2)],
        out_specs=pl.BlockSpec((tm, tn), lambda j, i: (i, j)),
        out_shape=jax.ShapeDtypeStruct((rt, d), BF16),
        scratch_shapes=[pltpu.VMEM((ATT_W, tn), BF16), pltpu.VMEM((HY_W, tn), BF16),
                        pltpu.VMEM((DN_V_W, tn), BF16)],
        compiler_params=_cp(("parallel", "arbitrary"), 56),
        name="branch_merge",
    )(ya, yb, yc, w_pa, w_pb, w_pc, p2, p2, p2)


def _outproj_kernel(m_ref, w_ref, x_ref, gate_ref, o_ref, w_s, *, tm, tiles_per_batch, s_len, n_batch):
    @pl.when(pl.program_id(1) == 0)
    def _():
        w_s[...] = w_ref[0].astype(BF16)

    is_ctx, gl, gc = _row_mods(gate_ref, pl.program_id(1), 0, tm, tm, tiles_per_batch, s_len, n_batch)
    o_ref[...] = x_ref[...] + jnp.where(is_ctx, gc, gl) * _dot(m_ref[...], w_s[...])


def _outproj_call(m, w_out, layer, xs2, mods, *, n_batch, r_len, s_len):
    rt, d = xs2.shape
    tm = _pick(r_len, 1088, 16)
    tn = _pick(d, 512)
    kern = functools.partial(_outproj_kernel, tm=tm, tiles_per_batch=r_len // tm, s_len=s_len, n_batch=n_batch)
    return pl.pallas_call(
        kern,
        grid=(d // tn, rt // tm),
        in_specs=[pl.BlockSpec((tm, d), lambda j, i: (i, 0)),
                  pl.BlockSpec((1, d, tn), lambda j, i: (layer, 0, j)),
                  pl.BlockSpec((tm, tn), lambda j, i: (i, j)),
                  pl.BlockSpec((8, tn), lambda j, i: (0, 2 * d // tn + j))],
        out_specs=pl.BlockSpec((tm, tn), lambda j, i: (i, j)),
        out_shape=jax.ShapeDtypeStruct((rt, d), F32),
        scratch_shapes=[pltpu.VMEM((d, tn), BF16)],
        compiler_params=_cp(("parallel", "arbitrary")),
        name="out_proj_residual",
    )(m, w_out, xs2, mods)


def _finalnorm_kernel(x_ref, g_ref, o_ref):
    x = x_ref[0]
    o_ref[0] = x * lax.rsqrt(jnp.mean(x * x, axis=-1, keepdims=True) + NORM_EPS) * g_ref[...]


def _finalnorm_call(xs, final_g, *, s_len):
    n_batch, _, d = xs.shape
    tr = _pick(s_len, 512, 8)
    return pl.pallas_call(
        _finalnorm_kernel,
        grid=(n_batch, s_len // tr),
        in_specs=[pl.BlockSpec((1, tr, d), lambda b, i: (b, i, 0)), pl.BlockSpec((1, d), lambda b, i: (0, 0))],
        out_specs=pl.BlockSpec((1, tr, d), lambda b, i: (b, i, 0)),
        out_shape=jax.ShapeDtypeStruct((n_batch, s_len, d), F32),
        compiler_params=_cp(("parallel", "parallel")),
        name="final_norm",
    )(xs, final_g.reshape(1, d))


def kernel(x, c, ctx, c_ctx, norm_g, w_mod, b_mod, w_in, q_norm_g, k_norm_g, hy_conv_w, hy_conv_b, hy_w1, hy_b1, hy_freq1, hy_w2, hy_b2, hy_freq2, hy_w3, hy_d, dn_conv_w, dn_a_log, dn_dt_bias, dn_norm_g, w_pa, w_pb, w_pc, w_out, final_g):
    n_batch, s_len, d = x.shape
    ctx_len = ctx.shape[1]
    r_len = s_len + ctx_len
    depth = w_in.shape[0]
    assert n_batch + 1 <= 8 and w_in.shape[2] == OFF_DL + N_LOGITS + N_BRANCH * d

    xs = jnp.concatenate([x, ctx], axis=1)
    cs = jnp.zeros((8, d), F32).at[:n_batch].set(c).at[n_batch].set(c_ctx)
    mods = _mod_call(cs, w_mod, b_mod)
    cos_t, sin_t = _rope_tables(s_len, ctx_len)
    w1p = jnp.pad(hy_w1, ((0, 0), (0, LANE - HY_EMB), (0, 0)))
    w_head, w_gate = _split_w_in(w_in)

    for layer in range(depth):
        need_ctx = layer < depth - 1
        xs2 = xs.reshape(n_batch * r_len, d)
        h = _modnorm_call(xs2, norm_g[layer], mods[layer], n_batch=n_batch, r_len=r_len, s_len=s_len)
        p3 = _inproj_call(h, w_head, layer).reshape(n_batch, r_len, -1)
        p_gate = _inproj_call(h, w_gate, layer)

        ya = _attn_call(p3, cos_t, sin_t, q_norm_g[layer], k_norm_g[layer], s_len=s_len)

        filt = (w1p[layer], hy_b1[layer], hy_freq1[layer], hy_w2[layer], hy_b2[layer], hy_freq2[layer],
                hy_w3[layer])
        yb = _hyena_mixer(p3, 0, s_len, hy_conv_w[layer], hy_conv_b[layer], hy_d[layer], filt)
        if need_ctx:
            yb_c = _hyena_mixer(p3, s_len // ctx_len, ctx_len, hy_conv_w[layer], hy_conv_b[layer],
                                hy_d[layer], filt)
        else:
            yb_c = jnp.zeros((n_batch, ctx_len, HY_W), BF16)
        yb = jnp.concatenate([yb, yb_c], axis=1)

        yc = _deltanet_mixer(p3, dn_conv_w[layer], dn_a_log[layer], dn_dt_bias[layer], dn_norm_g[layer],
                             s_len=s_len)

        rt = n_batch * r_len
        m = _merge_call(ya.reshape(rt, ATT_W), yb.reshape(rt, HY_W), yc.reshape(rt, DN_V_W),
                        w_pa, w_pb, w_pc, layer, p_gate, r_len=r_len)
        xs = _outproj_call(m, w_out, layer, xs2, mods[layer],
                           n_batch=n_batch, r_len=r_len, s_len=s_len).reshape(n_batch, r_len, d)

    return _finalnorm_call(xs, final_g, s_len=s_len)
```

```python
import functools
import math

import jax
import jax.numpy as jnp
import numpy as np
from jax import lax
from jax.experimental import pallas as pl
from jax.experimental.pallas import tpu as pltpu

F32 = jnp.float32
BF16 = jnp.bfloat16

GRID_W = 64
NORM_EPS = 1e-6
N_BRANCH = 3

ATT_HEADS = 8
ATT_KV_HEADS = 2
HEAD_DIM = 128
ATT_GROUP = ATT_HEADS // ATT_KV_HEADS
ATT_W = ATT_HEADS * HEAD_DIM
ATT_KV_W = ATT_KV_HEADS * HEAD_DIM
AXIS_ROPE_DIM = HEAD_DIM // 2
ROPE_THETA = 10000.0

HY_W = 1024
HY_ORDER = 2
HY_BANDS = 16
HY_EMB = 1 + 2 * HY_BANDS
HY_FILTER_HIDDEN = 64
HY_DECAY_TARGET = 1e-2
HY_DECAY_FAST = 0.3
HY_DECAY_SLOW = 1.5

DN_QK_HEADS = 4
DN_V_HEADS = 8
DN_HEAD_DIM = 128
DN_QK_W = DN_QK_HEADS * DN_HEAD_DIM
DN_V_W = DN_V_HEADS * DN_HEAD_DIM
DN_CHUNK = 64
DN_W = 2 * DN_QK_W + DN_V_W

LANE = 128
MIB = 1024 * 1024

OFF_AQ = 0
OFF_AK = OFF_AQ + ATT_W
OFF_AV = OFF_AK + ATT_KV_W
OFF_AG = OFF_AV + ATT_KV_W
OFF_HV = OFF_AG + ATT_W
OFF_HX1 = OFF_HV + HY_W
OFF_HX2 = OFF_HX1 + HY_W
OFF_HG = OFF_HX2 + HY_W
OFF_DQ = OFF_HG + HY_W
OFF_DV = OFF_DQ + 2 * DN_QK_W
OFF_DZ = OFF_DV + DN_V_W
OFF_DL = OFF_DZ + DN_V_W
N_LOGITS = 4 * DN_V_HEADS
MERGE_ALIGN = 512
OFF_MG = -(-(OFF_DL + N_LOGITS) // MERGE_ALIGN) * MERGE_ALIGN


def _pick(n, cap, mult=LANE):
    best = None
    for t in range(mult, min(n, cap) + 1, mult):
        if n % t == 0:
            best = t
    assert best is not None, (n, cap, mult)
    return best


def _cp(sem, vmem_mib=48):
    return pltpu.CompilerParams(dimension_semantics=sem, vmem_limit_bytes=vmem_mib * MIB)


def _silu(x):
    return x * jax.nn.sigmoid(x)


def _split3(x):
    x1 = x.astype(BF16)
    r = x - x1.astype(F32)
    x2 = r.astype(BF16)
    x3 = (r - x2.astype(F32)).astype(BF16)
    return x1, x2, x3


def _dot(a, b):
    return jnp.dot(a, b, preferred_element_type=F32)


def _dot_hp(a, b):
    a1 = a.astype(BF16)
    a2 = (a - a1.astype(F32)).astype(BF16)
    b1 = b.astype(BF16)
    b2 = (b - b1.astype(F32)).astype(BF16)
    return _dot(a1, b1) + (_dot(a1, b2) + _dot(a2, b1))


def _mod_kernel(cs_ref, w_ref, b_ref, o_ref):
    cs = cs_ref[...]
    o_ref[0] = _dot(_silu(cs).astype(BF16), w_ref[0].astype(BF16)) + b_ref[0]


def _mod_call(cs, w_mod, b_mod):
    depth, d, d3 = w_mod.shape
    tn = _pick(d3, 512)
    return pl.pallas_call(
        _mod_kernel,
        grid=(depth, d3 // tn),
        in_specs=[pl.BlockSpec((8, d), lambda l, j: (0, 0)),
                  pl.BlockSpec((1, d, tn), lambda l, j: (l, 0, j)),
                  pl.BlockSpec((1, 1, tn), lambda l, j: (l, 0, j))],
        out_specs=pl.BlockSpec((1, 8, tn), lambda l, j: (l, 0, j)),
        out_shape=jax.ShapeDtypeStruct((depth, 8, d3), F32),
        compiler_params=_cp(("parallel", "parallel")),
        name="adaln_mod",
    )(cs, w_mod, b_mod.reshape(depth, 1, d3))


def _row_mods(mod_ref, i, r0, rows, tm, tiles_per_batch, s_len, n_batch):
    b = i // tiles_per_batch
    row = (i % tiles_per_batch) * tm + r0 + lax.broadcasted_iota(jnp.int32, (rows, 1), 0)
    return row >= s_len, mod_ref[pl.ds(b, 1), :], mod_ref[n_batch:n_batch + 1, :]


NORM_ROWS = 64


def _modnorm_kernel(x_ref, g_ref, mod_ref, h_ref, *, tm, tiles_per_batch, s_len, n_batch, d):
    i = pl.program_id(0)

    def body(c, carry):
        r0 = pl.multiple_of(c * NORM_ROWS, NORM_ROWS)
        x = x_ref[pl.ds(r0, NORM_ROWS), :]
        y = x * lax.rsqrt(jnp.mean(x * x, axis=-1, keepdims=True) + NORM_EPS) * g_ref[...]
        is_ctx, ml, mc = _row_mods(mod_ref, i, r0, NORM_ROWS, tm, tiles_per_batch, s_len, n_batch)
        shift = jnp.where(is_ctx, mc[:, :d], ml[:, :d])
        scale = jnp.where(is_ctx, mc[:, d:2 * d], ml[:, d:2 * d])
        h_ref[pl.ds(r0, NORM_ROWS), :] = (y * (1.0 + scale) + shift).astype(BF16)
        return carry
    lax.fori_loop(0, tm // NORM_ROWS, body, 0)


def _modnorm_call(xs2, norm_g, mods, *, n_batch, r_len, s_len):
    rt, d = xs2.shape
    tm = _pick(r_len, 576, NORM_ROWS)
    kern = functools.partial(_modnorm_kernel, tm=tm, tiles_per_batch=r_len // tm, s_len=s_len,
                             n_batch=n_batch, d=d)
    return pl.pallas_call(
        kern,
        grid=(rt // tm,),
        in_specs=[pl.BlockSpec((tm, d), lambda i: (i, 0)),
                  pl.BlockSpec((1, d), lambda i: (0, 0)),
                  pl.BlockSpec((8, 3 * d), lambda i: (0, 0))],
        out_specs=pl.BlockSpec((tm, d), lambda i: (i, 0)),
        out_shape=jax.ShapeDtypeStruct((rt, d), BF16),
        compiler_params=_cp(("parallel",)),
        name="mod_norm",
    )(xs2, norm_g.reshape(1, d), mods)


def _inproj_kernel(h_ref, w_ref, o_ref):
    o_ref[...] = _dot(h_ref[...], w_ref[0])


def _split_w_in(w_in):
    cut = OFF_DL + N_LOGITS
    head = jnp.pad(w_in[..., :cut].astype(BF16), ((0, 0), (0, 0), (0, OFF_MG - cut)))
    return head, w_in[..., cut:].astype(BF16)


def _inproj_call(h, w_bf, layer):
    rt, d = h.shape
    nw = w_bf.shape[2]
    tm = _pick(rt, 2176, 16)
    tn = _pick(nw, 1024)
    return pl.pallas_call(
        _inproj_kernel,
        grid=(rt // tm, nw // tn),
        in_specs=[pl.BlockSpec((tm, d), lambda i, j: (i, 0)),
                  pl.BlockSpec((1, d, tn), lambda i, j: (layer, 0, j))],
        out_specs=pl.BlockSpec((tm, tn), lambda i, j: (i, j)),
        out_shape=jax.ShapeDtypeStruct((rt, nw), F32),
        compiler_params=_cp(("parallel", "arbitrary"), 56),
        name="in_proj",
    )(h, w_bf)


def _rope_tables(s_len, ctx_len):
    m = AXIS_ROPE_DIM // 2
    inv_freq = ROPE_THETA ** (-np.arange(0, AXIS_ROPE_DIM, 2, dtype=np.float64) / AXIS_ROPE_DIM)
    t = np.arange(s_len)
    ang_r = (t // GRID_W)[:, None] * inv_freq
    ang_c = (t % GRID_W)[:, None] * inv_freq
    cos = np.concatenate([np.cos(ang_r)] * 2 + [np.cos(ang_c)] * 2, axis=-1)
    sin = np.concatenate([-np.sin(ang_r), np.sin(ang_r), -np.sin(ang_c), np.sin(ang_c)], axis=-1)
    assert cos.shape[1] == 4 * m == HEAD_DIM
    cos = np.concatenate([cos, np.ones((ctx_len, HEAD_DIM))], axis=0)
    sin = np.concatenate([sin, np.zeros((ctx_len, HEAD_DIM))], axis=0)
    return jnp.asarray(cos, F32), jnp.asarray(sin, F32)


def _norm_rope(x, g, cs, sn):
    y = x * lax.rsqrt(jnp.mean(x * x, axis=-1, keepdims=True) + NORM_EPS) * g
    lane = lax.broadcasted_iota(jnp.int32, (1, HEAD_DIM), 1)
    first = (lane % AXIS_ROPE_DIM) < (AXIS_ROPE_DIM // 2)
    q = AXIS_ROPE_DIM // 2
    partner = jnp.where(first, pltpu.roll(y, HEAD_DIM - q, 1), pltpu.roll(y, q, 1))
    return y * cs + partner * sn


def _attn_kernel(q_ref, k_ref, v_ref, gt_ref, cos_ref, sin_ref, qg_ref, kg_ref, o_ref, ks_ref, vs_ref,
                 *, tq, r_len, s_len, kv_chunk):
    i = pl.program_id(2)

    @pl.when(i == 0)
    def _():
        def body(c, carry):
            r0 = pl.multiple_of(c * kv_chunk, kv_chunk)
            kk = _norm_rope(k_ref[0, pl.ds(r0, kv_chunk), :], kg_ref[...],
                            cos_ref[pl.ds(r0, kv_chunk), :], sin_ref[pl.ds(r0, kv_chunk), :])
            ks_ref[pl.ds(r0, kv_chunk), :] = kk.astype(BF16)
            vs_ref[pl.ds(r0, kv_chunk), :] = v_ref[0, pl.ds(r0, kv_chunk), :].astype(BF16)
            return carry
        lax.fori_loop(0, r_len // kv_chunk, body, 0)

    r0 = pl.multiple_of(i * tq, tq)
    cs = cos_ref[pl.ds(r0, tq), :]
    sn = sin_ref[pl.ds(r0, tq), :]
    scale = HEAD_DIM ** -0.5

    def heads(k_lo, k_hi):
        for g in range(ATT_GROUP):
            sl = slice(g * HEAD_DIM, (g + 1) * HEAD_DIM)
            qh = (_norm_rope(q_ref[0, :, sl], qg_ref[...], cs, sn) * scale).astype(BF16)
            s = lax.dot_general(qh, ks_ref[k_lo:k_hi, :], (((1,), (1,)), ((), ())),
                                preferred_element_type=F32)
            e = jnp.exp(s - jnp.max(s, axis=-1, keepdims=True))
            den = jnp.sum(e, axis=-1, keepdims=True)
            o = _dot(e.astype(BF16), vs_ref[k_lo:k_hi, :]) / den
            o_ref[0, :, sl] = (o * _silu(gt_ref[0, :, sl])).astype(o_ref.dtype)

    @pl.when(i < s_len // tq)
    def _():
        heads(0, r_len)

    @pl.when(i >= s_len // tq)
    def _():
        heads(s_len, r_len)


def _attn_call(p3, cos_t, sin_t, q_g, k_g, *, s_len):
    n_batch, r_len, _ = p3.shape
    ctx_len = r_len - s_len
    tq = 256 if (ctx_len % 256 == 0 and s_len % 256 == 0) else 128
    assert ctx_len % tq == 0 and s_len % tq == 0
    gw = ATT_GROUP * HEAD_DIM
    kern = functools.partial(_attn_kernel, tq=tq, r_len=r_len, s_len=s_len, kv_chunk=tq)
    return pl.pallas_call(
        kern,
        grid=(n_batch, ATT_KV_HEADS, r_len // tq),
        in_specs=[pl.BlockSpec((1, tq, gw), lambda b, h, i: (b, i, OFF_AQ // gw + h)),
                  pl.BlockSpec((1, r_len, HEAD_DIM), lambda b, h, i: (b, 0, OFF_AK // HEAD_DIM + h)),
                  pl.BlockSpec((1, r_len, HEAD_DIM), lambda b, h, i: (b, 0, OFF_AV // HEAD_DIM + h)),
                  pl.BlockSpec((1, tq, gw), lambda b, h, i: (b, i, OFF_AG // gw + h)),
                  pl.BlockSpec((r_len, HEAD_DIM), lambda b, h, i: (0, 0)),
                  pl.BlockSpec((r_len, HEAD_DIM), lambda b, h, i: (0, 0)),
                  pl.BlockSpec((1, HEAD_DIM), lambda b, h, i: (0, 0)),
                  pl.BlockSpec((1, HEAD_DIM), lambda b, h, i: (0, 0))],
        out_specs=pl.BlockSpec((1, tq, gw), lambda b, h, i: (b, i, h)),
        out_shape=jax.ShapeDtypeStruct((n_batch, r_len, ATT_W), BF16),
        scratch_shapes=[pltpu.VMEM((r_len, HEAD_DIM), BF16), pltpu.VMEM((r_len, HEAD_DIM), BF16)],
        compiler_params=_cp(("parallel", "parallel", "arbitrary")),
        name="gqa_attention",
    )(p3, p3, p3, p3, cos_t, sin_t, q_g.reshape(1, HEAD_DIM), k_g.reshape(1, HEAD_DIM))


def _conv3(x, w, first, last):
    n = x.shape[0]
    prev = jnp.where(first, 0.0, pltpu.roll(x, 1, 0))
    nxt = jnp.where(last, 0.0, pltpu.roll(x, n - 1, 0))
    return prev * w[0:1, :] + x * w[1:2, :] + nxt * w[2:3, :]


FFT_UNROLL = 16


def _fft_split(n):
    if n <= 512:
        return 1, n
    n2 = 64
    return n // n2, n2


@functools.lru_cache(maxsize=None)
def _fft_mats(n, n1, n2):
    half = n2 // 2
    a = np.arange(n1)[:, None, None]
    k2 = np.arange(n2)[None, :, None]

    def g(bs):
        ph = (k2 * (a + n1 * bs[None, None, :])) % n
        ang = -2.0 * np.pi * ph / n
        return np.cos(ang), np.sin(ang)

    gre, gim = g(np.arange(half))
    gal = np.concatenate([gre, gim], axis=1)
    gar = np.concatenate([-gim, gre], axis=1)
    hre = np.swapaxes(gre, 1, 2) / n
    him = -np.swapaxes(gim, 1, 2) / n
    hal = np.concatenate([hre, him], axis=1)
    har = np.concatenate([-him, hre], axis=1)
    fre_f, fim_f = g(np.arange(n2))
    gf = np.concatenate([fre_f, fim_f], axis=1)
    k1 = np.arange(n1)
    ang1 = -2.0 * np.pi * ((k1[:, None] * k1[None, :]) % n1) / n1
    f1re, f1im = np.cos(ang1), np.sin(ang1)
    fbl = np.concatenate([f1re, f1im], axis=0)
    fbr = np.concatenate([-f1im, f1re], axis=0)
    fbil = np.concatenate([f1re, -f1im], axis=0)
    fbir = np.concatenate([f1im, f1re], axis=0)
    cast = lambda m: jnp.asarray(m, BF16)
    return dict(ga=cast(np.concatenate([gal, gar], axis=2)), ha=cast(np.concatenate([hal, har], axis=2)),
                gf=cast(gf), fb=cast(np.concatenate([fbl, fbr], axis=1)),
                fbi=cast(np.concatenate([fbil, fbir], axis=1)))


@functools.lru_cache(maxsize=None)
def _hy_tables(n_tok):
    pos = np.arange(n_tok, dtype=np.float64)
    t = pos / max(n_tok - 1, 1)
    bands = np.linspace(1e-4, HY_BANDS - 1, HY_BANDS)
    ang = (2.0 * np.pi / n_tok) * pos[:, None] * bands
    z = np.concatenate([t[:, None], np.cos(ang), np.sin(ang)], axis=-1)
    zrev = np.zeros_like(z)
    zrev[1:] = z[:0:-1]
    ztab = np.zeros((2 * n_tok, LANE))
    ztab[:, :HY_EMB] = np.concatenate([z, zrev], axis=0)
    deltas = np.abs(np.linspace(math.log(HY_DECAY_TARGET) / HY_DECAY_SLOW,
                                math.log(HY_DECAY_TARGET) / HY_DECAY_FAST, HY_W))
    return jnp.asarray(ztab, F32), jnp.asarray(np.tile(deltas, HY_ORDER)[None, :], F32)


SLAB_PAD = 8


def _hyfilt_kernel(z_ref, w1_ref, b1_ref, f1_ref, w2_ref, b2_ref, f2_ref, w3_ref, dl_ref, k_ref, s_ref,
                   *, n_tok, tr, slab_rows):
    i = pl.program_id(0)
    z = z_ref[...]
    h = jnp.sin(f1_ref[...] * (_dot_hp(z, w1_ref[...]) + b1_ref[...]))
    h = jnp.sin(f2_ref[...] * (_dot_hp(h, w2_ref[...]) + b2_ref[...]))
    h = _dot(h.astype(BF16), w3_ref[...].astype(BF16)) * jnp.exp(-z[:, 0:1] * dl_ref[...])
    row = i * tr + lax.broadcasted_iota(jnp.int32, (tr, 1), 0)
    h = jnp.where(row == n_tok, 0.0, h)
    if slab_rows is None:
        k_ref[...] = h
    else:
        step = slab_rows + SLAB_PAD
        for s in range(tr // slab_rows):
            k_ref[s * step:s * step + slab_rows, :] = h[s * slab_rows:(s + 1) * slab_rows]
            k_ref[s * step + slab_rows:(s + 1) * step, :] = jnp.zeros((SLAB_PAD, h.shape[1]), F32)

    @pl.when(i == 0)
    def _():
        s_ref[...] = jnp.zeros_like(s_ref)

    s_ref[...] += jnp.sum(jnp.abs(h), axis=0, keepdims=True)


def _hyfilt_call(n_tok, w1p, b1, f1, w2, b2, f2, w3):
    ztab, dl = _hy_tables(n_tok)
    n = 2 * n_tok
    tr = _pick(n_tok, 512, 8)
    ow = HY_ORDER * HY_W
    hid = HY_FILTER_HIDDEN
    n1, _ = _fft_split(n)
    slab_rows = n1 if n1 > 1 else None
    assert slab_rows is None or tr % slab_rows == 0
    out_tr = tr if slab_rows is None else (tr // slab_rows) * (slab_rows + SLAB_PAD)
    kern = functools.partial(_hyfilt_kernel, n_tok=n_tok, tr=tr, slab_rows=slab_rows)
    c2 = lambda i: (0, 0)
    return pl.pallas_call(
        kern,
        grid=(n // tr,),
        in_specs=[pl.BlockSpec((tr, LANE), lambda i: (i, 0)),
                  pl.BlockSpec((LANE, hid), c2), pl.BlockSpec((1, hid), c2), pl.BlockSpec((1, hid), c2),
                  pl.BlockSpec((hid, hid), c2), pl.BlockSpec((1, hid), c2), pl.BlockSpec((1, hid), c2),
                  pl.BlockSpec((hid, ow), lambda i: (0, i // (n_tok // tr))),
                  pl.BlockSpec((1, ow), c2)],
        out_specs=[pl.BlockSpec((out_tr, ow), lambda i: (i, 0)), pl.BlockSpec((1, ow), c2)],
        out_shape=[jax.ShapeDtypeStruct((n // tr * out_tr, ow), F32), jax.ShapeDtypeStruct((1, ow), F32)],
        compiler_params=_cp(("arbitrary",)),
        name="hyena_filter",
    )(ztab, w1p, b1.reshape(1, hid), f1.reshape(1, hid), w2, b2.reshape(1, hid), f2.reshape(1, hid), w3, dl)


FFT_GROUP = 16


def _filtfft_kernel(k_ref, s_ref, gf_ref, fb_ref, o_ref, z_ref, *, n, n1, n2):
    inv = 1.0 / (s_ref[...] + 1e-6)
    if n1 == 1:
        o_ref[...] = _dot(gf_ref[0], k_ref[...].astype(BF16)) * inv
        return
    slab = 2 * n1
    zs, ks = slab + SLAB_PAD, n1 + SLAB_PAD

    def stage_a(a, carry):
        out = _dot(gf_ref[a], k_ref[pl.ds(a, n2, stride=ks), :].astype(BF16))
        z_ref[pl.ds(a, n2, stride=zs), :] = out[:n2]
        z_ref[pl.ds(n1 + a, n2, stride=zs), :] = out[n2:]
        return carry
    lax.fori_loop(0, n1, stage_a, 0, unroll=FFT_UNROLL)

    def stage_b(g, carry):
        for u in range(FFT_GROUP):
            k2 = g * FFT_GROUP + u
            z = z_ref[pl.ds(pl.multiple_of(k2 * zs, 8), slab), :].astype(BF16)
            o_ref[pl.ds(pl.multiple_of(k2 * slab, slab), slab), :] = _dot(fb_ref[...], z) * inv
        return carry
    lax.fori_loop(0, n2 // FFT_GROUP, stage_b, 0)


def _filtfft_call(k_un, asum, n_tok):
    k_rows, ow = k_un.shape
    n = 2 * n_tok
    n1, n2 = _fft_split(n)
    assert k_rows == (n if n1 == 1 else n2 * (n1 + SLAB_PAD))
    m = _fft_mats(n, n1, n2)
    cb = LANE
    kern = functools.partial(_filtfft_kernel, n=n, n1=n1, n2=n2)
    return pl.pallas_call(
        kern,
        grid=(ow // cb,),
        in_specs=[pl.BlockSpec((k_rows, cb), lambda j: (0, j)),
                  pl.BlockSpec((1, cb), lambda j: (0, j)),
                  pl.BlockSpec(m["gf"].shape, lambda j: (0, 0, 0)),
                  pl.BlockSpec(m["fb"].shape, lambda j: (0, 0))],
        out_specs=pl.BlockSpec((2 * n, cb), lambda j: (0, j)),
        out_shape=jax.ShapeDtypeStruct((2 * n, ow), F32),
        scratch_shapes=[pltpu.VMEM((n2 * (2 * n1 + SLAB_PAD) if n1 > 1 else 8, cb), F32)],
        compiler_params=_cp(("parallel",)),
        name="hyena_filter_fft",
    )(k_un, asum, m["gf"], m["fb"])


def _hyena_kernel(*refs, n_tok, n1, n2, conv_a, has_gate):
    it = iter(refs)
    a_ref, m_ref = next(it), next(it)
    g_ref = next(it) if has_gate else None
    if conv_a:
        cwa_ref, cba_ref = next(it), next(it)
    cwm_ref, cbm_ref, d_ref, ks_ref = next(it), next(it), next(it), next(it)
    ga_ref, ha_ref = next(it), next(it)
    if n1 > 1:
        fb_ref, fbi_ref = next(it), next(it)
    o_ref, ac_ref, z_ref = next(it), next(it), next(it)

    n = 2 * n_tok
    half = n2 // 2
    row = lax.broadcasted_iota(jnp.int32, (n_tok, 1), 0)
    first, last = row == 0, row == n_tok - 1
    sr = n1 if n1 > 1 else n_tok
    astep = sr + SLAB_PAD if n1 > 1 else sr
    for b in range(2):
        a = a_ref[b]
        if conv_a:
            a = _conv3(a, cwa_ref[...], first, last) + cba_ref[...]
        for s in range(n_tok // sr):
            ac_ref[b, s * astep:s * astep + sr, :] = a[s * sr:(s + 1) * sr]

    def spectrum_mul(x, k, h):
        xre, xim, kre, kim = x[:h], x[h:], k[:h], k[h:]
        return jnp.concatenate([xre * kre - xim * kim, xre * kim + xim * kre], axis=0).astype(BF16)

    if n1 == 1:
        x = _dot(ga_ref[0], jnp.concatenate([ac_ref[0], ac_ref[1]], axis=0).astype(BF16))
        y = _dot(ha_ref[0], spectrum_mul(x, ks_ref[...], n))
        for b in range(2):
            ac_ref[b] = y[b * n_tok:(b + 1) * n_tok] + ac_ref[b] * d_ref[0]
    else:
        slab = 2 * n1
        zs = slab + SLAB_PAD

        def stage_a(a, carry):
            rows = jnp.concatenate([ac_ref[0, pl.ds(a, half, stride=astep), :],
                                    ac_ref[1, pl.ds(a, half, stride=astep), :]], axis=0).astype(BF16)
            out = _dot(ga_ref[a], rows)
            z_ref[pl.ds(a, n2, stride=zs), :] = out[:n2]
            z_ref[pl.ds(n1 + a, n2, stride=zs), :] = out[n2:]
            return carry
        lax.fori_loop(0, n1, stage_a, 0, unroll=FFT_UNROLL)

        def stage_b(g, carry):
            k2s = [g * FFT_GROUP + u for u in range(FFT_GROUP)]
            zrows = [pl.ds(pl.multiple_of(k2 * zs, 8), slab) for k2 in k2s]
            xs = [_dot(fb_ref[...], z_ref[zr, :].astype(BF16)) for zr in zrows]
            ys = [spectrum_mul(x, ks_ref[pl.ds(pl.multiple_of(k2 * slab, slab), slab), :], n1)
                  for x, k2 in zip(xs, k2s)]
            for y, zr in zip(ys, zrows):
                z_ref[zr, :] = _dot(fbi_ref[...], y)
            return carry
        lax.fori_loop(0, n2 // FFT_GROUP, stage_b, 0)

        def stage_a_inv(a, carry):
            rows = jnp.concatenate([z_ref[pl.ds(a, n2, stride=zs), :],
                                    z_ref[pl.ds(n1 + a, n2, stride=zs), :]], axis=0).astype(BF16)
            y = _dot(ha_ref[a], rows)
            for b in range(2):
                cur = ac_ref[b, pl.ds(a, half, stride=astep), :]
                ac_ref[b, pl.ds(a, half, stride=astep), :] = y[b * half:(b + 1) * half] + cur * d_ref[0]
            return carry
        lax.fori_loop(0, n1, stage_a_inv, 0, unroll=FFT_UNROLL)

    for b in range(2):
        mc = _conv3(m_ref[b], cwm_ref[...], first, last) + cbm_ref[...]
        for s in range(n_tok // sr):
            rows = slice(s * sr, (s + 1) * sr)
            out = mc[rows] * ac_ref[b, s * astep:s * astep + sr, :]
            if has_gate:
                out = out * _silu(g_ref[b, rows, :])
            o_ref[b, rows, :] = out.astype(o_ref.dtype)


def _hyena_call(a_arr, a_blk, m_arr, m_blk, gate, conv_a, conv_m, d_row, spec, spec_blk, *, n_tok, out_dtype):
    n_batch = a_arr.shape[0]
    assert n_batch % 2 == 0
    n = 2 * n_tok
    n1, n2 = _fft_split(n)
    m = _fft_mats(n, n1, n2)
    cb = LANE
    one = pl.Buffered(1)
    tile_mode, spec_mode = (one, pl.Buffered(2)) if gate is not None else (pl.Buffered(2), one)

    def tile(blk):
        return pl.BlockSpec((2, n_tok, cb), lambda p, j, blk=blk: (p, blk[0], blk[1] + j), pipeline_mode=tile_mode)

    def convspec(c0):
        return [pl.BlockSpec((3, cb), lambda p, j, c0=c0: (0, c0 + j)),
                pl.BlockSpec((1, cb), lambda p, j, c0=c0: (0, c0 + j))]

    ins, specs = [a_arr, m_arr], [tile(a_blk), tile(m_blk)]
    if gate is not None:
        ins.append(gate[0])
        specs.append(tile(gate[1]))
    if conv_a is not None:
        ins += [conv_a[0], conv_a[1]]
        specs += convspec(conv_a[2])
    ins += [conv_m[0], conv_m[1], d_row, spec]
    specs += convspec(conv_m[2])
    specs += [pl.BlockSpec((1, 1, cb), lambda p, j: (0, 0, j)),
              pl.BlockSpec((2 * n, cb), lambda p, j, s0=spec_blk: (0, s0 + j), pipeline_mode=spec_mode)]
    names = ["ga", "ha"] + (["fb", "fbi"] if n1 > 1 else [])
    for nm in names:
        ins.append(m[nm])
        specs.append(pl.BlockSpec(m[nm].shape, lambda p, j, nd=m[nm].ndim: (0,) * nd, pipeline_mode=one))
    kern = functools.partial(_hyena_kernel, n_tok=n_tok, n1=n1, n2=n2, conv_a=conv_a is not None,
                             has_gate=gate is not None)
    return pl.pallas_call(
        kern,
        grid=(n_batch // 2, HY_W // cb),
        in_specs=specs,
        out_specs=pl.BlockSpec((2, n_tok, cb), lambda p, j: (p, 0, j)),
        out_shape=jax.ShapeDtypeStruct((n_batch, n_tok, HY_W), out_dtype),
        scratch_shapes=[pltpu.VMEM((2, n_tok // n1 * (n1 + SLAB_PAD) if n1 > 1 else n_tok, cb), F32),
                        pltpu.VMEM((n2 * (2 * n1 + SLAB_PAD) if n1 > 1 else 8, cb), F32)],
        compiler_params=_cp(("parallel", "arbitrary"), 56),
        name="hyena_conv",
    )(*ins)


def _hyena_mixer(p3, row_blk, n_tok, conv_w, conv_b, hy_d, filt):
    k_un, asum = _hyfilt_call(n_tok, *filt)
    spec = _filtfft_call(k_un, asum, n_tok)
    cbias = conv_b.reshape(1, -1)
    cblk = HY_W // LANE
    d3 = hy_d.reshape(HY_ORDER, 1, HY_W)
    z = _hyena_call(p3, (row_blk, OFF_HV // LANE), p3, (row_blk, OFF_HX1 // LANE), None,
                    (conv_w, cbias, 0), (conv_w, cbias, cblk), d3[0:1], spec, 0,
                    n_tok=n_tok, out_dtype=F32)
    return _hyena_call(z, (0, 0), p3, (row_blk, OFF_HX2 // LANE), (p3, (row_blk, OFF_HG // LANE)),
                       None, (conv_w, cbias, 2 * cblk), d3[1:2], spec, cblk,
                       n_tok=n_tok, out_dtype=BF16)


def _softplus(x):
    return jnp.maximum(x, 0.0) + jnp.log1p(jnp.exp(-jnp.abs(x)))


FLIP_ROWS = 256


def _exchange_matrix():
    return jnp.asarray(np.eye(FLIP_ROWS)[::-1], BF16)


def _flip_rows(jm, x):
    x1, x2, x3 = _split3(x)
    return _dot(jm, x1) + (_dot(jm, x2) + _dot(jm, x3))


def _dnprep_kernel(x_ref, w_ref, jm_ref, o_ref, *, r_len, s_len):
    j = pl.program_id(1)
    row = lax.broadcasted_iota(jnp.int32, (r_len, 1), 0)
    first = (row == 0) | (row == s_len)
    last = (row == s_len - 1) | (row == r_len - 1)
    u = _silu(_conv3(x_ref[0], w_ref[...], first, last))
    nrm = u * lax.rsqrt(jnp.sum(u * u, axis=-1, keepdims=True) + 1e-6)
    o_ref[0, 0] = jnp.where(j < 2 * DN_QK_HEADS, nrm, u)
    for seg0, seg_len in ((0, s_len), (s_len, r_len - s_len)):
        nt = seg_len // FLIP_ROWS
        for t in range(nt):
            src = seg0 + t * FLIP_ROWS
            dst = seg0 + (nt - 1 - t) * FLIP_ROWS
            o_ref[1, 0, dst:dst + FLIP_ROWS, :] = _flip_rows(jm_ref[...], o_ref[0, 0, src:src + FLIP_ROWS, :])


def _dnprep_call(p3, conv_w, *, s_len):
    n_batch, r_len, _ = p3.shape
    assert s_len % FLIP_ROWS == 0 and (r_len - s_len) % FLIP_ROWS == 0
    kern = functools.partial(_dnprep_kernel, r_len=r_len, s_len=s_len)
    return pl.pallas_call(
        kern,
        grid=(n_batch, DN_W // LANE),
        in_specs=[pl.BlockSpec((1, r_len, LANE), lambda b, j: (b, 0, OFF_DQ // LANE + j)),
                  pl.BlockSpec((3, LANE), lambda b, j: (0, j)),
                  pl.BlockSpec((FLIP_ROWS, FLIP_ROWS), lambda b, j: (0, 0))],
        out_specs=pl.BlockSpec((2, 1, r_len, LANE), lambda b, j: (0, b, 0, j)),
        out_shape=jax.ShapeDtypeStruct((2, n_batch, r_len, DN_W), F32),
        compiler_params=_cp(("parallel", "parallel")),
        name="deltanet_prep",
    )(p3, conv_w, _exchange_matrix())


def _tri_mats(tr):
    idx = np.arange(tr)
    same = (idx[:, None] // DN_CHUNK) == (idx[None, :] // DN_CHUNK)
    low = same & (idx[:, None] >= idx[None, :])
    return jnp.asarray(low, BF16), jnp.asarray(low.T, BF16), jnp.asarray(same, BF16)


def _dnintra_kernel(u_ref, la_ref, lb_ref, lat_ref, alr_ref, dtr_ref, alc_ref, dtc_ref, low_ref, upp_ref,
                    one_ref, a_ref, qk_ref, be_ref, eg_ref, ek_ref, gt_ref, *, tr):
    g_col = -jnp.exp(alr_ref[0]) * _softplus(la_ref[0, 0] + dtr_ref[0])
    beta = jax.nn.sigmoid(lb_ref[0, 0])
    g1, g2, g3 = _split3(g_col)
    gc_col = _dot(low_ref[...], g1) + (_dot(low_ref[...], g2) + _dot(low_ref[...], g3))
    gt_col = _dot(one_ref[...], g1) + (_dot(one_ref[...], g2) + _dot(one_ref[...], g3))
    be_ref[0, 0] = beta
    eg_ref[0, 0] = jnp.exp(gc_col)
    ek_ref[0, 0] = jnp.exp(gt_col - gc_col)
    gt_ref[0, 0] = jnp.exp(gt_col)
    g_row = -jnp.exp(alc_ref[0]) * _softplus(lat_ref[0, 0] + dtc_ref[0])
    r1, r2, r3 = _split3(g_row)
    gc_row = _dot(r1, upp_ref[...]) + (_dot(r2, upp_ref[...]) + _dot(r3, upp_ref[...]))

    ii = lax.broadcasted_iota(jnp.int32, (DN_CHUNK, DN_CHUNK), 0)
    jj = lax.broadcasted_iota(jnp.int32, (DN_CHUNK, DN_CHUNK), 1)
    scale = DN_HEAD_DIM ** -0.5
    nt = (((1,), (1,)), ((), ()))
    for c in range(tr // DN_CHUNK):
        rows = slice(c * DN_CHUNK, (c + 1) * DN_CHUNK)
        for hq in range(DN_QK_HEADS):
            q = u_ref[0, 0, rows, hq * DN_HEAD_DIM:(hq + 1) * DN_HEAD_DIM].astype(BF16)
            k = u_ref[0, 0, rows, DN_QK_W + hq * DN_HEAD_DIM:DN_QK_W + (hq + 1) * DN_HEAD_DIM].astype(BF16)
            kk = lax.dot_general(k, k, nt, preferred_element_type=F32)
            qk = lax.dot_general(q, k, nt, preferred_element_type=F32) * scale
            for h in range(hq * (DN_V_HEADS // DN_QK_HEADS), (hq + 1) * (DN_V_HEADS // DN_QK_HEADS)):
                diff = gc_col[rows, h:h + 1] - gc_row[h:h + 1, rows]
                dec = jnp.where(ii >= jj, jnp.exp(jnp.minimum(diff, 0.0)), 0.0)
                a_ref[0, 0, h, c] = jnp.where(ii > jj, kk * beta[rows, h:h + 1] * dec, 0.0)
                qk_ref[0, 0, h, c] = (qk * dec).astype(qk_ref.dtype)


def _dnintra_call(u2, la, lb, lat, alog, dtb):
    n_dir, n_batch, r_len, _ = u2.shape
    tr = 4 * DN_CHUNK
    nc = r_len // DN_CHUNK
    low, upp, one = _tri_mats(tr)
    kern = functools.partial(_dnintra_kernel, tr=tr)
    pad = lambda v: jnp.pad(v, ((0, 0), (0, LANE - DN_V_HEADS))).reshape(n_dir, 1, LANE)
    col = lambda v: v.reshape(n_dir, DN_V_HEADS, 1)
    cm = lambda d, b, t: (0, 0)
    gspec = pl.BlockSpec((1, 1, tr, LANE), lambda d, b, t: (d, b, t, 0))
    mspec = pl.BlockSpec((1, 1, DN_V_HEADS, tr // DN_CHUNK, DN_CHUNK, DN_CHUNK), lambda d, b, t: (d, b, 0, t, 0, 0))
    gshape = jax.ShapeDtypeStruct((n_dir, n_batch, r_len, LANE), F32)
    mshape = jax.ShapeDtypeStruct((n_dir, n_batch, DN_V_HEADS, nc, DN_CHUNK, DN_CHUNK), F32)
    return pl.pallas_call(
        kern,
        grid=(n_dir, n_batch, r_len // tr),
        in_specs=[pl.BlockSpec((1, 1, tr, 2 * DN_QK_W), lambda d, b, t: (d, b, t, 0)),
                  gspec, gspec,
                  pl.BlockSpec((1, 1, DN_V_HEADS, tr), lambda d, b, t: (d, b, 0, t)),
                  pl.BlockSpec((1, 1, LANE), lambda d, b, t: (d, 0, 0)),
                  pl.BlockSpec((1, 1, LANE), lambda d, b, t: (d, 0, 0)),
                  pl.BlockSpec((1, DN_V_HEADS, 1), lambda d, b, t: (d, 0, 0)),
                  pl.BlockSpec((1, DN_V_HEADS, 1), lambda d, b, t: (d, 0, 0)),
                  pl.BlockSpec((tr, tr), cm), pl.BlockSpec((tr, tr), cm), pl.BlockSpec((tr, tr), cm)],
        out_specs=[mspec, mspec, gspec, gspec, gspec, gspec],
        out_shape=[mshape, jax.ShapeDtypeStruct(mshape.shape, BF16), gshape, gshape, gshape, gshape],
        compiler_params=_cp(("parallel", "parallel", "parallel")),
        name="deltanet_intra",
    )(u2, la, lb, lat, pad(alog), pad(dtb), col(alog), col(dtb), low, upp, one)


def _dnsolve_kernel(a_ref, o_ref, at_ref, tt_ref):
    c = DN_CHUNK
    for blk in range(c * c // LANE):
        at_ref[blk * LANE:(blk + 1) * LANE, :] = a_ref[:, blk * LANE:(blk + 1) * LANE].T
    tt_ref[...] = jnp.zeros_like(tt_ref)
    for i in range(c):
        nr = 8 * (i // 8 + 1)
        rr = lax.broadcasted_iota(jnp.int32, (nr, LANE), 0)
        acc = jnp.where(rr == i, 1.0, 0.0)

        def body(j, acc, i=i, nr=nr):
            a = at_ref[pl.ds(i * c + j, 1), :]
            return acc - a * tt_ref[pl.ds(pl.multiple_of(j * c, c), nr), :]
        if i > 0:
            acc = lax.fori_loop(0, i, body, acc, unroll=min(i, 8))
        tt_ref[i * c:i * c + nr, :] = acc
    for blk in range(c * c // LANE):
        o_ref[:, blk * LANE:(blk + 1) * LANE] = tt_ref[blk * LANE:(blk + 1) * LANE, :].T.astype(o_ref.dtype)


def _dnsolve_call(a2):
    ni, cc = a2.shape
    assert ni % LANE == 0
    return pl.pallas_call(
        _dnsolve_kernel,
        grid=(ni // LANE,),
        in_specs=[pl.BlockSpec((LANE, cc), lambda i: (i, 0))],
        out_specs=pl.BlockSpec((LANE, cc), lambda i: (i, 0)),
        out_shape=jax.ShapeDtypeStruct((ni, cc), BF16),
        scratch_shapes=[pltpu.VMEM((cc, LANE), F32), pltpu.VMEM((cc, LANE), F32)],
        compiler_params=_cp(("parallel",)),
        name="deltanet_solve",
    )(a2)


def _dnscan_kernel(u_ref, t_ref, qk_ref, be_ref, eg_ref, ek_ref, gt_ref, o_ref, *s_refs, n_dir, n_batch):
    @pl.when(pl.program_id(0) == 0)
    def _():
        for s_ref in s_refs:
            s_ref[...] = jnp.zeros_like(s_ref)

    scale = DN_HEAD_DIM ** -0.5
    rep = DN_V_HEADS // DN_QK_HEADS
    streams = [(d, b, h) for d in range(n_dir) for b in range(n_batch) for h in range(DN_V_HEADS)]

    def qkv(d, b, h):
        hq = h // rep
        q = u_ref[d, b, :, hq * DN_HEAD_DIM:(hq + 1) * DN_HEAD_DIM]
        k = u_ref[d, b, :, DN_QK_W + hq * DN_HEAD_DIM:DN_QK_W + (hq + 1) * DN_HEAD_DIM]
        v = u_ref[d, b, :, 2 * DN_QK_W + h * DN_HEAD_DIM:2 * DN_QK_W + (h + 1) * DN_HEAD_DIM]
        return q, k, v

    uws, egs = [], []
    for d, b, h in streams:
        _, k, v = qkv(d, b, h)
        be = be_ref[d, b, :, h:h + 1]
        egs.append(jnp.broadcast_to(eg_ref[d, b, :, h:h + 1], (DN_CHUNK, DN_HEAD_DIM)))
        rhs = jnp.concatenate([v * be, (k * be) * egs[-1]], axis=1).astype(BF16)
        uws.append(_dot(t_ref[d, b, h, 0], rhs))
    wqs = []
    for i, (d, b, h) in enumerate(streams):
        q, _, _ = qkv(d, b, h)
        lhs = jnp.concatenate([uws[i][:, DN_HEAD_DIM:], q * (scale * egs[i])], axis=0)
        wqs.append(_dot(lhs.astype(BF16), s_refs[i][...].astype(BF16)))
    for i, (d, b, h) in enumerate(streams):
        _, k, _ = qkv(d, b, h)
        vnb = (uws[i][:, :DN_HEAD_DIM] - wqs[i][:DN_CHUNK]).astype(BF16)
        o_ref[d, b, :, h * DN_HEAD_DIM:(h + 1) * DN_HEAD_DIM] = (
            wqs[i][DN_CHUNK:] + _dot(qk_ref[d, b, h, 0], vnb))
        kd = (k * ek_ref[d, b, :, h:h + 1]).astype(BF16)
        s_refs[i][...] = s_refs[i][...] * gt_ref[d, b, 0:1, h:h + 1] + lax.dot_general(
            kd, vnb, (((0,), (0,)), ((), ())), preferred_element_type=F32)


def _dnscan_call(u2, t6, qk6, be, eg, ek, gt, *, s_len):
    n_dir, n_batch, r_len, _ = u2.shape
    nc = r_len // DN_CHUNK
    ncl = s_len // DN_CHUNK

    def cidx(t):
        return jnp.where(t < nc - ncl, ncl + t, t - (nc - ncl))

    kern = functools.partial(_dnscan_kernel, n_dir=n_dir, n_batch=n_batch)
    gspec = pl.BlockSpec((n_dir, n_batch, DN_CHUNK, LANE), lambda t: (0, 0, cidx(t), 0))
    mspec = pl.BlockSpec((n_dir, n_batch, DN_V_HEADS, 1, DN_CHUNK, DN_CHUNK), lambda t: (0, 0, 0, cidx(t), 0, 0))
    return pl.pallas_call(
        kern,
        grid=(nc,),
        in_specs=[pl.BlockSpec((n_dir, n_batch, DN_CHUNK, DN_W), lambda t: (0, 0, cidx(t), 0)),
                  mspec, mspec, gspec, gspec, gspec, gspec],
        out_specs=pl.BlockSpec((n_dir, n_batch, DN_CHUNK, DN_V_W), lambda t: (0, 0, cidx(t), 0)),
        out_shape=jax.ShapeDtypeStruct((n_dir, n_batch, r_len, DN_V_W), F32),
        scratch_shapes=[pltpu.VMEM((DN_HEAD_DIM, DN_HEAD_DIM), F32)] * (n_dir * n_batch * DN_V_HEADS),
        compiler_params=_cp(("arbitrary",)),
        name="deltanet_scan",
    )(u2, t6, qk6, be, eg, ek, gt)


DN_OUT_COLS = 512


def _dnout_kernel(of_ref, ob_ref, z_ref, g_ref, jm_ref, o_ref):
    for h in range(DN_OUT_COLS // DN_HEAD_DIM):
        sl = slice(h * DN_HEAD_DIM, (h + 1) * DN_HEAD_DIM)
        o = of_ref[0, 0, :, sl] + _flip_rows(jm_ref[...], ob_ref[0, 0, :, sl])
        y = o * lax.rsqrt(jnp.mean(o * o, axis=-1, keepdims=True) + NORM_EPS) * g_ref[...]
        o_ref[0, :, sl] = (y * _silu(z_ref[0, :, sl])).astype(o_ref.dtype)


def _dnout_call(o2, p3, norm_g, *, s_len):
    _, n_batch, r_len, _ = o2.shape
    tr = FLIP_ROWS
    cw = DN_OUT_COLS
    ns, nc = s_len // tr, (r_len - s_len) // tr
    assert OFF_DZ % cw == 0

    def mirror(i):
        return jnp.where(i < ns, ns - 1 - i, 2 * ns + nc - 1 - i)

    return pl.pallas_call(
        _dnout_kernel,
        grid=(n_batch, r_len // tr, DN_V_W // cw),
        in_specs=[pl.BlockSpec((1, 1, tr, cw), lambda b, i, j: (0, b, i, j)),
                  pl.BlockSpec((1, 1, tr, cw), lambda b, i, j: (1, b, mirror(i), j)),
                  pl.BlockSpec((1, tr, cw), lambda b, i, j: (b, i, OFF_DZ // cw + j)),
                  pl.BlockSpec((1, DN_HEAD_DIM), lambda b, i, j: (0, 0)),
                  pl.BlockSpec((FLIP_ROWS, FLIP_ROWS), lambda b, i, j: (0, 0))],
        out_specs=pl.BlockSpec((1, tr, cw), lambda b, i, j: (b, i, j)),
        out_shape=jax.ShapeDtypeStruct((n_batch, r_len, DN_V_W), BF16),
        compiler_params=_cp(("parallel", "parallel", "parallel")),
        name="deltanet_out",
    )(o2, o2, p3, norm_g.reshape(1, DN_HEAD_DIM), _exchange_matrix())


def _seq_flip(a, s_len, axis):
    lat, ctx = jnp.split(a, [s_len], axis=axis)
    return jnp.concatenate([jnp.flip(lat, axis), jnp.flip(ctx, axis)], axis=axis)


def _deltanet_mixer(p3, conv_w, a_log, dt_bias, norm_g, *, s_len):
    n_batch, r_len, _ = p3.shape
    u2 = _dnprep_call(p3, conv_w, s_len=s_len)
    lg =p3[:, :, OFF_DL:OFF_DL + N_LOGITS].reshape(n_batch, r_len, 2, 2, DN_V_HEADS)

    def dirs(x):
        return jnp.stack([x[:, :, 0], _seq_flip(x[:, :, 1], s_len, 1)])

    la, lb = dirs(lg[:, :, 0]), dirs(lg[:, :, 1])
    padl = lambda x: jnp.pad(x, ((0, 0), (0, 0), (0, 0), (0, LANE - DN_V_HEADS)))
    a6, qk6, be, eg, ek, gt = _dnintra_call(u2, padl(la), padl(lb), jnp.swapaxes(la, 2, 3), a_log, dt_bias)
    t6 = _dnsolve_call(a6.reshape(-1, DN_CHUNK * DN_CHUNK)).reshape(a6.shape)
    o2 = _dnscan_call(u2, t6, qk6, be, eg, ek, gt, s_len=s_len)
    return _dnout_call(o2, p3, norm_g, s_len=s_len)


def _merge_kernel(ya_ref, yb_ref, yc_ref, wa_ref, wb_ref, wc_ref, ga_ref, gb_ref, gc_ref, o_ref,
                  wa_s, wb_s, wc_s):
    @pl.when(pl.program_id(1) == 0)
    def _():
        wa_s[...] = wa_ref[0].astype(BF16)
        wb_s[...] = wb_ref[0].astype(BF16)
        wc_s[...] = wc_ref[0].astype(BF16)

    m = (jax.nn.sigmoid(ga_ref[...]) * _dot(ya_ref[...], wa_s[...])
         + jax.nn.sigmoid(gb_ref[...]) * _dot(yb_ref[...], wb_s[...])
         + jax.nn.sigmoid(gc_ref[...]) * _dot(yc_ref[...], wc_s[...]))
    o_ref[...] = m.astype(o_ref.dtype)


def _merge_call(ya, yb, yc, w_pa, w_pb, w_pc, layer, p2, *, r_len):
    rt = ya.shape[0]
    d = w_pa.shape[2]
    tm = _pick(r_len, 1088, 16)
    tn = _pick(d, MERGE_ALIGN)
    yspec = lambda w: pl.BlockSpec((tm, w), lambda j, i: (i, 0))
    wspec = lambda w: pl.BlockSpec((1, w, tn), lambda j, i: (layer, 0, j))
    gspec = lambda br: pl.BlockSpec((tm, tn), lambda j, i, br=br: (i, br * d // tn + j))
    return pl.pallas_call(
        _merge_kernel,
        grid=(d // tn, rt // tm),
        in_specs=[yspec(ATT_W), yspec(HY_W), yspec(DN_V_W), wspec(ATT_W), wspec(HY_W), wspec(DN_V_W),
                  gspec(0), gspec(1), gspec(2)],
        out_specs=pl.BlockSpec((tm, tn), lambda j, i: (i, j)),
        out_shape=jax.ShapeDtypeStruct((rt, d), BF16),
        scratch_shapes=[pltpu.VMEM((ATT_W, tn), BF16), pltpu.VMEM((HY_W, tn), BF16),
                        pltpu.VMEM((DN_V_W, tn), BF16)],
        compiler_params=_cp(("parallel", "arbitrary"), 56),
        name="branch_merge",
    )(ya, yb, yc, w_pa, w_pb, w_pc, p2, p2, p2)


def _outproj_kernel(m_ref, w_ref, x_ref, gate_ref, o_ref, w_s, *, tm, tiles_per_batch, s_len, n_batch):
    @pl.when(pl.program_id(1) == 0)
    def _():
        w_s[...] = w_ref[0].astype(BF16)

    is_ctx, gl, gc = _row_mods(gate_ref, pl.program_id(1), 0, tm, tm, tiles_per_batch, s_len, n_batch)
    o_ref[...] = x_ref[...] + jnp.where(is_ctx, gc, gl) * _dot(m_ref[...], w_s[...])


def _outproj_call(m, w_out, layer, xs2, mods, *, n_batch, r_len, s_len):
    rt, d = xs2.shape
    tm = _pick(r_len, 1088, 16)
    tn = _pick(d, 512)
    kern = functools.partial(_outproj_kernel, tm=tm, tiles_per_batch=r_len // tm, s_len=s_len, n_batch=n_batch)
    return pl.pallas_call(
        kern,
        grid=(d // tn, rt // tm),
        in_specs=[pl.BlockSpec((tm, d), lambda j, i: (i, 0)),
                  pl.BlockSpec((1, d, tn), lambda j, i: (layer, 0, j)),
                  pl.BlockSpec((tm, tn), lambda j, i: (i, j)),
                  pl.BlockSpec((8, tn), lambda j, i: (0, 2 * d // tn + j))],
        out_specs=pl.BlockSpec((tm, tn), lambda j, i: (i, j)),
        out_shape=jax.ShapeDtypeStruct((rt, d), F32),
        scratch_shapes=[pltpu.VMEM((d, tn), BF16)],
        compiler_params=_cp(("parallel", "arbitrary")),
        name="out_proj_residual",
    )(m, w_out, xs2, mods)


def _finalnorm_kernel(x_ref, g_ref, o_ref):
    x = x_ref[0]
    o_ref[0] = x * lax.rsqrt(jnp.mean(x * x, axis=-1, keepdims=True) + NORM_EPS) * g_ref[...]


def _finalnorm_call(xs, final_g, *, s_len):
    n_batch, _, d = xs.shape
    tr = _pick(s_len, 512, 8)
    return pl.pallas_call(
        _finalnorm_kernel,
        grid=(n_batch, s_len // tr),
        in_specs=[pl.BlockSpec((1, tr, d), lambda b, i: (b, i, 0)), pl.BlockSpec((1, d), lambda b, i: (0, 0))],
        out_specs=pl.BlockSpec((1, tr, d), lambda b, i: (b, i, 0)),
        out_shape=jax.ShapeDtypeStruct((n_batch, s_len, d), F32),
        compiler_params=_cp(("parallel", "parallel")),
        name="final_norm",
    )(xs, final_g.reshape(1, d))


def kernel(x, c, ctx, c_ctx, norm_g, w_mod, b_mod, w_in, q_norm_g, k_norm_g, hy_conv_w, hy_conv_b, hy_w1, hy_b1, hy_freq1, hy_w2, hy_b2, hy_freq2, hy_w3, hy_d, dn_conv_w, dn_a_log, dn_dt_bias, dn_norm_g, w_pa, w_pb, w_pc, w_out, final_g):
    n_batch, s_len, d = x.shape
    ctx_len = ctx.shape[1]
    r_len = s_len + ctx_len
    depth = w_in.shape[0]
    assert n_batch + 1 <= 8 and w_in.shape[2] == OFF_DL + N_LOGITS + N_BRANCH * d

    xs = jnp.concatenate([x, ctx], axis=1)
    cs = jnp.zeros((8, d), F32).at[:n_batch].set(c).at[n_batch].set(c_ctx)
    mods = _mod_call(cs, w_mod, b_mod)
    cos_t, sin_t = _rope_tables(s_len, ctx_len)
    w1p = jnp.pad(hy_w1, ((0, 0), (0, LANE - HY_EMB), (0, 0)))
    w_head, w_gate = _split_w_in(w_in)

    for layer in range(depth):
        need_ctx = layer < depth - 1
        xs2 = xs.reshape(n_batch * r_len, d)
        h = _modnorm_call(xs2, norm_g[layer], mods[layer], n_batch=n_batch, r_len=r_len, s_len=s_len)
        p3 = _inproj_call(h, w_head, layer).reshape(n_batch, r_len, -1)
        p_gate = _inproj_call(h, w_gate, layer)

        ya = _attn_call(p3, cos_t, sin_t, q_norm_g[layer], k_norm_g[layer], s_len=s_len)

        filt = (w1p[layer], hy_b1[layer], hy_freq1[layer], hy_w2[layer], hy_b2[layer], hy_freq2[layer],
                hy_w3[layer])
        yb = _hyena_mixer(p3, 0, s_len, hy_conv_w[layer], hy_conv_b[layer], hy_d[layer], filt)
        if need_ctx:
            yb_c = _hyena_mixer(p3, s_len // ctx_len, ctx_len, hy_conv_w[layer], hy_conv_b[layer],
                                hy_d[layer], filt)
        else:
            yb_c = jnp.zeros((n_batch, ctx_len, HY_W), BF16)
        yb = jnp.concatenate([yb, yb_c], axis=1)

        yc = _deltanet_mixer(p3, dn_conv_w[layer], dn_a_log[layer], dn_dt_bias[layer], dn_norm_g[layer],
                             s_len=s_len)

        rt = n_batch * r_len
        m = _merge_call(ya.reshape(rt, ATT_W), yb.reshape(rt, HY_W), yc.reshape(rt, DN_V_W),
                        w_pa, w_pb, w_pc, layer, p_gate, r_len=r_len)
        xs = _outproj_call(m, w_out, layer, xs2, mods[layer],
                           n_batch=n_batch, r_len=r_len, s_len=s_len).reshape(n_batch, r_len, d)

    return _finalnorm_call(xs, final_g, s_len=s_len)
```

```python
import functools
import math

import jax
import jax.numpy as jnp
import numpy as np
from jax import lax
from jax.experimental import pallas as pl
from jax.experimental.pallas import tpu as pltpu

F32 = jnp.float32
BF16 = jnp.bfloat16

GRID_W = 64
NORM_EPS = 1e-6
N_BRANCH = 3

ATT_HEADS = 8
ATT_KV_HEADS = 2
HEAD_DIM = 128
ATT_GROUP = ATT_HEADS // ATT_KV_HEADS
ATT_W = ATT_HEADS * HEAD_DIM
ATT_KV_W = ATT_KV_HEADS * HEAD_DIM
AXIS_ROPE_DIM = HEAD_DIM // 2
ROPE_THETA = 10000.0

HY_W = 1024
HY_ORDER = 2
HY_BANDS = 16
HY_EMB = 1 + 2 * HY_BANDS
HY_FILTER_HIDDEN = 64
HY_DECAY_TARGET = 1e-2
HY_DECAY_FAST = 0.3
HY_DECAY_SLOW = 1.5

DN_QK_HEADS = 4
DN_V_HEADS = 8
DN_HEAD_DIM = 128
DN_QK_W = DN_QK_HEADS * DN_HEAD_DIM
DN_V_W = DN_V_HEADS * DN_HEAD_DIM
DN_CHUNK = 64
DN_W = 2 * DN_QK_W + DN_V_W

LANE = 128
MIB = 1024 * 1024

OFF_AQ = 0
OFF_AK = OFF_AQ + ATT_W
OFF_AV = OFF_AK + ATT_KV_W
OFF_AG = OFF_AV + ATT_KV_W
OFF_HV = OFF_AG + ATT_W
OFF_HX1 = OFF_HV + HY_W
OFF_HX2 = OFF_HX1 + HY_W
OFF_HG = OFF_HX2 + HY_W
OFF_DQ = OFF_HG + HY_W
OFF_DV = OFF_DQ + 2 * DN_QK_W
OFF_DZ = OFF_DV + DN_V_W
OFF_DL = OFF_DZ + DN_V_W
N_LOGITS = 4 * DN_V_HEADS
MERGE_ALIGN = 512
OFF_MG = -(-(OFF_DL + N_LOGITS) // MERGE_ALIGN) * MERGE_ALIGN


def _pick(n, cap, mult=LANE):
    best = None
    for t in range(mult, min(n, cap) + 1, mult):
        if n % t == 0:
            best = t
    assert best is not None, (n, cap, mult)
    return best


def _cp(sem, vmem_mib=48):
    return pltpu.CompilerParams(dimension_semantics=sem, vmem_limit_bytes=vmem_mib * MIB)


def _silu(x):
    return x * jax.nn.sigmoid(x)


def _split3(x):
    x1 = x.astype(BF16)
    r = x - x1.astype(F32)
    x2 = r.astype(BF16)
    x3 = (r - x2.astype(F32)).astype(BF16)
    return x1, x2, x3


def _dot(a, b):
    return jnp.dot(a, b, preferred_element_type=F32)


def _dot_hp(a, b):
    a1 = a.astype(BF16)
    a2 = (a - a1.astype(F32)).astype(BF16)
    b1 = b.astype(BF16)
    b2 = (b - b1.astype(F32)).astype(BF16)
    return _dot(a1, b1) + (_dot(a1, b2) + _dot(a2, b1))


def _mod_kernel(cs_ref, w_ref, b_ref, o_ref):
    cs = cs_ref[...]
    o_ref[0] = _dot(_silu(cs).astype(BF16), w_ref[0].astype(BF16)) + b_ref[0]


def _mod_call(cs, w_mod, b_mod):
    depth, d, d3 = w_mod.shape
    tn = _pick(d3, 512)
    return pl.pallas_call(
        _mod_kernel,
        grid=(depth, d3 // tn),
        in_specs=[pl.BlockSpec((8, d), lambda l, j: (0, 0)),
                  pl.BlockSpec((1, d, tn), lambda l, j: (l, 0, j)),
                  pl.BlockSpec((1, 1, tn), lambda l, j: (l, 0, j))],
        out_specs=pl.BlockSpec((1, 8, tn), lambda l, j: (l, 0, j)),
        out_shape=jax.ShapeDtypeStruct((depth, 8, d3), F32),
        compiler_params=_cp(("parallel", "parallel")),
        name="adaln_mod",
    )(cs, w_mod, b_mod.reshape(depth, 1, d3))


def _row_mods(mod_ref, i, r0, rows, tm, tiles_per_batch, s_len, n_batch):
    b = i // tiles_per_batch
    row = (i % tiles_per_batch) * tm + r0 + lax.broadcasted_iota(jnp.int32, (rows, 1), 0)
    return row >= s_len, mod_ref[pl.ds(b, 1), :], mod_ref[n_batch:n_batch + 1, :]


NORM_ROWS = 64


def _modnorm_kernel(x_ref, g_ref, mod_ref, h_ref, *, tm, tiles_per_batch, s_len, n_batch, d):
    i = pl.program_id(0)

    def body(c, carry):
        r0 = pl.multiple_of(c * NORM_ROWS, NORM_ROWS)
        x = x_ref[pl.ds(r0, NORM_ROWS), :]
        y = x * lax.rsqrt(jnp.mean(x * x, axis=-1, keepdims=True) + NORM_EPS) * g_ref[...]
        is_ctx, ml, mc = _row_mods(mod_ref, i, r0, NORM_ROWS, tm, tiles_per_batch, s_len, n_batch)
        shift = jnp.where(is_ctx, mc[:, :d], ml[:, :d])
        scale = jnp.where(is_ctx, mc[:, d:2 * d], ml[:, d:2 * d])
        h_ref[pl.ds(r0, NORM_ROWS), :] = (y * (1.0 + scale) + shift).astype(BF16)
        return carry
    lax.fori_loop(0, tm // NORM_ROWS, body, 0)


def _modnorm_call(xs2, norm_g, mods, *, n_batch, r_len, s_len):
    rt, d = xs2.shape
    tm = _pick(r_len, 576, NORM_ROWS)
    kern = functools.partial(_modnorm_kernel, tm=tm, tiles_per_batch=r_len // tm, s_len=s_len,
                             n_batch=n_batch, d=d)
    return pl.pallas_call(
        kern,
        grid=(rt // tm,),
        in_specs=[pl.BlockSpec((tm, d), lambda i: (i, 0)),
                  pl.BlockSpec((1, d), lambda i: (0, 0)),
                  pl.BlockSpec((8, 3 * d), lambda i: (0, 0))],
        out_specs=pl.BlockSpec((tm, d), lambda i: (i, 0)),
        out_shape=jax.ShapeDtypeStruct((rt, d), BF16),
        compiler_params=_cp(("parallel",)),
        name="mod_norm",
    )(xs2, norm_g.reshape(1, d), mods)


def _inproj_kernel(h_ref, w_ref, o_ref):
    o_ref[...] = _dot(h_ref[...], w_ref[0])


def _split_w_in(w_in):
    cut = OFF_DL + N_LOGITS
    head = jnp.pad(w_in[..., :cut].astype(BF16), ((0, 0), (0, 0), (0, OFF_MG - cut)))
    return head, w_in[..., cut:].astype(BF16)


def _inproj_call(h, w_bf, layer):
    rt, d = h.shape
    nw = w_bf.shape[2]
    tm = _pick(rt, 2176, 16)
    tn = _pick(nw, 1024)
    return pl.pallas_call(
        _inproj_kernel,
        grid=(rt // tm, nw // tn),
        in_specs=[pl.BlockSpec((tm, d), lambda i, j: (i, 0)),
                  pl.BlockSpec((1, d, tn), lambda i, j: (layer, 0, j))],
        out_specs=pl.BlockSpec((tm, tn), lambda i, j: (i, j)),
        out_shape=jax.ShapeDtypeStruct((rt, nw), F32),
        compiler_params=_cp(("parallel", "arbitrary"), 56),
        name="in_proj",
    )(h, w_bf)


def _rope_tables(s_len, ctx_len):
    m = AXIS_ROPE_DIM // 2
    inv_freq = ROPE_THETA ** (-np.arange(0, AXIS_ROPE_DIM, 2, dtype=np.float64) / AXIS_ROPE_DIM)
    t = np.arange(s_len)
    ang_r = (t // GRID_W)[:, None] * inv_freq
    ang_c = (t % GRID_W)[:, None] * inv_freq
    cos = np.concatenate([np.cos(ang_r)] * 2 + [np.cos(ang_c)] * 2, axis=-1)
    sin = np.concatenate([-np.sin(ang_r), np.sin(ang_r), -np.sin(ang_c), np.sin(ang_c)], axis=-1)
    assert cos.shape[1] == 4 * m == HEAD_DIM
    cos = np.concatenate([cos, np.ones((ctx_len, HEAD_DIM))], axis=0)
    sin = np.concatenate([sin, np.zeros((ctx_len, HEAD_DIM))], axis=0)
    return jnp.asarray(cos, F32), jnp.asarray(sin, F32)


def _norm_rope(x, g, cs, sn):
    y = x * lax.rsqrt(jnp.mean(x * x, axis=-1, keepdims=True) + NORM_EPS) * g
    lane = lax.broadcasted_iota(jnp.int32, (1, HEAD_DIM), 1)
    first = (lane % AXIS_ROPE_DIM) < (AXIS_ROPE_DIM // 2)
    q = AXIS_ROPE_DIM // 2
    partner = jnp.where(first, pltpu.roll(y, HEAD_DIM - q, 1), pltpu.roll(y, q, 1))
    return y * cs + partner * sn


def _attn_kernel(q_ref, k_ref, v_ref, gt_ref, cos_ref, sin_ref, qg_ref, kg_ref, o_ref, ks_ref, vs_ref,
                 *, tq, r_len, s_len, kv_chunk):
    i = pl.program_id(2)

    @pl.when(i == 0)
    def _():
        def body(c, carry):
            r0 = pl.multiple_of(c * kv_chunk, kv_chunk)
            kk = _norm_rope(k_ref[0, pl.ds(r0, kv_chunk), :], kg_ref[...],
                            cos_ref[pl.ds(r0, kv_chunk), :], sin_ref[pl.ds(r0, kv_chunk), :])
            ks_ref[pl.ds(r0, kv_chunk), :] = kk.astype(BF16)
            vs_ref[pl.ds(r0, kv_chunk), :] = v_ref[0, pl.ds(r0, kv_chunk), :].astype(BF16)
            return carry
        lax.fori_loop(0, r_len // kv_chunk, body, 0)

    r0 = pl.multiple_of(i * tq, tq)
    cs = cos_ref[pl.ds(r0, tq), :]
    sn = sin_ref[pl.ds(r0, tq), :]
    scale = HEAD_DIM ** -0.5

    def heads(k_lo, k_hi):
        sls = [slice(g * HEAD_DIM, (g + 1) * HEAD_DIM) for g in range(ATT_GROUP)]
        scores = []
        for sl in sls:
            qh = (_norm_rope(q_ref[0, :, sl], qg_ref[...], cs, sn) * scale).astype(BF16)
            scores.append(lax.dot_general(qh, ks_ref[k_lo:k_hi, :], (((1,), (1,)), ((), ())),
                                          preferred_element_type=F32))
        probs = []
        for s in scores:
            e = jnp.exp(s - jnp.max(s, axis=-1, keepdims=True))
            probs.append((e.astype(BF16), jnp.sum(e, axis=-1, keepdims=True)))
        for sl, (e, den) in zip(sls, probs):
            o = _dot(e, vs_ref[k_lo:k_hi, :]) / den
            o_ref[0, :, sl] = (o * _silu(gt_ref[0, :, sl])).astype(o_ref.dtype)

    @pl.when(i < s_len // tq)
    def _():
        heads(0, r_len)

    @pl.when(i >= s_len // tq)
    def _():
        heads(s_len, r_len)


def _attn_call(p3, cos_t, sin_t, q_g, k_g, *, s_len):
    n_batch, r_len, _ = p3.shape
    ctx_len = r_len - s_len
    tq = 256 if (ctx_len % 256 == 0 and s_len % 256 == 0) else 128
    assert ctx_len % tq == 0 and s_len % tq == 0
    gw = ATT_GROUP * HEAD_DIM
    kern = functools.partial(_attn_kernel, tq=tq, r_len=r_len, s_len=s_len, kv_chunk=tq)
    return pl.pallas_call(
        kern,
        grid=(n_batch, ATT_KV_HEADS, r_len // tq),
        in_specs=[pl.BlockSpec((1, tq, gw), lambda b, h, i: (b, i, OFF_AQ // gw + h)),
                  pl.BlockSpec((1, r_len, HEAD_DIM), lambda b, h, i: (b, 0, OFF_AK // HEAD_DIM + h)),
                  pl.BlockSpec((1, r_len, HEAD_DIM), lambda b, h, i: (b, 0, OFF_AV // HEAD_DIM + h)),
                  pl.BlockSpec((1, tq, gw), lambda b, h, i: (b, i, OFF_AG // gw + h)),
                  pl.BlockSpec((r_len, HEAD_DIM), lambda b, h, i: (0, 0)),
                  pl.BlockSpec((r_len, HEAD_DIM), lambda b, h, i: (0, 0)),
                  pl.BlockSpec((1, HEAD_DIM), lambda b, h, i: (0, 0)),
                  pl.BlockSpec((1, HEAD_DIM), lambda b, h, i: (0, 0))],
        out_specs=pl.BlockSpec((1, tq, gw), lambda b, h, i: (b, i, h)),
        out_shape=jax.ShapeDtypeStruct((n_batch, r_len, ATT_W), BF16),
        scratch_shapes=[pltpu.VMEM((r_len, HEAD_DIM), BF16), pltpu.VMEM((r_len, HEAD_DIM), BF16)],
        compiler_params=_cp(("parallel", "parallel", "arbitrary")),
        name="gqa_attention",
    )(p3, p3, p3, p3, cos_t, sin_t, q_g.reshape(1, HEAD_DIM), k_g.reshape(1, HEAD_DIM))


def _conv3(x, w, first, last):
    n = x.shape[0]
    prev = jnp.where(first, 0.0, pltpu.roll(x, 1, 0))
    nxt = jnp.where(last, 0.0, pltpu.roll(x, n - 1, 0))
    return prev * w[0:1, :] + x * w[1:2, :] + nxt * w[2:3, :]


FFT_UNROLL = 16


def _fft_split(n):
    if n <= 512:
        return 1, n
    n2 = 64
    return n // n2, n2


@functools.lru_cache(maxsize=None)
def _fft_mats(n, n1, n2):
    half = n2 // 2
    a = np.arange(n1)[:, None, None]
    k2 = np.arange(n2)[None, :, None]

    def g(bs):
        ph = (k2 * (a + n1 * bs[None, None, :])) % n
        ang = -2.0 * np.pi * ph / n
        return np.cos(ang), np.sin(ang)

    gre, gim = g(np.arange(half))
    gal = np.concatenate([gre, gim], axis=1)
    gar = np.concatenate([-gim, gre], axis=1)
    hre = np.swapaxes(gre, 1, 2) / n
    him = -np.swapaxes(gim, 1, 2) / n
    hal = np.concatenate([hre, him], axis=1)
    har = np.concatenate([-him, hre], axis=1)
    fre_f, fim_f = g(np.arange(n2))
    gf = np.concatenate([fre_f, fim_f], axis=1)
    k1 = np.arange(n1)
    ang1 = -2.0 * np.pi * ((k1[:, None] * k1[None, :]) % n1) / n1
    f1re, f1im = np.cos(ang1), np.sin(ang1)
    fbl = np.concatenate([f1re, f1im], axis=0)
    fbr = np.concatenate([-f1im, f1re], axis=0)
    fbil = np.concatenate([f1re, -f1im], axis=0)
    fbir = np.concatenate([f1im, f1re], axis=0)
    cast = lambda m: jnp.asarray(m, BF16)
    return dict(ga=cast(np.concatenate([gal, gar], axis=2)), ha=cast(np.concatenate([hal, har], axis=2)),
                gf=cast(gf), fb=cast(np.concatenate([fbl, fbr], axis=1)),
                fbi=cast(np.concatenate([fbil, fbir], axis=1)))


@functools.lru_cache(maxsize=None)
def _hy_tables(n_tok):
    pos = np.arange(n_tok, dtype=np.float64)
    t = pos / max(n_tok - 1, 1)
    bands = np.linspace(1e-4, HY_BANDS - 1, HY_BANDS)
    ang = (2.0 * np.pi / n_tok) * pos[:, None] * bands
    z = np.concatenate([t[:, None], np.cos(ang), np.sin(ang)], axis=-1)
    zrev = np.zeros_like(z)
    zrev[1:] = z[:0:-1]
    ztab = np.zeros((2 * n_tok, LANE))
    ztab[:, :HY_EMB] = np.concatenate([z, zrev], axis=0)
    deltas = np.abs(np.linspace(math.log(HY_DECAY_TARGET) / HY_DECAY_SLOW,
                                math.log(HY_DECAY_TARGET) / HY_DECAY_FAST, HY_W))
    return jnp.asarray(ztab, F32), jnp.asarray(np.tile(deltas, HY_ORDER)[None, :], F32)


SLAB_PAD = 8


def _hyfilt_kernel(z_ref, w1_ref, b1_ref, f1_ref, w2_ref, b2_ref, f2_ref, w3_ref, dl_ref, k_ref, s_ref,
                   *, n_tok, tr, slab_rows):
    i = pl.program_id(0)
    z = z_ref[...]
    h = jnp.sin(f1_ref[...] * (_dot_hp(z, w1_ref[...]) + b1_ref[...]))
    h = jnp.sin(f2_ref[...] * (_dot_hp(h, w2_ref[...]) + b2_ref[...]))
    h = _dot(h.astype(BF16), w3_ref[...].astype(BF16)) * jnp.exp(-z[:, 0:1] * dl_ref[...])
    row = i * tr + lax.broadcasted_iota(jnp.int32, (tr, 1), 0)
    h = jnp.where(row == n_tok, 0.0, h)
    if slab_rows is None:
        k_ref[...] = h
    else:
        step = slab_rows + SLAB_PAD
        for s in range(tr // slab_rows):
            k_ref[s * step:s * step + slab_rows, :] = h[s * slab_rows:(s + 1) * slab_rows]
            k_ref[s * step + slab_rows:(s + 1) * step, :] = jnp.zeros((SLAB_PAD, h.shape[1]), F32)

    @pl.when(i == 0)
    def _():
        s_ref[...] = jnp.zeros_like(s_ref)

    s_ref[...] += jnp.sum(jnp.abs(h), axis=0, keepdims=True)


def _hyfilt_call(n_tok, w1p, b1, f1, w2, b2, f2, w3):
    ztab, dl = _hy_tables(n_tok)
    n = 2 * n_tok
    tr = _pick(n_tok, 512, 8)
    ow = HY_ORDER * HY_W
    hid = HY_FILTER_HIDDEN
    n1, _ = _fft_split(n)
    slab_rows = n1 if n1 > 1 else None
    assert slab_rows is None or tr % slab_rows == 0
    out_tr = tr if slab_rows is None else (tr // slab_rows) * (slab_rows + SLAB_PAD)
    kern = functools.partial(_hyfilt_kernel, n_tok=n_tok, tr=tr, slab_rows=slab_rows)
    c2 = lambda i: (0, 0)
    return pl.pallas_call(
        kern,
        grid=(n // tr,),
        in_specs=[pl.BlockSpec((tr, LANE), lambda i: (i, 0)),
                  pl.BlockSpec((LANE, hid), c2), pl.BlockSpec((1, hid), c2), pl.BlockSpec((1, hid), c2),
                  pl.BlockSpec((hid, hid), c2), pl.BlockSpec((1, hid), c2), pl.BlockSpec((1, hid), c2),
                  pl.BlockSpec((hid, ow), lambda i: (0, i // (n_tok // tr))),
                  pl.BlockSpec((1, ow), c2)],
        out_specs=[pl.BlockSpec((out_tr, ow), lambda i: (i, 0)), pl.BlockSpec((1, ow), c2)],
        out_shape=[jax.ShapeDtypeStruct((n // tr * out_tr, ow), F32), jax.ShapeDtypeStruct((1, ow), F32)],
        compiler_params=_cp(("arbitrary",)),
        name="hyena_filter",
    )(ztab, w1p, b1.reshape(1, hid), f1.reshape(1, hid), w2, b2.reshape(1, hid), f2.reshape(1, hid), w3, dl)


FFT_GROUP = 16


def _filtfft_kernel(k_ref, s_ref, gf_ref, fb_ref, o_ref, z_ref, *, n, n1, n2):
    inv = 1.0 / (s_ref[...] + 1e-6)
    if n1 == 1:
        o_ref[...] = _dot(gf_ref[0], k_ref[...].astype(BF16)) * inv
        return
    slab = 2 * n1
    zs, ks = slab + SLAB_PAD, n1 + SLAB_PAD

    def stage_a(a, carry):
        out = _dot(gf_ref[a], k_ref[pl.ds(a, n2, stride=ks), :].astype(BF16))
        z_ref[pl.ds(a, n2, stride=zs), :] = out[:n2]
        z_ref[pl.ds(n1 + a, n2, stride=zs), :] = out[n2:]
        return carry
    lax.fori_loop(0, n1, stage_a, 0, unroll=FFT_UNROLL)

    def stage_b(g, carry):
        for u in range(FFT_GROUP):
            k2 = g * FFT_GROUP + u
            z = z_ref[pl.ds(pl.multiple_of(k2 * zs, 8), slab), :].astype(BF16)
            o_ref[pl.ds(pl.multiple_of(k2 * slab, slab), slab), :] = _dot(fb_ref[...], z) * inv
        return carry
    lax.fori_loop(0, n2 // FFT_GROUP, stage_b, 0)


def _filtfft_call(k_un, asum, n_tok):
    k_rows, ow = k_un.shape
    n = 2 * n_tok
    n1, n2 = _fft_split(n)
    assert k_rows == (n if n1 == 1 else n2 * (n1 + SLAB_PAD))
    m = _fft_mats(n, n1, n2)
    cb = LANE
    kern = functools.partial(_filtfft_kernel, n=n, n1=n1, n2=n2)
    return pl.pallas_call(
        kern,
        grid=(ow // cb,),
        in_specs=[pl.BlockSpec((k_rows, cb), lambda j: (0, j)),
                  pl.BlockSpec((1, cb), lambda j: (0, j)),
                  pl.BlockSpec(m["gf"].shape, lambda j: (0, 0, 0)),
                  pl.BlockSpec(m["fb"].shape, lambda j: (0, 0))],
        out_specs=pl.BlockSpec((2 * n, cb), lambda j: (0, j)),
        out_shape=jax.ShapeDtypeStruct((2 * n, ow), F32),
        scratch_shapes=[pltpu.VMEM((n2 * (2 * n1 + SLAB_PAD) if n1 > 1 else 8, cb), F32)],
        compiler_params=_cp(("parallel",)),
        name="hyena_filter_fft",
    )(k_un, asum, m["gf"], m["fb"])


def _hyena_kernel(*refs, n_tok, n1, n2, conv_a, has_gate):
    it = iter(refs)
    a_ref, m_ref = next(it), next(it)
    g_ref = next(it) if has_gate else None
    if conv_a:
        cwa_ref, cba_ref = next(it), next(it)
    cwm_ref, cbm_ref, d_ref, ks_ref = next(it), next(it), next(it), next(it)
    ga_ref, ha_ref = next(it), next(it)
    if n1 > 1:
        fb_ref, fbi_ref = next(it), next(it)
    o_ref, ac_ref, z_ref = next(it), next(it), next(it)

    n = 2 * n_tok
    half = n2 // 2
    row = lax.broadcasted_iota(jnp.int32, (n_tok, 1), 0)
    first, last = row == 0, row == n_tok - 1
    sr = n1 if n1 > 1 else n_tok
    astep = sr + SLAB_PAD if n1 > 1 else sr
    for b in range(2):
        a = a_ref[b]
        if conv_a:
            a = _conv3(a, cwa_ref[...], first, last) + cba_ref[...]
        for s in range(n_tok // sr):
            ac_ref[b, s * astep:s * astep + sr, :] = a[s * sr:(s + 1) * sr]

    def spectrum_mul(x, k, h):
        xre, xim, kre, kim = x[:h], x[h:], k[:h], k[h:]
        return jnp.concatenate([xre * kre - xim * kim, xre * kim + xim * kre], axis=0).astype(BF16)

    if n1 == 1:
        x = _dot(ga_ref[0], jnp.concatenate([ac_ref[0], ac_ref[1]], axis=0).astype(BF16))
        y = _dot(ha_ref[0], spectrum_mul(x, ks_ref[...], n))
        for b in range(2):
            ac_ref[b] = y[b * n_tok:(b + 1) * n_tok] + ac_ref[b] * d_ref[0]
    else:
        slab = 2 * n1
        zs = slab + SLAB_PAD

        def stage_a(a, carry):
            rows = jnp.concatenate([ac_ref[0, pl.ds(a, half, stride=astep), :],
                                    ac_ref[1, pl.ds(a, half, stride=astep), :]], axis=0).astype(BF16)
            out = _dot(ga_ref[a], rows)
            z_ref[pl.ds(a, n2, stride=zs), :] = out[:n2]
            z_ref[pl.ds(n1 + a, n2, stride=zs), :] = out[n2:]
            return carry
        lax.fori_loop(0, n1, stage_a, 0, unroll=FFT_UNROLL)

        def stage_b(g, carry):
            k2s = [g * FFT_GROUP + u for u in range(FFT_GROUP)]
            zrows = [pl.ds(pl.multiple_of(k2 * zs, 8), slab) for k2 in k2s]
            xs = [_dot(fb_ref[...], z_ref[zr, :].astype(BF16)) for zr in zrows]
            ys = [spectrum_mul(x, ks_ref[pl.ds(pl.multiple_of(k2 * slab, slab), slab), :], n1)
                  for x, k2 in zip(xs, k2s)]
            for y, zr in zip(ys, zrows):
                z_ref[zr, :] = _dot(fbi_ref[...], y)
            return carry
        lax.fori_loop(0, n2 // FFT_GROUP, stage_b, 0)

        def stage_a_inv(a, carry):
            rows = jnp.concatenate([z_ref[pl.ds(a, n2, stride=zs), :],
                                    z_ref[pl.ds(n1 + a, n2, stride=zs), :]], axis=0).astype(BF16)
            y = _dot(ha_ref[a], rows)
            for b in range(2):
                cur = ac_ref[b, pl.ds(a, half, stride=astep), :]
                ac_ref[b, pl.ds(a, half, stride=astep), :] = y[b * half:(b + 1) * half] + cur * d_ref[0]
            return carry
        lax.fori_loop(0, n1, stage_a_inv, 0, unroll=FFT_UNROLL)

    for b in range(2):
        mc = _conv3(m_ref[b], cwm_ref[...], first, last) + cbm_ref[...]
        for s in range(n_tok // sr):
            rows = slice(s * sr, (s + 1) * sr)
            out = mc[rows] * ac_ref[b, s * astep:s * astep + sr, :]
            if has_gate:
                out = out * _silu(g_ref[b, rows, :])
            o_ref[b, rows, :] = out.astype(o_ref.dtype)


def _hyena_call(a_arr, a_blk, m_arr, m_blk, gate, conv_a, conv_m, d_row, spec, spec_blk, *, n_tok, out_dtype):
    n_batch = a_arr.shape[0]
    assert n_batch % 2 == 0
    n = 2 * n_tok
    n1, n2 = _fft_split(n)
    m = _fft_mats(n, n1, n2)
    cb = LANE
    one = pl.Buffered(1)
    tile_mode, spec_mode = (one, pl.Buffered(2)) if gate is not None else (pl.Buffered(2), one)

    def tile(blk):
        return pl.BlockSpec((2, n_tok, cb), lambda p, j, blk=blk: (p, blk[0], blk[1] + j), pipeline_mode=tile_mode)

    def convspec(c0):
        return [pl.BlockSpec((3, cb), lambda p, j, c0=c0: (0, c0 + j)),
                pl.BlockSpec((1, cb), lambda p, j, c0=c0: (0, c0 + j))]

    ins, specs = [a_arr, m_arr], [tile(a_blk), tile(m_blk)]
    if gate is not None:
        ins.append(gate[0])
        specs.append(tile(gate[1]))
    if conv_a is not None:
        ins += [conv_a[0], conv_a[1]]
        specs += convspec(conv_a[2])
    ins += [conv_m[0], conv_m[1], d_row, spec]
    specs += convspec(conv_m[2])
    specs += [pl.BlockSpec((1, 1, cb), lambda p, j: (0, 0, j)),
              pl.BlockSpec((2 * n, cb), lambda p, j, s0=spec_blk: (0, s0 + j), pipeline_mode=spec_mode)]
    names = ["ga", "ha"] + (["fb", "fbi"] if n1 > 1 else [])
    for nm in names:
        ins.append(m[nm])
        specs.append(pl.BlockSpec(m[nm].shape, lambda p, j, nd=m[nm].ndim: (0,) * nd, pipeline_mode=one))
    kern = functools.partial(_hyena_kernel, n_tok=n_tok, n1=n1, n2=n2, conv_a=conv_a is not None,
                             has_gate=gate is not None)
    return pl.pallas_call(
        kern,
        grid=(n_batch // 2, HY_W // cb),
        in_specs=specs,
        out_specs=pl.BlockSpec((2, n_tok, cb), lambda p, j: (p, 0, j)),
        out_shape=jax.ShapeDtypeStruct((n_batch, n_tok, HY_W), out_dtype),
        scratch_shapes=[pltpu.VMEM((2, n_tok // n1 * (n1 + SLAB_PAD) if n1 > 1 else n_tok, cb), F32),
                        pltpu.VMEM((n2 * (2 * n1 + SLAB_PAD) if n1 > 1 else 8, cb), F32)],
        compiler_params=_cp(("parallel", "arbitrary"), 56),
        name="hyena_conv",
    )(*ins)


def _hyena_mixer(p3, row_blk, n_tok, conv_w, conv_b, hy_d, filt):
    k_un, asum = _hyfilt_call(n_tok, *filt)
    spec = _filtfft_call(k_un, asum, n_tok)
    cbias = conv_b.reshape(1, -1)
    cblk = HY_W // LANE
    d3 = hy_d.reshape(HY_ORDER, 1, HY_W)
    z = _hyena_call(p3, (row_blk, OFF_HV // LANE), p3, (row_blk, OFF_HX1 // LANE), None,
                    (conv_w, cbias, 0), (conv_w, cbias, cblk), d3[0:1], spec, 0,
                    n_tok=n_tok, out_dtype=F32)
    return _hyena_call(z, (0, 0), p3, (row_blk, OFF_HX2 // LANE), (p3, (row_blk, OFF_HG // LANE)),
                       None, (conv_w, cbias, 2 * cblk), d3[1:2], spec, cblk,
                       n_tok=n_tok, out_dtype=BF16)


def _softplus(x):
    return jnp.maximum(x, 0.0) + jnp.log1p(jnp.exp(-jnp.abs(x)))


FLIP_ROWS = 256


def _exchange_matrix():
    return jnp.asarray(np.eye(FLIP_ROWS)[::-1], BF16)


def _flip_rows(jm, x):
    x1, x2, x3 = _split3(x)
    return _dot(jm, x1) + (_dot(jm, x2) + _dot(jm, x3))


def _dnprep_kernel(x_ref, w_ref, jm_ref, o_ref, *, r_len, s_len):
    j = pl.program_id(1)
    row = lax.broadcasted_iota(jnp.int32, (r_len, 1), 0)
    first = (row == 0) | (row == s_len)
    last = (row == s_len - 1) | (row == r_len - 1)
    u = _silu(_conv3(x_ref[0], w_ref[...], first, last))
    nrm = u * lax.rsqrt(jnp.sum(u * u, axis=-1, keepdims=True) + 1e-6)
    o_ref[0, 0] = jnp.where(j < 2 * DN_QK_HEADS, nrm, u)
    for seg0, seg_len in ((0, s_len), (s_len, r_len - s_len)):
        nt = seg_len // FLIP_ROWS
        for t in range(nt):
            src = seg0 + t * FLIP_ROWS
            dst = seg0 + (nt - 1 - t) * FLIP_ROWS
            o_ref[1, 0, dst:dst + FLIP_ROWS, :] = _flip_rows(jm_ref[...], o_ref[0, 0, src:src + FLIP_ROWS, :])


def _dnprep_call(p3, conv_w, *, s_len):
    n_batch, r_len, _ = p3.shape
    assert s_len % FLIP_ROWS == 0 and (r_len - s_len) % FLIP_ROWS == 0
    kern = functools.partial(_dnprep_kernel, r_len=r_len, s_len=s_len)
    return pl.pallas_call(
        kern,
        grid=(n_batch, DN_W // LANE),
        in_specs=[pl.BlockSpec((1, r_len, LANE), lambda b, j: (b, 0, OFF_DQ // LANE + j)),
                  pl.BlockSpec((3, LANE), lambda b, j: (0, j)),
                  pl.BlockSpec((FLIP_ROWS, FLIP_ROWS), lambda b, j: (0, 0))],
        out_specs=pl.BlockSpec((2, 1, r_len, LANE), lambda b, j: (0, b, 0, j)),
        out_shape=jax.ShapeDtypeStruct((2, n_batch, r_len, DN_W), F32),
        compiler_params=_cp(("parallel", "parallel")),
        name="deltanet_prep",
    )(p3, conv_w, _exchange_matrix())


def _tri_mats(tr):
    idx = np.arange(tr)
    same = (idx[:, None] // DN_CHUNK) == (idx[None, :] // DN_CHUNK)
    low = same & (idx[:, None] >= idx[None, :])
    return jnp.asarray(low, BF16), jnp.asarray(low.T, BF16), jnp.asarray(same, BF16)


def _dnintra_kernel(u_ref, la_ref, lb_ref, lat_ref, alr_ref, dtr_ref, alc_ref, dtc_ref, low_ref, upp_ref,
                    one_ref, a_ref, qk_ref, be_ref, eg_ref, ek_ref, gt_ref, *, tr):
    g_col = -jnp.exp(alr_ref[0]) * _softplus(la_ref[0, 0] + dtr_ref[0])
    beta = jax.nn.sigmoid(lb_ref[0, 0])
    g1, g2, g3 = _split3(g_col)
    gc_col = _dot(low_ref[...], g1) + (_dot(low_ref[...], g2) + _dot(low_ref[...], g3))
    gt_col = _dot(one_ref[...], g1) + (_dot(one_ref[...], g2) + _dot(one_ref[...], g3))
    be_ref[0, 0] = beta
    eg_ref[0, 0] = jnp.exp(gc_col)
    ek_ref[0, 0] = jnp.exp(gt_col - gc_col)
    gt_ref[0, 0] = jnp.exp(gt_col)
    g_row = -jnp.exp(alc_ref[0]) * _softplus(lat_ref[0, 0] + dtc_ref[0])
    r1, r2, r3 = _split3(g_row)
    gc_row = _dot(r1, upp_ref[...]) + (_dot(r2, upp_ref[...]) + _dot(r3, upp_ref[...]))

    ii = lax.broadcasted_iota(jnp.int32, (DN_CHUNK, DN_CHUNK), 0)
    jj = lax.broadcasted_iota(jnp.int32, (DN_CHUNK, DN_CHUNK), 1)
    scale = DN_HEAD_DIM ** -0.5
    nt = (((1,), (1,)), ((), ()))
    for c in range(tr // DN_CHUNK):
        rows = slice(c * DN_CHUNK, (c + 1) * DN_CHUNK)
        for hq in range(DN_QK_HEADS):
            q = u_ref[0, 0, rows, hq * DN_HEAD_DIM:(hq + 1) * DN_HEAD_DIM].astype(BF16)
            k = u_ref[0, 0, rows, DN_QK_W + hq * DN_HEAD_DIM:DN_QK_W + (hq + 1) * DN_HEAD_DIM].astype(BF16)
            kk = lax.dot_general(k, k, nt, preferred_element_type=F32)
            qk = lax.dot_general(q, k, nt, preferred_element_type=F32) * scale
            for h in range(hq * (DN_V_HEADS // DN_QK_HEADS), (hq + 1) * (DN_V_HEADS // DN_QK_HEADS)):
                diff = gc_col[rows, h:h + 1] - gc_row[h:h + 1, rows]
                dec = jnp.where(ii >= jj, jnp.exp(jnp.minimum(diff, 0.0)), 0.0)
                a_ref[0, 0, h, c] = jnp.where(ii > jj, kk * beta[rows, h:h + 1] * dec, 0.0)
                qk_ref[0, 0, h, c] = (qk * dec).astype(qk_ref.dtype)


def _dnintra_call(u2, la, lb, lat, alog, dtb):
    n_dir, n_batch, r_len, _ = u2.shape
    tr = 4 * DN_CHUNK
    nc = r_len // DN_CHUNK
    low, upp, one = _tri_mats(tr)
    kern = functools.partial(_dnintra_kernel, tr=tr)
    pad = lambda v: jnp.pad(v, ((0, 0), (0, LANE - DN_V_HEADS))).reshape(n_dir, 1, LANE)
    col = lambda v: v.reshape(n_dir, DN_V_HEADS, 1)
    cm = lambda d, b, t: (0, 0)
    gspec = pl.BlockSpec((1, 1, tr, LANE), lambda d, b, t: (d, b, t, 0))
    mspec = pl.BlockSpec((1, 1, DN_V_HEADS, tr // DN_CHUNK, DN_CHUNK, DN_CHUNK), lambda d, b, t: (d, b, 0, t, 0, 0))
    gshape = jax.ShapeDtypeStruct((n_dir, n_batch, r_len, LANE), F32)
    mshape = jax.ShapeDtypeStruct((n_dir, n_batch, DN_V_HEADS, nc, DN_CHUNK, DN_CHUNK), F32)
    return pl.pallas_call(
        kern,
        grid=(n_dir, n_batch, r_len // tr),
        in_specs=[pl.BlockSpec((1, 1, tr, 2 * DN_QK_W), lambda d, b, t: (d, b, t, 0)),
                  gspec, gspec,
                  pl.BlockSpec((1, 1, DN_V_HEADS, tr), lambda d, b, t: (d, b, 0, t)),
                  pl.BlockSpec((1, 1, LANE), lambda d, b, t: (d, 0, 0)),
                  pl.BlockSpec((1, 1, LANE), lambda d, b, t: (d, 0, 0)),
                  pl.BlockSpec((1, DN_V_HEADS, 1), lambda d, b, t: (d, 0, 0)),
                  pl.BlockSpec((1, DN_V_HEADS, 1), lambda d, b, t: (d, 0, 0)),
                  pl.BlockSpec((tr, tr), cm), pl.BlockSpec((tr, tr), cm), pl.BlockSpec((tr, tr), cm)],
        out_specs=[mspec, mspec, gspec, gspec, gspec, gspec],
        out_shape=[mshape, jax.ShapeDtypeStruct(mshape.shape, BF16), gshape, gshape, gshape, gshape],
        compiler_params=_cp(("parallel", "parallel", "parallel")),
        name="deltanet_intra",
    )(u2, la, lb, lat, pad(alog), pad(dtb), col(alog), col(dtb), low, upp, one)


def _dnsolve_kernel(a_ref, o_ref, at_ref, tt_ref):
    c = DN_CHUNK
    for blk in range(c * c // LANE):
        at_ref[blk * LANE:(blk + 1) * LANE, :] = a_ref[:, blk * LANE:(blk + 1) * LANE].T
    tt_ref[...] = jnp.zeros_like(tt_ref)
    for i in range(c):
        nr = 8 * (i // 8 + 1)
        rr = lax.broadcasted_iota(jnp.int32, (nr, LANE), 0)
        acc = jnp.where(rr == i, 1.0, 0.0)

        def body(j, acc, i=i, nr=nr):
            a = at_ref[pl.ds(i * c + j, 1), :]
            return acc - a * tt_ref[pl.ds(pl.multiple_of(j * c, c), nr), :]
        if i > 0:
            acc = lax.fori_loop(0, i, body, acc, unroll=min(i, 8))
        tt_ref[i * c:i * c + nr, :] = acc
    for blk in range(c * c // LANE):
        o_ref[:, blk * LANE:(blk + 1) * LANE] = tt_ref[blk * LANE:(blk + 1) * LANE, :].T.astype(o_ref.dtype)


def _dnsolve_call(a2):
    ni, cc = a2.shape
    assert ni % LANE == 0
    return pl.pallas_call(
        _dnsolve_kernel,
        grid=(ni // LANE,),
        in_specs=[pl.BlockSpec((LANE, cc), lambda i: (i, 0))],
        out_specs=pl.BlockSpec((LANE, cc), lambda i: (i, 0)),
        out_shape=jax.ShapeDtypeStruct((ni, cc), BF16),
        scratch_shapes=[pltpu.VMEM((cc, LANE), F32), pltpu.VMEM((cc, LANE), F32)],
        compiler_params=_cp(("parallel",)),
        name="deltanet_solve",
    )(a2)


def _dnscan_kernel(u_ref, t_ref, qk_ref, be_ref, eg_ref, ek_ref, gt_ref, o_ref, *s_refs, n_dir, n_batch):
    @pl.when(pl.program_id(0) == 0)
    def _():
        for s_ref in s_refs:
            s_ref[...] = jnp.zeros_like(s_ref)

    scale = DN_HEAD_DIM ** -0.5
    rep = DN_V_HEADS // DN_QK_HEADS
    streams = [(d, b, h) for d in range(n_dir) for b in range(n_batch) for h in range(DN_V_HEADS)]

    def qkv(d, b, h):
        hq = h // rep
        q = u_ref[d, b, :, hq * DN_HEAD_DIM:(hq + 1) * DN_HEAD_DIM]
        k = u_ref[d, b, :, DN_QK_W + hq * DN_HEAD_DIM:DN_QK_W + (hq + 1) * DN_HEAD_DIM]
        v = u_ref[d, b, :, 2 * DN_QK_W + h * DN_HEAD_DIM:2 * DN_QK_W + (h + 1) * DN_HEAD_DIM]
        return q, k, v

    uws, egs = [], []
    for d, b, h in streams:
        _, k, v = qkv(d, b, h)
        be = be_ref[d, b, :, h:h + 1]
        egs.append(jnp.broadcast_to(eg_ref[d, b, :, h:h + 1], (DN_CHUNK, DN_HEAD_DIM)))
        rhs = jnp.concatenate([v * be, (k * be) * egs[-1]], axis=1).astype(BF16)
        uws.append(_dot(t_ref[d, b, h, 0], rhs))
    wqs = []
    for i, (d, b, h) in enumerate(streams):
        q, _, _ = qkv(d, b, h)
        lhs = jnp.concatenate([uws[i][:, DN_HEAD_DIM:], q * (scale * egs[i])], axis=0)
        wqs.append(_dot(lhs.astype(BF16), s_refs[i][...].astype(BF16)))
    for i, (d, b, h) in enumerate(streams):
        _, k, _ = qkv(d, b, h)
        vnb = (uws[i][:, :DN_HEAD_DIM] - wqs[i][:DN_CHUNK]).astype(BF16)
        o_ref[d, b, :, h * DN_HEAD_DIM:(h + 1) * DN_HEAD_DIM] = (
            wqs[i][DN_CHUNK:] + _dot(qk_ref[d, b, h, 0], vnb))
        kd = (k * ek_ref[d, b, :, h:h + 1]).astype(BF16)
        s_refs[i][...] = s_refs[i][...] * gt_ref[d, b, 0:1, h:h + 1] + lax.dot_general(
            kd, vnb, (((0,), (0,)), ((), ())), preferred_element_type=F32)


def _dnscan_call(u2, t6, qk6, be, eg, ek, gt, *, s_len):
    n_dir, n_batch, r_len, _ = u2.shape
    nc = r_len // DN_CHUNK
    ncl = s_len // DN_CHUNK

    def cidx(t):
        return jnp.where(t < nc - ncl, ncl + t, t - (nc - ncl))

    kern = functools.partial(_dnscan_kernel, n_dir=n_dir, n_batch=n_batch)
    gspec = pl.BlockSpec((n_dir, n_batch, DN_CHUNK, LANE), lambda t: (0, 0, cidx(t), 0))
    mspec = pl.BlockSpec((n_dir, n_batch, DN_V_HEADS, 1, DN_CHUNK, DN_CHUNK), lambda t: (0, 0, 0, cidx(t), 0, 0))
    return pl.pallas_call(
        kern,
        grid=(nc,),
        in_specs=[pl.BlockSpec((n_dir, n_batch, DN_CHUNK, DN_W), lambda t: (0, 0, cidx(t), 0)),
                  mspec, mspec, gspec, gspec, gspec, gspec],
        out_specs=pl.BlockSpec((n_dir, n_batch, DN_CHUNK, DN_V_W), lambda t: (0, 0, cidx(t), 0)),
        out_shape=jax.ShapeDtypeStruct((n_dir, n_batch, r_len, DN_V_W), F32),
        scratch_shapes=[pltpu.VMEM((DN_HEAD_DIM, DN_HEAD_DIM), F32)] * (n_dir * n_batch * DN_V_HEADS),
        compiler_params=_cp(("arbitrary",)),
        name="deltanet_scan",
    )(u2, t6, qk6, be, eg, ek, gt)


DN_OUT_COLS = 512


def _dnout_kernel(of_ref, ob_ref, z_ref, g_ref, jm_ref, o_ref):
    for h in range(DN_OUT_COLS // DN_HEAD_DIM):
        sl = slice(h * DN_HEAD_DIM, (h + 1) * DN_HEAD_DIM)
        o = of_ref[0, 0, :, sl] + _flip_rows(jm_ref[...], ob_ref[0, 0, :, sl])
        y = o * lax.rsqrt(jnp.mean(o * o, axis=-1, keepdims=True) + NORM_EPS) * g_ref[...]
        o_ref[0, :, sl] = (y * _silu(z_ref[0, :, sl])).astype(o_ref.dtype)


def _dnout_call(o2, p3, norm_g, *, s_len):
    _, n_batch, r_len, _ = o2.shape
    tr = FLIP_ROWS
    cw = DN_OUT_COLS
    ns, nc = s_len // tr, (r_len - s_len) // tr
    assert OFF_DZ % cw == 0

    def mirror(i):
        return jnp.where(i < ns, ns - 1 - i, 2 * ns + nc - 1 - i)

    return pl.pallas_call(
        _dnout_kernel,
        grid=(n_batch, r_len // tr, DN_V_W // cw),
        in_specs=[pl.BlockSpec((1, 1, tr, cw), lambda b, i, j: (0, b, i, j)),
                  pl.BlockSpec((1, 1, tr, cw), lambda b, i, j: (1, b, mirror(i), j)),
                  pl.BlockSpec((1, tr, cw), lambda b, i, j: (b, i, OFF_DZ // cw + j)),
                  pl.BlockSpec((1, DN_HEAD_DIM), lambda b, i, j: (0, 0)),
                  pl.BlockSpec((FLIP_ROWS, FLIP_ROWS), lambda b, i, j: (0, 0))],
        out_specs=pl.BlockSpec((1, tr, cw), lambda b, i, j: (b, i, j)),
        out_shape=jax.ShapeDtypeStruct((n_batch, r_len, DN_V_W), BF16),
        compiler_params=_cp(("parallel", "parallel", "parallel")),
        name="deltanet_out",
    )(o2, o2, p3, norm_g.reshape(1, DN_HEAD_DIM), _exchange_matrix())


def _seq_flip(a, s_len, axis):
    lat, ctx = jnp.split(a, [s_len], axis=axis)
    return jnp.concatenate([jnp.flip(lat, axis), jnp.flip(ctx, axis)], axis=axis)


def _deltanet_mixer(p3, conv_w, a_log, dt_bias, norm_g, *, s_len):
    n_batch, r_len, _ = p3.shape
    u2 = _dnprep_call(p3, conv_w, s_len=s_len)
    lg =p3[:, :, OFF_DL:OFF_DL + N_LOGITS].reshape(n_batch, r_len, 2, 2, DN_V_HEADS)

    def dirs(x):
        return jnp.stack([x[:, :, 0], _seq_flip(x[:, :, 1], s_len, 1)])

    la, lb = dirs(lg[:, :, 0]), dirs(lg[:, :, 1])
    padl = lambda x: jnp.pad(x, ((0, 0), (0, 0), (0, 0), (0, LANE - DN_V_HEADS)))
    a6, qk6, be, eg, ek, gt = _dnintra_call(u2, padl(la), padl(lb), jnp.swapaxes(la, 2, 3), a_log, dt_bias)
    t6 = _dnsolve_call(a6.reshape(-1, DN_CHUNK * DN_CHUNK)).reshape(a6.shape)
    o2 = _dnscan_call(u2, t6, qk6, be, eg, ek, gt, s_len=s_len)
    return _dnout_call(o2, p3, norm_g, s_len=s_len)


def _merge_kernel(ya_ref, yb_ref, yc_ref, wa_ref, wb_ref, wc_ref, ga_ref, gb_ref, gc_ref, o_ref,
                  wa_s, wb_s, wc_s):
    @pl.when(pl.program_id(1) == 0)
    def _():
        wa_s[...] = wa_ref[0].astype(BF16)
        wb_s[...] = wb_ref[0].astype(BF16)
        wc_s[...] = wc_ref[0].astype(BF16)

    m = (jax.nn.sigmoid(ga_ref[...]) * _dot(ya_ref[...], wa_s[...])
         + jax.nn.sigmoid(gb_ref[...]) * _dot(yb_ref[...], wb_s[...])
         + jax.nn.sigmoid(gc_ref[...]) * _dot(yc_ref[...], wc_s[...]))
    o_ref[...] = m.astype(o_ref.dtype)


def _merge_call(ya, yb, yc, w_pa, w_pb, w_pc, layer, p2, *, r_len):
    rt = ya.shape[0]
    d = w_pa.shape[2]
    tm = _pick(r_len, 1088, 16)
    tn = _pick(d, MERGE_ALIGN)
    yspec = lambda w: pl.BlockSpec((tm, w), lambda j, i: (i, 0))
    wspec = lambda w: pl.BlockSpec((1, w, tn), lambda j, i: (layer, 0, j))
    gspec = lambda br: pl.BlockSpec((tm, tn), lambda j, i, br=br: (i, br * d // tn + j))
    return pl.pallas_call(
        _merge_kernel,
        grid=(d // tn, rt // tm),
        in_specs=[yspec(ATT_W), yspec(HY_W), yspec(DN_V_W), wspec(ATT_W), wspec(HY_W), wspec(DN_V_W),
                  gspec(0), gspec(1), gspec(2)],
        out_specs=pl.BlockSpec((tm, tn), lambda j, i: (i, j)),
        out_shape=jax.ShapeDtypeStruct((rt, d), BF16),
        scratch_shapes=[pltpu.VMEM((ATT_W, tn), BF16), pltpu.VMEM((HY_W, tn), BF16),
                        pltpu.VMEM((DN_V_W, tn), BF16)],
        compiler_params=_cp(("parallel", "arbitrary"), 56),
        name="branch_merge",
    )(ya, yb, yc, w_pa, w_pb, w_pc, p2, p2, p2)


def _outproj_kernel(m_ref, w_ref, x_ref, gate_ref, o_ref, w_s, *, tm, tiles_per_batch, s_len, n_batch):
    @pl.when(pl.program_id(1) == 0)
    def _():
        w_s[...] = w_ref[0].astype(BF16)

    is_ctx, gl, gc = _row_mods(gate_ref, pl.program_id(1), 0, tm, tm, tiles_per_batch, s_len, n_batch)
    o_ref[...] = x_ref[...] + jnp.where(is_ctx, gc, gl) * _dot(m_ref[...], w_s[...])


def _outproj_call(m, w_out, layer, xs2, mods, *, n_batch, r_len, s_len):
    rt, d = xs2.shape
    tm = _pick(r_len, 1088, 16)
    tn = _pick(d, 512)
    kern = functools.partial(_outproj_kernel, tm=tm, tiles_per_batch=r_len // tm, s_len=s_len, n_batch=n_batch)
    return pl.pallas_call(
        kern,
        grid=(d // tn, rt // tm),
        in_specs=[pl.BlockSpec((tm, d), lambda j, i: (i, 0)),
                  pl.BlockSpec((1, d, tn), lambda j, i: (layer, 0, j)),
                  pl.BlockSpec((tm, tn), lambda j, i: (i, j)),
                  pl.BlockSpec((8, tn), lambda j, i: (0, 2 * d // tn + j))],
        out_specs=pl.BlockSpec((tm, tn), lambda j, i: (i, j)),
        out_shape=jax.ShapeDtypeStruct((rt, d), F32),
        scratch_shapes=[pltpu.VMEM((d, tn), BF16)],
        compiler_params=_cp(("parallel", "arbitrary")),
        name="out_proj_residual",
    )(m, w_out, xs2, mods)


def _finalnorm_kernel(x_ref, g_ref, o_ref):
    x = x_ref[0]
    o_ref[0] = x * lax.rsqrt(jnp.mean(x * x, axis=-1, keepdims=True) + NORM_EPS) * g_ref[...]


def _finalnorm_call(xs, final_g, *, s_len):
    n_batch, _, d = xs.shape
    tr = _pick(s_len, 512, 8)
    return pl.pallas_call(
        _finalnorm_kernel,
        grid=(n_batch, s_len // tr),
        in_specs=[pl.BlockSpec((1, tr, d), lambda b, i: (b, i, 0)), pl.BlockSpec((1, d), lambda b, i: (0, 0))],
        out_specs=pl.BlockSpec((1, tr, d), lambda b, i: (b, i, 0)),
        out_shape=jax.ShapeDtypeStruct((n_batch, s_len, d), F32),
        compiler_params=_cp(("parallel", "parallel")),
        name="final_norm",
    )(xs, final_g.reshape(1, d))


def kernel(x, c, ctx, c_ctx, norm_g, w_mod, b_mod, w_in, q_norm_g, k_norm_g, hy_conv_w, hy_conv_b, hy_w1, hy_b1, hy_freq1, hy_w2, hy_b2, hy_freq2, hy_w3, hy_d, dn_conv_w, dn_a_log, dn_dt_bias, dn_norm_g, w_pa, w_pb, w_pc, w_out, final_g):
    n_batch, s_len, d = x.shape
    ctx_len = ctx.shape[1]
    r_len = s_len + ctx_len
    depth = w_in.shape[0]
    assert n_batch + 1 <= 8 and w_in.shape[2] == OFF_DL + N_LOGITS + N_BRANCH * d

    xs = jnp.concatenate([x, ctx], axis=1)
    cs = jnp.zeros((8, d), F32).at[:n_batch].set(c).at[n_batch].set(c_ctx)
    mods = _mod_call(cs, w_mod, b_mod)
    cos_t, sin_t = _rope_tables(s_len, ctx_len)
    w1p = jnp.pad(hy_w1, ((0, 0), (0, LANE - HY_EMB), (0, 0)))
    w_head, w_gate = _split_w_in(w_in)

    for layer in range(depth):
        need_ctx = layer < depth - 1
        xs2 = xs.reshape(n_batch * r_len, d)
        h = _modnorm_call(xs2, norm_g[layer], mods[layer], n_batch=n_batch, r_len=r_len, s_len=s_len)
        p3 = _inproj_call(h, w_head, layer).reshape(n_batch, r_len, -1)
        p_gate = _inproj_call(h, w_gate, layer)

        ya = _attn_call(p3, cos_t, sin_t, q_norm_g[layer], k_norm_g[layer], s_len=s_len)

        filt = (w1p[layer], hy_b1[layer], hy_freq1[layer], hy_w2[layer], hy_b2[layer], hy_freq2[layer],
                hy_w3[layer])
        yb = _hyena_mixer(p3, 0, s_len, hy_conv_w[layer], hy_conv_b[layer], hy_d[layer], filt)
        if need_ctx:
            yb_c = _hyena_mixer(p3, s_len // ctx_len, ctx_len, hy_conv_w[layer], hy_conv_b[layer],
                                hy_d[layer], filt)
        else:
            yb_c = jnp.zeros((n_batch, ctx_len, HY_W), BF16)
        yb = jnp.concatenate([yb, yb_c], axis=1)

        yc = _deltanet_mixer(p3, dn_conv_w[layer], dn_a_log[layer], dn_dt_bias[layer], dn_norm_g[layer],
                             s_len=s_len)

        rt = n_batch * r_len
        m = _merge_call(ya.reshape(rt, ATT_W), yb.reshape(rt, HY_W), yc.reshape(rt, DN_V_W),
                        w_pa, w_pb, w_pc, layer, p_gate, r_len=r_len)
        xs = _outproj_call(m, w_out, layer, xs2, mods[layer],
                           n_batch=n_batch, r_len=r_len, s_len=s_len).reshape(n_batch, r_len, d)

    return _finalnorm_call(xs, final_g, s_len=s_len)
```
